```python
import numpy as np
import jax
import jax.numpy as jnp
from jax import lax

D_MODEL = 1024
BATCH = 4
SEQ = 4096
DEPTH = 4
DEC_BATCH = 128
DEC_SEQ = 4
PAST_LEN = 2048
PAGE_SIZE = 128

N_BRANCH = 4
BRANCH_W = D_MODEL // 4
H_A = 4
HD_A = BRANCH_W // H_A
Q_BLOCK = 128
SB_BIAS_INIT = -8.0
H_B = 4
DK_B = BRANCH_W // H_B
DV_B = BRANCH_W // H_B
CHUNK_B = 64
W_C = BRANCH_W
H_C = 4
CONV_C = 4
RG_C = 8.0
W_D = BRANCH_W
POOL_WINDOWS = (2, 4, 8, 16)
POOL_G = len(POOL_WINDOWS)
POOL_HIST = max(POOL_WINDOWS) - 1
D_FF = 2816
CONV_F = 3
D_PLE = 256
EPS = 1e-6
IN_SIZES = (BRANCH_W,) * 10 + (N_BRANCH * D_MODEL,)
F32 = jnp.float32

kernel_name = 'hybrid_sb_hgrn2_rglru_pool_decoder_step'


def rmsnorm(x, g):
    xf = x.astype(F32)
    y = xf * lax.rsqrt(jnp.mean(xf * xf, axis=-1, keepdims=True) + EPS)
    return (y * g.astype(F32)).astype(x.dtype)


def causal_dwconv(hist, x, w, b):
    width = w.shape[0]
    L = x.shape[1]
    ext = jnp.concatenate([hist.astype(x.dtype), x], axis=1)
    y = b + ext[:, 0:L] * w[0]
    for j in range(1, width):
        y = y + ext[:, j:j + L] * w[j]
    return y, ext[:, L:]


def sb_block(q, k, v, q_pos, k_pos, bias):
    z = jnp.einsum('bqhd,bkhd->bhqk', q, k) * (HD_A ** -0.5) + bias[None, :, None, None]
    mask = k_pos[None, :] < q_pos[:, None]
    log_keep = jnp.where(mask, jax.nn.log_sigmoid(-z), 0.0)
    later = lax.cumsum(log_keep, axis=3, reverse=True) - log_keep
    w = jnp.where(mask, jnp.exp(jax.nn.log_sigmoid(z) + later), 0.0)
    return jnp.einsum('bhqk,bkhd->bqhd', w, v)


def stick_breaking(q, k, v, pos0, bias):
    B, L, H, D = q.shape
    k_pos = jnp.arange(k.shape[1])
    q_pos = pos0 + jnp.arange(L)
    blk = Q_BLOCK if L % Q_BLOCK == 0 else L
    nb = L // blk
    kf, vf = k.astype(F32), v.astype(F32)
    bf = bias.astype(F32)
    qb = q.astype(F32).reshape(B, nb, blk, H, D).transpose(1, 0, 2, 3, 4)
    pb = q_pos.reshape(nb, blk)
    o = lax.map(lambda a: sb_block(a[0], kf, vf, a[1], k_pos, bf), (qb, pb))
    return o.transpose(1, 0, 2, 3, 4).reshape(B, L, H, D)


def gla_chunk_scan(q, k, v, log_f, S0, chunk):
    B, L, H, _ = q.shape
    DV = v.shape[-1]
    n = L // chunk

    def to_chunks(a):
        return a.reshape(B, n, chunk, H, a.shape[-1]).transpose(1, 0, 3, 2, 4)

    causal = jnp.tril(jnp.ones((chunk, chunk), dtype=bool))

    def step(S, inp):
        qc, kc, vc, lfc = inp
        b = jnp.cumsum(lfc, axis=2)
        diff = jnp.where(causal[None, None, :, :, None],
                         b[:, :, :, None, :] - b[:, :, None, :, :], -jnp.inf)
        att = jnp.einsum('bhtk,bhsk,bhtsk->bhts', qc, kc, jnp.exp(diff))
        o = jnp.einsum('bhts,bhsv->bhtv', att, vc) + jnp.einsum('bhtk,bhkv->bhtv', qc * jnp.exp(b), S)
        b_last = b[:, :, -1:, :]
        S_new = jnp.exp(b_last[:, :, 0, :])[..., None] * S + jnp.einsum(
            'bhsk,bhsv->bhkv', kc * jnp.exp(b_last - b), vc)
        return S_new, o

    S, o = lax.scan(step, S0, (to_chunks(q), to_chunks(k), to_chunks(v), to_chunks(log_f)))
    return o.transpose(1, 0, 3, 2, 4).reshape(B, L, H, DV), S


def hgrn2(qb, fb, ib, gb, S0, lp):
    B, L, _ = qb.shape
    lb = lp['lb']
    fr = fb.astype(F32)
    log_f = jnp.logaddexp(jnp.log(lb), jnp.log1p(-lb) + jax.nn.log_sigmoid(fr))
    kb = (1.0 - lb) * jax.nn.sigmoid(-fr)
    qh = jax.nn.silu(qb.astype(F32))
    to_h = lambda a: a.reshape(B, L, H_B, -1)
    chunk = CHUNK_B if L % CHUNK_B == 0 else L
    o, S = gla_chunk_scan(to_h(qh), to_h(kb), to_h(ib.astype(F32)), to_h(log_f), S0.astype(F32), chunk)
    o = rmsnorm(o, lp['hgrn_norm']).reshape(B, L, BRANCH_W) * jax.nn.silu(gb.astype(F32))
    return o, S


def rg_lru(xc, gc, conv_hist, h0, lp):
    xconv, conv_new = causal_dwconv(conv_hist, xc, lp['conv_c_w'], lp['conv_c_b'])
    xf = xconv.astype(F32)
    B, L, C = xf.shape
    xh = xf.reshape(B, L, H_C, C // H_C)
    r = jax.nn.sigmoid(jnp.einsum('blhi,hij->blhj', xh, lp['w_rg_a'].astype(F32)).reshape(B, L, C) + lp['b_rg_a'])
    ig = jax.nn.sigmoid(jnp.einsum('blhi,hij->blhj', xh, lp['w_rg_x'].astype(F32)).reshape(B, L, C) + lp['b_rg_x'])
    log_a = -RG_C * r * jax.nn.softplus(-lp['lam'].astype(F32))
    a = jnp.exp(log_a)
    u = jnp.sqrt(-jnp.expm1(2.0 * log_a)) * (ig * xf)

    def step(h, au):
        h = au[0] * h + au[1]
        return h, h

    hT, hs = lax.scan(step, h0.astype(F32), (a.transpose(1, 0, 2), u.transpose(1, 0, 2)))
    y = hs.transpose(1, 0, 2) * jax.nn.gelu(gc.astype(F32))
    return y, hT, conv_new


def pool_mixer(xd, hist, pos0, lp):
    B, L, C = xd.shape
    ext_raw = jnp.concatenate([hist.astype(xd.dtype), xd], axis=1)
    ext = ext_raw.astype(F32)
    cs = jnp.concatenate([jnp.zeros((B, 1, C), F32), jnp.cumsum(ext, axis=1)], axis=1)
    pos = pos0 + jnp.arange(L)
    gc = C // POOL_G
    start = POOL_HIST + 1
    pooled = []
    for g, w in enumerate(POOL_WINDOWS):
        sl = slice(g * gc, (g + 1) * gc)
        s = cs[:, start:start + L, sl] - cs[:, start - w:start - w + L, sl]
        cnt = jnp.minimum(pos + 1, w).astype(F32)[None, :, None]
        pooled.append(s / cnt)
    diff = jnp.concatenate(pooled, axis=-1) - ext[:, POOL_HIST:]
    y = jnp.einsum('blgi,gij->blgj', diff.reshape(B, L, POOL_G, gc), lp['w_pool'].astype(F32)).reshape(B, L, C)
    return y * lp['pool_scale'], ext_raw[:, L:]


def trunk_layer(x, pe, pos0, k_past, v_past, S0, h0, conv_c_hist, pool_hist, ffn_hist, lp):
    B, L, _ = x.shape
    dt = x.dtype
    h = rmsnorm(x, lp['norm_mix'])
    z = h @ lp['w_in']
    split_at = [int(s) for s in np.cumsum(IN_SIZES)[:-1]]
    qa, ka, va, qb, fb, ib, gb, xc, gc, xd, gl = jnp.split(z, split_at, axis=-1)
    qa = rmsnorm(qa.reshape(B, L, H_A, HD_A), lp['q_norm'])
    ka = rmsnorm(ka.reshape(B, L, H_A, HD_A), lp['k_norm'])
    va = va.reshape(B, L, H_A, HD_A)
    if k_past is None:
        k_all, v_all = ka, va
    else:
        k_all = jnp.concatenate([k_past.astype(dt), ka], axis=1)
        v_all = jnp.concatenate([v_past.astype(dt), va], axis=1)
    o_a = stick_breaking(qa, k_all, v_all, pos0, lp['sb_bias']).reshape(B, L, BRANCH_W)
    o_b, S_new = hgrn2(qb, fb, ib, gb, S0, lp)
    o_c, h_new, conv_c_new = rg_lru(xc, gc, conv_c_hist, h0, lp)
    o_d, pool_new = pool_mixer(xd, pool_hist, pos0, lp)
    branches = jnp.stack([o_a.astype(dt), o_b.astype(dt), o_c.astype(dt), o_d.astype(dt)], axis=2)
    proj = jnp.einsum('blnc,ncd->blnd', branches, lp['w_branch'])
    gates = jax.nn.sigmoid(gl.reshape(B, L, N_BRANCH, D_MODEL))
    x = x + jnp.sum(gates * proj, axis=2) @ lp['w_out']
    u = rmsnorm(x, lp['norm_ffn']) @ lp['w_up']
    u, ffn_new = causal_dwconv(ffn_hist, u, lp['conv_f_w'], lp['conv_f_b'])
    ua, ub = jnp.split(u, 2, axis=-1)
    x = x + (jax.nn.gelu(ua) * ub) @ lp['w_down']
    x = x + jax.nn.sigmoid(rmsnorm(x, lp['norm_ple']) @ lp['w_ple_gate']) * (pe.astype(dt) @ lp['w_ple'])
    return x, (ka, va, S_new, h_new, conv_c_new, pool_new, ffn_new)


def setup_inputs(seed: int = 0) -> dict:
    key = jax.random.key(seed)
    k = jax.random.split(key, 40)
    nrm = lambda kk, shape, scale=1.0: scale * jax.random.normal(kk, shape, F32)
    n_pages = PAST_LEN // PAGE_SIZE
    n_used = DEC_BATCH * n_pages
    n_pool = n_used + n_used // 4
    n_in = sum(IN_SIZES)
    page_table = jax.random.permutation(k[9], n_pool)[:n_used].reshape(DEC_BATCH, n_pages).astype(jnp.int32)
    a0 = jax.random.uniform(k[24], (DEPTH, W_C), F32, 0.9, 0.999)
    gc = W_D // POOL_G
    return {
        'x_prompt': nrm(k[0], (BATCH, SEQ, D_MODEL)),
        'x_sample': nrm(k[1], (DEC_BATCH, DEC_SEQ, D_MODEL)),
        'cache_k': nrm(k[2], (DEPTH, n_pool, PAGE_SIZE, H_A, HD_A)),
        'cache_v': nrm(k[3], (DEPTH, n_pool, PAGE_SIZE, H_A, HD_A)),
        'state_hgrn': nrm(k[4], (DEPTH, DEC_BATCH, H_B, DK_B, DV_B), 0.3),
        'state_rglru_h': nrm(k[5], (DEPTH, DEC_BATCH, W_C), 0.5),
        'state_rglru_conv': nrm(k[6], (DEPTH, DEC_BATCH, CONV_C - 1, W_C)),
        'state_pool': nrm(k[7], (DEPTH, DEC_BATCH, POOL_HIST, W_D)),
        'state_ffn_conv': nrm(k[8], (DEPTH, DEC_BATCH, CONV_F - 1, 2 * D_FF)),
        'page_table': page_table,
        'p_prompt': nrm(k[10], (DEPTH, BATCH, SEQ, D_PLE)),
        'p_sample': nrm(k[11], (DEPTH, DEC_BATCH, DEC_SEQ, D_PLE)),
        'norm_mix': 1.0 + nrm(k[12], (DEPTH, D_MODEL), 0.05),
        'w_in': nrm(k[13], (DEPTH, D_MODEL, n_in), D_MODEL ** -0.5),
        'q_norm': 1.0 + nrm(k[14], (DEPTH, HD_A), 0.05),
        'k_norm': 1.0 + nrm(k[15], (DEPTH, HD_A), 0.05),
        'sb_bias': SB_BIAS_INIT + nrm(k[37], (DEPTH, H_A), 0.1),
        'lb_logits': nrm(k[16], (DEPTH, H_B * DK_B), 0.5),
        'hgrn_norm': 1.0 + nrm(k[17], (DEPTH, DV_B), 0.05),
        'conv_c_w': nrm(k[18], (DEPTH, CONV_C, W_C), CONV_C ** -0.5),
        'conv_c_b': nrm(k[19], (DEPTH, W_C), 0.01),
        'w_rg_a': nrm(k[20], (DEPTH, H_C, W_C // H_C, W_C // H_C), (W_C // H_C) ** -0.5),
        'b_rg_a': nrm(k[21], (DEPTH, W_C), 0.01),
        'w_rg_x': nrm(k[22], (DEPTH, H_C, W_C // H_C, W_C // H_C), (W_C // H_C) ** -0.5),
        'b_rg_x': nrm(k[23], (DEPTH, W_C), 0.01),
        'lam': jnp.log(a0) - jnp.log1p(-a0),
        'w_pool': nrm(k[25], (DEPTH, POOL_G, gc, gc), gc ** -0.5),
        'pool_scale': 1.0 + nrm(k[26], (DEPTH, W_D), 0.05),
        'w_branch': nrm(k[27], (DEPTH, N_BRANCH, BRANCH_W, D_MODEL), BRANCH_W ** -0.5),
        'w_out': nrm(k[28], (DEPTH, D_MODEL, D_MODEL), D_MODEL ** -0.5),
        'norm_ffn': 1.0 + nrm(k[29], (DEPTH, D_MODEL), 0.05),
        'w_up': nrm(k[30], (DEPTH, D_MODEL, 2 * D_FF), D_MODEL ** -0.5),
        'conv_f_w': nrm(k[31], (DEPTH, CONV_F, 2 * D_FF), CONV_F ** -0.5),
        'conv_f_b': nrm(k[32], (DEPTH, 2 * D_FF), 0.01),
        'w_down': nrm(k[33], (DEPTH, D_FF, D_MODEL), D_FF ** -0.5),
        'norm_ple': 1.0 + nrm(k[34], (DEPTH, D_MODEL), 0.05),
        'w_ple_gate': nrm(k[35], (DEPTH, D_MODEL, D_MODEL), D_MODEL ** -0.5),
        'w_ple': nrm(k[36], (DEPTH, D_PLE, D_MODEL), D_PLE ** -0.5),
    }


def reference(x_prompt, x_sample, cache_k, cache_v, state_hgrn, state_rglru_h, state_rglru_conv,
              state_pool, state_ffn_conv, page_table, p_prompt, p_sample,
              norm_mix, w_in, q_norm, k_norm, sb_bias, lb_logits, hgrn_norm, conv_c_w, conv_c_b,
              w_rg_a, b_rg_a, w_rg_x, b_rg_x, lam, w_pool, pool_scale, w_branch, w_out,
              norm_ffn, w_up, conv_f_w, conv_f_b, w_down, norm_ple, w_ple_gate, w_ple):
    lb_all = jnp.cumsum(jax.nn.softmax(lb_logits.astype(F32), axis=0), axis=0)
    lb_all = lb_all - lb_all[:1]
    B = x_prompt.shape[0]
    DB = x_sample.shape[0]
    dt = x_prompt.dtype
    S0_p = jnp.zeros((B, H_B, DK_B, DV_B), F32)
    h0_p = jnp.zeros((B, W_C), F32)
    convc_p = jnp.zeros((B, CONV_C - 1, W_C), dt)
    pool_p = jnp.zeros((B, POOL_HIST, W_D), dt)
    ffn_p = jnp.zeros((B, CONV_F - 1, 2 * D_FF), dt)
    xp, xs = x_prompt, x_sample
    outs_p, outs_s = [], []
    for i in range(DEPTH):
        lp = {'norm_mix': norm_mix[i], 'w_in': w_in[i], 'q_norm': q_norm[i], 'k_norm': k_norm[i],
              'sb_bias': sb_bias[i],
              'lb': lb_all[i], 'hgrn_norm': hgrn_norm[i], 'conv_c_w': conv_c_w[i], 'conv_c_b': conv_c_b[i],
              'w_rg_a': w_rg_a[i], 'b_rg_a': b_rg_a[i], 'w_rg_x': w_rg_x[i], 'b_rg_x': b_rg_x[i],
              'lam': lam[i], 'w_pool': w_pool[i], 'pool_scale': pool_scale[i], 'w_branch': w_branch[i],
              'w_out': w_out[i], 'norm_ffn': norm_ffn[i], 'w_up': w_up[i], 'conv_f_w': conv_f_w[i],
              'conv_f_b': conv_f_b[i], 'w_down': w_down[i], 'norm_ple': norm_ple[i],
              'w_ple_gate': w_ple_gate[i], 'w_ple': w_ple[i]}
        xp, sp = trunk_layer(xp, p_prompt[i], 0, None, None, S0_p, h0_p, convc_p, pool_p, ffn_p, lp)
        k_past = cache_k[i][page_table].reshape(DB, -1, H_A, HD_A)
        v_past = cache_v[i][page_table].reshape(DB, -1, H_A, HD_A)
        xs, ss = trunk_layer(xs, p_sample[i], k_past.shape[1], k_past, v_past, state_hgrn[i],
                             state_rglru_h[i], state_rglru_conv[i], state_pool[i], state_ffn_conv[i], lp)
        outs_p.append(sp)
        outs_s.append(ss)
    stk = lambda outs, j: jnp.stack([o[j] for o in outs], axis=0)
    return (xp, xs,
            stk(outs_p, 0), stk(outs_p, 1), stk(outs_s, 0), stk(outs_s, 1),
            stk(outs_p, 2), stk(outs_s, 2),
            stk(outs_p, 3), stk(outs_s, 3),
            stk(outs_p, 4), stk(outs_s, 4),
            stk(outs_p, 5), stk(outs_s, 5),
            stk(outs_p, 6), stk(outs_s, 6))
```

```python
import functools
import math

import jax
import jax.numpy as jnp
from jax import lax
from jax.experimental import pallas as pl
from jax.experimental.pallas import tpu as pltpu

F32 = jnp.float32
BF16 = jnp.bfloat16
EPS = 1e-6

N_HEAD = 4
HEAD_W = 64
BRANCH_W = N_HEAD * HEAD_W
N_BRANCH = 4
RG_C = 8.0
POOL_WINDOWS = (2, 4, 8, 16)
POOL_HIST = max(POOL_WINDOWS) - 1
CONV_C = 4
CONV_F = 3
HGRN_CHUNK = 64
HGRN_SUB = 16
NEG_BIG = -1e30

V7X_VMEM_LIMIT_BYTES = 56 * 1024 * 1024

TM_PROJ = 512
TQ_ATTN = 256
TL_SCAN = 512
TL_HGRN = 256
TM_FFN = 1024
TF_FFN = 256
FFN_HALO = 16


def _params(*sem):
    return pltpu.CompilerParams(dimension_semantics=sem, vmem_limit_bytes=V7X_VMEM_LIMIT_BYTES)


def _dot(a, b):
    return jnp.dot(a, b, preferred_element_type=F32)


def _dot_nt(a, b):
    return lax.dot_general(a, b, (((1,), (1,)), ((), ())), preferred_element_type=F32)


def _dot_tn(a, b):
    return lax.dot_general(a, b, (((0,), (0,)), ((), ())), preferred_element_type=F32)


def _split2(x):
    hi = x.astype(BF16)
    lo = (x - hi.astype(F32)).astype(BF16)
    return hi, lo


def _split3(x):
    hi = x.astype(BF16)
    r = x - hi.astype(F32)
    mid = r.astype(BF16)
    lo = (r - mid.astype(F32)).astype(BF16)
    return hi, mid, lo


def _dot_x2(x, m):
    hi, lo = _split2(x)
    return _dot(hi, m) + _dot(lo, m)


def _sigmoid(x):
    return 1.0 / (1.0 + jnp.exp(-x))


def _softplus_tail(x):
    return jnp.log1p(jnp.exp(-jnp.abs(x)))


def _log_sigmoid(x):
    return jnp.minimum(x, 0.0) - _softplus_tail(x)


def _softplus(x):
    return jnp.maximum(x, 0.0) + _softplus_tail(x)


def _silu(x):
    return x * _sigmoid(x)


def _gelu(x):
    c = math.sqrt(2.0 / math.pi)
    return 0.5 * x * (1.0 + jnp.tanh(c * (x + 0.044715 * (x * x * x))))


def _rms(x, g):
    ms = jnp.mean(x * x, axis=-1, keepdims=True)
    return x * lax.rsqrt(ms + EPS) * g


def _head_id(shape, dim):
    return lax.shift_right_logical(lax.broadcasted_iota(jnp.int32, shape, dim), 6)


def _head_block_ones():
    n = BRANCH_W
    return jnp.where(_head_id((n, n), 0) == _head_id((n, n), 1), 1.0, 0.0).astype(BF16)


def _head_rms(a, g, bd):
    ms = _dot_x2(a * a, bd) * (1.0 / HEAD_W)
    return a * lax.rsqrt(ms + EPS) * g


def _shift_rows(x, d, fill):
    rolled = pltpu.roll(x, d, 0)
    row = lax.broadcasted_iota(jnp.int32, x.shape, 0)
    return jnp.where(row < d, fill, rolled)


def _lb_kernel(lg_ref, lb_ref, loglb_ref, log1m_ref):
    x = lg_ref[...]
    depth = x.shape[0]
    rows = [x[i:i + 1] for i in range(depth)]
    m = functools.reduce(jnp.maximum, rows)
    e = [jnp.exp(r - m) for r in rows]
    tot = functools.reduce(lambda a, b: a + b, e)
    zero = jnp.zeros_like(m)
    lb_ref[0:1, :] = zero
    loglb_ref[0:1, :] = jnp.full_like(m, -jnp.inf)
    log1m_ref[0:1, :] = zero
    acc = zero
    for i in range(1, depth):
        acc = acc + e[i] / tot
        lb_ref[i:i + 1, :] = acc
        loglb_ref[i:i + 1, :] = jnp.log(acc)
        log1m_ref[i:i + 1, :] = jnp.log1p(-acc)


def _lower_bounds(lb_logits):
    shp = jax.ShapeDtypeStruct(lb_logits.shape, F32)
    return pl.pallas_call(_lb_kernel, out_shape=(shp, shp, shp), name="hgrn_lower_bounds")(lb_logits)


def _inproj_kernel(x_ref, g_ref, w_ref, qn_ref, kn_ref,
                   q_o, k_o, v_o, kb_o, vb_o, zb_o, zc_o, zd_o):
    h = _rms(x_ref[...], g_ref[...]).astype(BF16)
    z = _dot(h, w_ref[...])
    bd = _head_block_ones()
    w = BRANCH_W
    q = _head_rms(z[:, 0:w], qn_ref[...], bd)
    k = _head_rms(z[:, w:2 * w], kn_ref[...], bd)
    v = z[:, 2 * w:3 * w]
    q_o[...] = (q * (HEAD_W ** -0.5)).astype(q_o.dtype)
    k_o[...] = k
    v_o[...] = v
    kb_o[...] = k.astype(BF16)
    vb_o[...] = v.astype(BF16)
    zb_o[...] = z[:, 3 * w:7 * w]
    zc_o[...] = z[:, 7 * w:9 * w]
    zd_o[...] = z[:, 9 * w:10 * w]


def _inproj(x2, g, w, qn, kn, q_dtype):
    t, d = x2.shape
    tm = min(TM_PROJ, t)
    n = w.shape[1]
    bw = BRANCH_W
    row = lambda i: (i, 0)
    fix = lambda i: (0, 0)
    outs = [(bw, q_dtype), (bw, F32), (bw, F32), (bw, BF16), (bw, BF16), (4 * bw, F32), (2 * bw, F32), (bw, F32)]
    return pl.pallas_call(
        _inproj_kernel,
        grid=(t // tm,),
        in_specs=[pl.BlockSpec((tm, d), row), pl.BlockSpec((1, d), fix), pl.BlockSpec((d, n), fix),
                  pl.BlockSpec((1, bw), fix), pl.BlockSpec((1, bw), fix)],
        out_specs=[pl.BlockSpec((tm, c), row) for c, _ in outs],
        out_shape=[jax.ShapeDtypeStruct((t, c), dt) for c, dt in outs],
        compiler_params=_params("parallel"),
        name="inproj",
    )(x2, g, w, qn, kn)


def _sb_tile(qh, kt, vt, bias, carry, upper, mask):
    z = _dot_nt(qh, kt) + bias
    tail = _softplus_tail(z)
    log_beta = jnp.minimum(z, 0.0) - tail
    log_keep = -jnp.maximum(z, 0.0) - tail
    if mask is not None:
        log_keep = jnp.where(mask, log_keep, 0.0)
    local = _dot_x2(log_keep, upper)
    w = jnp.exp(log_beta + local + carry)
    if mask is not None:
        w = jnp.where(mask, w, 0.0)
    pv = _dot(w.astype(BF16), vt)
    return pv, carry + local[:, 0:1] + log_keep[:, 0:1]


def _attn_prompt_kernel(bias_ref, q_ref, k_ref, v_ref, o_ref):
    qi, h = pl.program_id(1), pl.program_id(2)
    tq = q_ref.shape[1]
    q = q_ref[0]
    r = lax.broadcasted_iota(jnp.int32, (tq, tq), 0)
    c = lax.broadcasted_iota(jnp.int32, (tq, tq), 1)
    upper = jnp.where(r > c, 1.0, 0.0).astype(BF16)
    causal = c < r
    in_head = _head_id((1, BRANCH_W), 1) == h
    qh = jnp.where(in_head, q, jnp.zeros_like(q))
    bias = bias_ref[h]
    k0 = pl.multiple_of(qi * tq, tq)
    pv, carry = _sb_tile(qh, k_ref[0, pl.ds(k0, tq), :], v_ref[0, pl.ds(k0, tq), :], bias,
                         jnp.zeros((tq, 1), F32), upper, causal)

    def body(j, state):
        acc, carry = state
        ks = pl.multiple_of((qi - j) * tq, tq)
        pv, carry = _sb_tile(qh, k_ref[0, pl.ds(ks, tq), :], v_ref[0, pl.ds(ks, tq), :], bias,
                             carry, upper, None)
        return acc + pv, carry

    acc, _ = lax.fori_loop(1, qi + 1, body, (pv, carry))

    @pl.when(h == 0)
    def _():
        o_ref[0] = jnp.zeros_like(o_ref[0])

    o_ref[0] += jnp.where(in_head, acc, 0.0)


def _attn_prompt(bias, qb, kb, vb):
    b, l, w = qb.shape
    tq = min(TQ_ATTN, l)
    return pl.pallas_call(
        _attn_prompt_kernel,
        grid=(b, l // tq, N_HEAD),
        in_specs=[pl.BlockSpec(memory_space=pltpu.SMEM),
                  pl.BlockSpec((1, tq, w), lambda i, j, h: (i, j, 0)),
                  pl.BlockSpec((1, l, w), lambda i, j, h: (i, 0, 0)),
                  pl.BlockSpec((1, l, w), lambda i, j, h: (i, 0, 0))],
        out_specs=pl.BlockSpec((1, tq, w), lambda i, j, h: (i, j, 0)),
        out_shape=jax.ShapeDtypeStruct((b, l, w), F32),
        compiler_params=_params("parallel", "arbitrary", "arbitrary"),
        name="attn_prompt",
    )(bias, qb, kb, vb)


def _attn_sample_kernel(pt_ref, bias_ref, q_ref, kn_ref, vn_ref, *refs, n_pages, n_new):
    del pt_ref
    k_refs, v_refs, o_ref = refs[:n_pages], refs[n_pages:2 * n_pages], refs[2 * n_pages]
    page = k_refs[0].shape[2]
    rows = N_HEAD * 8
    row = lax.broadcasted_iota(jnp.int32, (rows, 1), 0)
    row_t = jnp.bitwise_and(row, 7)
    row_h = lax.shift_right_logical(row, 3)
    lane_head = _head_id((1, BRANCH_W), 1)
    qs = jnp.zeros((rows, BRANCH_W), F32)
    for t in range(n_new):
        qs = qs + jnp.where(row_t == t, q_ref[t, 0], 0.0)
    qs = jnp.where(row_h == lane_head, qs, 0.0)
    bias = jnp.zeros((rows, 1), F32)
    for h in range(N_HEAD):
        bias = jnp.where(row_h == h, bias_ref[h], bias)

    carry = jnp.zeros((rows, 1), F32)
    out = jnp.zeros((rows, BRANCH_W), F32)
    for j in reversed(range(n_new)):
        z = jnp.sum(qs * kn_ref[j, 0], axis=-1, keepdims=True) + bias
        tail = _softplus_tail(z)
        seen = row_t > j
        w = jnp.where(seen, jnp.exp(jnp.minimum(z, 0.0) - tail + carry), 0.0)
        out = out + w * vn_ref[j, 0]
        carry = carry + jnp.where(seen, -jnp.maximum(z, 0.0) - tail, 0.0)

    qs16 = qs.astype(BF16)
    r = lax.broadcasted_iota(jnp.int32, (page, page), 0)
    c = lax.broadcasted_iota(jnp.int32, (page, page), 1)
    upper = jnp.where(r > c, 1.0, 0.0).astype(BF16)
    for p in reversed(range(n_pages)):
        pv, carry = _sb_tile(qs16, k_refs[p][0, 0].astype(BF16), v_refs[p][0, 0].astype(BF16), bias,
                             carry, upper, None)
        out = out + pv

    res = jnp.zeros((8, BRANCH_W), F32)
    for h in range(N_HEAD):
        res = res + jnp.where(lane_head == h, out[h * 8:(h + 1) * 8], 0.0)
    for t in range(n_new):
        o_ref[t, 0] = res[t:t + 1]


def _attn_sample(layer, page_table, bias, q4, kn4, vn4, cache_k4, cache_v4):
    n_new, db, _, w = q4.shape
    n_pages = page_table.shape[1]
    page = cache_k4.shape[2]
    new_spec = pl.BlockSpec((n_new, 1, 1, w), lambda b, pt, bs: (0, b, 0, 0))

    def page_spec(p):
        return pl.BlockSpec((1, 1, page, w), lambda b, pt, bs, p=p: (layer, pt[b, p], 0, 0))

    grid_spec = pltpu.PrefetchScalarGridSpec(
        num_scalar_prefetch=2,
        grid=(db,),
        in_specs=[new_spec, new_spec, new_spec] + [page_spec(p) for p in range(n_pages)] * 2,
        out_specs=new_spec,
    )
    return pl.pallas_call(
        functools.partial(_attn_sample_kernel, n_pages=n_pages, n_new=n_new),
        grid_spec=grid_spec,
        out_shape=jax.ShapeDtypeStruct(q4.shape, F32),
        compiler_params=_params("arbitrary"),
        name="attn_sample",
    )(page_table, bias, q4, kn4, vn4, *([cache_k4] * n_pages), *([cache_v4] * n_pages))


def _hgrn_gates(qb, fr, lb, loglb, log1m):
    c = log1m + _log_sigmoid(fr)
    log_f = jnp.maximum(loglb, c) + jnp.log1p(jnp.exp(-jnp.abs(loglb - c)))
    key = (1.0 - lb) * _sigmoid(-fr)
    return _silu(qb), key, log_f


def _hgrn_prompt_kernel(z_ref, lb_ref, loglb_ref, log1m_ref, gn_ref, o_ref, s_ref,
                        st_scr, q_scr, k_scr, v_scr, b_scr, p_scr, vx_scr, od_scr):
    ti = pl.program_id(1)
    w = BRANCH_W
    ck, sub = HGRN_CHUNK, HGRN_SUB
    n_sub = ck // sub
    tl = z_ref.shape[1]

    @pl.when(ti == 0)
    def _():
        st_scr[...] = jnp.zeros_like(st_scr)

    bd = _head_block_ones()
    bd_mask = _head_id((w, w), 0) == _head_id((w, w), 1)
    r = lax.broadcasted_iota(jnp.int32, (ck, ck), 0)
    c = lax.broadcasted_iota(jnp.int32, (ck, ck), 1)
    lower_incl = jnp.where(c <= r, 1.0, 0.0).astype(BF16)
    row_ck = lax.broadcasted_iota(jnp.int32, (ck, 1), 0)
    row_sub = lax.broadcasted_iota(jnp.int32, (sub, 1), 0)
    stack_mask = (lax.shift_right_logical(lax.broadcasted_iota(jnp.int32, (N_HEAD * sub, w), 0), 4)
                  == _head_id((N_HEAD * sub, w), 1))
    lb, loglb, log1m, gn = lb_ref[...], loglb_ref[...], log1m_ref[...], gn_ref[...]

    def chunk(ci, _):
        r0 = pl.multiple_of(ci * ck, ck)
        zz = z_ref[0, pl.ds(r0, ck), :]
        qh, key, log_f = _hgrn_gates(zz[:, 0:w], zz[:, w:2 * w], lb, loglb, log1m)
        val = zz[:, 2 * w:3 * w]
        f_hi, f_mid, f_lo = _split3(log_f)
        b = _dot(lower_incl, f_hi) + _dot(lower_incl, f_mid) + _dot(lower_incl, f_lo)
        q_scr[...] = qh
        k_scr[...] = key
        v_scr[...] = val
        b_scr[...] = b

        st = st_scr[...]
        out = _dot_nt((qh * jnp.exp(b)).astype(BF16), st.astype(BF16))
        for j in range(n_sub - 1):
            e_j = b[(j + 1) * sub - 1:(j + 1) * sub]
            qj = qh * jnp.exp(jnp.where(row_ck >= (j + 1) * sub, b - e_j, NEG_BIG))
            kj = key[j * sub:(j + 1) * sub] * jnp.exp(e_j - b[j * sub:(j + 1) * sub])
            vj = val[j * sub:(j + 1) * sub]
            k_st = jnp.where(stack_mask, jnp.concatenate([kj] * N_HEAD, axis=0), 0.0)
            v_st = jnp.where(stack_mask, jnp.concatenate([vj] * N_HEAD, axis=0), 0.0)
            att = _dot_nt(qj.astype(BF16), k_st.astype(BF16))
            out = out + _dot(att.astype(BF16), v_st.astype(BF16))

        b_last = b[ck - 1:ck]
        k_end = key * jnp.exp(b_last - b)
        st_new = st * jnp.exp(b_last) + _dot_tn(val.astype(BF16), k_end.astype(BF16))
        st_scr[...] = jnp.where(bd_mask, st_new, 0.0)

        def diag(si, _):
            s0 = pl.multiple_of(si * sub, sub)
            q_i = q_scr[pl.ds(s0, sub), :]
            b_i = b_scr[pl.ds(s0, sub), :]
            for s in range(sub):
                k_s = k_scr[pl.ds(s0 + s, 1), :]
                b_s = b_scr[pl.ds(s0 + s, 1), :]
                v_s = v_scr[pl.ds(s0 + s, 1), :]
                p_scr[s * sub:(s + 1) * sub, :] = q_i * k_s * jnp.exp(jnp.where(row_sub >= s, b_i - b_s, NEG_BIG))
                vx_scr[s * sub:(s + 1) * sub, :] = jnp.broadcast_to(v_s, (sub, w))
            att = _dot_x2(p_scr[...], bd)
            od_scr[pl.ds(s0, sub), :] = jnp.sum((att * vx_scr[...]).reshape(sub, sub, w), axis=0)
            return 0

        lax.fori_loop(0, n_sub, diag, 0)
        out = out + od_scr[...]
        o_ref[0, pl.ds(r0, ck), :] = _head_rms(out, gn, bd) * _silu(zz[:, 3 * w:4 * w])
        return 0

    lax.fori_loop(0, tl // ck, chunk, 0)

    @pl.when(ti == pl.num_programs(1) - 1)
    def _():
        s_ref[0] = st_scr[...].T


def _hgrn_prompt(zb3, lb, loglb, log1m, gn):
    b, l, _ = zb3.shape
    w = BRANCH_W
    tl = min(TL_HGRN, l)
    assert l % tl == 0 and tl % HGRN_CHUNK == 0
    fix = lambda i, j: (0, 0)
    scr = lambda rows: pltpu.VMEM((rows, w), F32)
    return pl.pallas_call(
        _hgrn_prompt_kernel,
        grid=(b, l // tl),
        in_specs=[pl.BlockSpec((1, tl, 4 * w), lambda i, j: (i, j, 0))] + [pl.BlockSpec((1, w), fix)] * 4,
        out_specs=[pl.BlockSpec((1, tl, w), lambda i, j: (i, j, 0)),
                   pl.BlockSpec((1, w, w), lambda i, j: (i, 0, 0))],
        out_shape=[jax.ShapeDtypeStruct((b, l, w), F32), jax.ShapeDtypeStruct((b, w, w), F32)],
        scratch_shapes=[scr(w), scr(HGRN_CHUNK), scr(HGRN_CHUNK), scr(HGRN_CHUNK), scr(HGRN_CHUNK),
                        scr(HGRN_SUB * HGRN_SUB), scr(HGRN_SUB * HGRN_SUB), scr(HGRN_CHUNK)],
        compiler_params=_params("parallel", "arbitrary"),
        name="hgrn_prompt",
    )(zb3, lb, loglb, log1m, gn)


def _hgrn_sample_kernel(z_ref, s0_ref, lb_ref, gn_ref, o_ref, s_ref, f_scr, k_scr, q_scr, v_scr, o_scr,
                        *, n_new, db):
    j = pl.program_id(0)
    w = BRANCH_W
    cols = s0_ref.shape[1]
    k_per_step = cols // HEAD_W
    steps_per_head = HEAD_W // k_per_step
    lb = lb_ref[...]

    @pl.when(j == 0)
    def _():
        for t in range(n_new):
            zz = z_ref[t * db:(t + 1) * db, :]
            fr = zz[:, w:2 * w]
            f_scr[t] = (lb + (1.0 - lb) * _sigmoid(fr)).T
            k_scr[t] = ((1.0 - lb) * _sigmoid(-fr)).T
            q_scr[t] = _silu(zz[:, 0:w]).T
            v_scr[t] = zz[:, 2 * w:3 * w].T
            o_scr[t] = jnp.zeros((w, db), F32)

    head = j // steps_per_head
    k_base = head * HEAD_W + (j % steps_per_head) * k_per_step
    v0 = pl.multiple_of(head * HEAD_W, HEAD_W)
    per_blk = 128 // HEAD_W
    for sb in range(cols // 128):
        blk = s0_ref[:, sb * 128:(sb + 1) * 128].T
        parts = []
        for kk in range(per_blk):
            s_k = blk[kk * HEAD_W:(kk + 1) * HEAD_W]
            row = k_base + sb * per_blk + kk
            for t in range(n_new):
                s_k = (f_scr[t, pl.ds(row, 1), :] * s_k
                       + k_scr[t, pl.ds(row, 1), :] * v_scr[t, pl.ds(v0, HEAD_W), :])
                o_scr[t, pl.ds(v0, HEAD_W), :] += q_scr[t, pl.ds(row, 1), :] * s_k
            parts.append(s_k)
        s_ref[:, sb * 128:(sb + 1) * 128] = jnp.concatenate(parts, axis=0).T

    @pl.when(j == pl.num_programs(0) - 1)
    def _():
        bd = _head_block_ones()
        for t in range(n_new):
            gate = _silu(z_ref[t * db:(t + 1) * db, 3 * w:4 * w])
            o_ref[t * db:(t + 1) * db, :] = _head_rms(o_scr[t].T, gn_ref[...], bd) * gate


def _hgrn_sample(layer, zb, state4, lb, gn, n_new):
    t, _ = zb.shape
    db = t // n_new
    w = BRANCH_W
    n_state = state4.shape[2]
    cols = 1024
    fix = lambda j: (0, 0)
    scr = pltpu.VMEM((n_new, w, db), F32)
    return pl.pallas_call(
        functools.partial(_hgrn_sample_kernel, n_new=n_new, db=db),
        grid=(n_state // cols,),
        in_specs=[pl.BlockSpec((t, 4 * w), fix),
                  pl.BlockSpec((None, db, cols), lambda j: (layer, 0, j)),
                  pl.BlockSpec((1, w), fix), pl.BlockSpec((1, w), fix)],
        out_specs=[pl.BlockSpec((t, w), fix), pl.BlockSpec((db, cols), lambda j: (0, j))],
        out_shape=[jax.ShapeDtypeStruct((t, w), F32), jax.ShapeDtypeStruct((db, n_state), F32)],
        scratch_shapes=[scr] * 5,
        compiler_params=_params("arbitrary"),
        name="hgrn_sample",
    )(zb, state4, lb, gn)


def _rglru_gates(xconv, wg, bg, sp_lam):
    w = BRANCH_W
    g = _sigmoid(_dot(xconv.astype(BF16), wg) + bg)
    log_a = -RG_C * g[:, 0:w] * sp_lam
    a = jnp.exp(log_a)
    one_minus_a2 = -jnp.tanh(log_a) * (a * a + 1.0)
    u = jnp.sqrt(one_minus_a2) * (g[:, w:2 * w] * xconv)
    return a, u


def _rglru_prompt_kernel(z_ref, cw_ref, cb_ref, wg_ref, bg_ref, lam_ref, o_ref, h_ref, ext_scr, h_scr):
    ti = pl.program_id(1)
    w = BRANCH_W
    tl = z_ref.shape[1]
    halo = 8

    @pl.when(ti == 0)
    def _():
        ext_scr[0:halo, :] = jnp.zeros((halo, w), F32)
        h_scr[...] = jnp.zeros_like(h_scr)

    x = z_ref[0, :, 0:w]
    ext_scr[halo:, :] = x
    xconv = cb_ref[...] + cw_ref[CONV_C - 1:CONV_C, :] * x
    for j in range(CONV_C - 1):
        xconv = xconv + cw_ref[j:j + 1, :] * ext_scr[pl.ds(halo - (CONV_C - 1) + j, tl), :]
    ext_scr[0:halo, :] = x[tl - halo:tl]
    a, u = _rglru_gates(xconv, wg_ref[...], bg_ref[...], _softplus(-lam_ref[...]))
    d = 1
    while d < tl:
        u = u + a * _shift_rows(u, d, 0.0)
        a = a * _shift_rows(a, d, 1.0)
        d *= 2
    h = u + a * h_scr[0:1, :]
    h_last = h[tl - 1:tl]
    h_scr[...] = jnp.broadcast_to(h_last, h_scr.shape)
    h_ref[0] = h_last
    o_ref[0] = h * _gelu(z_ref[0, :, w:2 * w])


def _rglru_prompt(zc3, cw, cb, wg, bg, lam):
    b, l, _ = zc3.shape
    w = BRANCH_W
    tl = min(TL_SCAN, l)
    fix = lambda i, j: (0, 0)
    return pl.pallas_call(
        _rglru_prompt_kernel,
        grid=(b, l // tl),
        in_specs=[pl.BlockSpec((1, tl, 2 * w), lambda i, j: (i, j, 0)),
                  pl.BlockSpec((CONV_C, w), fix), pl.BlockSpec((1, w), fix),
                  pl.BlockSpec((w, 2 * w), fix), pl.BlockSpec((1, 2 * w), fix), pl.BlockSpec((1, w), fix)],
        out_specs=[pl.BlockSpec((1, tl, w), lambda i, j: (i, j, 0)),
                   pl.BlockSpec((1, 1, w), lambda i, j: (i, 0, 0))],
        out_shape=[jax.ShapeDtypeStruct((b, l, w), F32), jax.ShapeDtypeStruct((b, 1, w), F32)],
        scratch_shapes=[pltpu.VMEM((tl + 8, w), F32), pltpu.VMEM((8, w), F32)],
        compiler_params=_params("parallel", "arbitrary"),
        name="rglru_prompt",
    )(zc3, cw, cb, wg, bg, lam)


def _pool_select(sums, x, pos, wp, scale):
    lane_group = _head_id((1, BRANCH_W), 1)
    pooled = jnp.zeros_like(x)
    for g, win in enumerate(POOL_WINDOWS):
        cnt = jnp.minimum(pos + 1, win).astype(F32)
        pooled = jnp.where(lane_group == g, sums[g] / cnt, pooled)
    return _dot((pooled - x).astype(BF16), wp) * scale


def _pool_prompt_kernel(x_ref, wp_ref, sc_ref, o_ref, ext_scr):
    ti = pl.program_id(1)
    w = BRANCH_W
    tl = x_ref.shape[1]
    halo = 16

    @pl.when(ti == 0)
    def _():
        ext_scr[0:halo, :] = jnp.zeros((halo, w), F32)

    x = x_ref[0]
    ext_scr[halo:, :] = x
    e = ext_scr[...]
    sums = []
    d = 1
    for _ in POOL_WINDOWS:
        e = e + pltpu.roll(e, d, 0)
        sums.append(e[halo:])
        d *= 2
    ext_scr[0:halo, :] = x[tl - halo:tl]
    pos = ti * tl + lax.broadcasted_iota(jnp.int32, (tl, 1), 0)
    o_ref[0] = _pool_select(sums, x, pos, wp_ref[...], sc_ref[...])


def _pool_prompt(zd3, wp, scale):
    b, l, w = zd3.shape
    tl = min(TL_SCAN, l)
    fix = lambda i, j: (0, 0)
    return pl.pallas_call(
        _pool_prompt_kernel,
        grid=(b, l // tl),
        in_specs=[pl.BlockSpec((1, tl, w), lambda i, j: (i, j, 0)),
                  pl.BlockSpec((w, w), fix), pl.BlockSpec((1, w), fix)],
        out_specs=pl.BlockSpec((1, tl, w), lambda i, j: (i, j, 0)),
        out_shape=jax.ShapeDtypeStruct((b, l, w), F32),
        scratch_shapes=[pltpu.VMEM((tl + 16, w), F32)],
        compiler_params=_params("parallel", "arbitrary"),
        name="pool_prompt",
    )(zd3, wp, scale)


def _cd_sample_kernel(zc_ref, zd_ref, h0_ref, ch_ref, ph_ref, cw_ref, cb_ref, wg_ref, bg_ref, lam_ref,
                      wp_ref, sc_ref, oc_ref, od_ref, h_ref, cn_ref, pn_ref, xc_scr, *, n_new, db, pos0):
    w = BRANCH_W
    slab = lambda ref, t, c0=0: ref[t * db:(t + 1) * db, c0:c0 + w]
    hist = lambda ref, j: ref[:, j * w:(j + 1) * w]

    n_hist = CONV_C - 1
    ext = [hist(ch_ref, j) for j in range(n_hist)] + [slab(zc_ref, t) for t in range(n_new)]
    for t in range(n_new):
        acc = cb_ref[...] + cw_ref[0:1, :] * ext[t]
        for j in range(1, CONV_C):
            acc = acc + cw_ref[j:j + 1, :] * ext[t + j]
        xc_scr[t * db:(t + 1) * db, :] = acc
    a, u = _rglru_gates(xc_scr[...], wg_ref[...], bg_ref[...], _softplus(-lam_ref[...]))
    h = h0_ref[...]
    for t in range(n_new):
        h = a[t * db:(t + 1) * db] * h + u[t * db:(t + 1) * db]
        oc_ref[t * db:(t + 1) * db, :] = h * _gelu(slab(zc_ref, t, w))
    h_ref[...] = h
    for j in range(n_hist):
        cn_ref[:, j * w:(j + 1) * w] = ext[n_new + j]

    pext = [hist(ph_ref, j) for j in range(POOL_HIST)] + [slab(zd_ref, t) for t in range(n_new)]
    for t in range(n_new):
        sums, run, k = [], None, 0
        for win in POOL_WINDOWS:
            while k < win:
                term = pext[POOL_HIST + t - k]
                run = term if run is None else run + term
                k += 1
            sums.append(run)
        pos = jnp.full((db, 1), pos0 + t, jnp.int32)
        od_ref[t * db:(t + 1) * db, :] = _pool_select(sums, pext[POOL_HIST + t], pos, wp_ref[...], sc_ref[...])
    for j in range(POOL_HIST):
        pn_ref[:, j * w:(j + 1) * w] = pext[n_new + j]


def _cd_sample(zc, zd, h0, conv_hist, pool_hist, cw, cb, wg, bg, lam, wp, scale, n_new, pos0):
    t, _ = zc.shape
    db = t // n_new
    w = BRANCH_W
    shp = lambda c: jax.ShapeDtypeStruct((db, c), F32)
    return pl.pallas_call(
        functools.partial(_cd_sample_kernel, n_new=n_new, db=db, pos0=pos0),
        out_shape=[jax.ShapeDtypeStruct((t, w), F32), jax.ShapeDtypeStruct((t, w), F32),
                   shp(w), shp((CONV_C - 1) * w), shp(POOL_HIST * w)],
        scratch_shapes=[pltpu.VMEM((t, w), F32)],
        compiler_params=pltpu.CompilerParams(vmem_limit_bytes=V7X_VMEM_LIMIT_BYTES),
        name="rglru_pool_sample",
    )(zc, zd, h0, conv_hist, pool_hist, cw, cb, wg, bg, lam, wp, scale)


def _merge_kernel(x_ref, oa_ref, ob_ref, oc_ref, od_ref, g_ref, wg_ref, wb_ref, wo_ref, o_ref):
    x = x_ref[...]
    d = x.shape[1]
    h = _rms(x, g_ref[...]).astype(BF16)
    mix = jnp.zeros(x.shape, F32)
    for n, br in enumerate((oa_ref, ob_ref, oc_ref, od_ref)):
        gate = _sigmoid(_dot(h, wg_ref[:, n * d:(n + 1) * d]))
        mix = mix + gate * _dot(br[...].astype(BF16), wb_ref[n])
    o_ref[...] = x + _dot(mix.astype(BF16), wo_ref[...])


def _merge(x2, oa, ob, oc, od, g, wg, wb, wo):
    t, d = x2.shape
    tm = min(TM_PROJ, t)
    w = BRANCH_W
    row = lambda i: (i, 0)
    fix = lambda i: (0, 0)
    return pl.pallas_call(
        _merge_kernel,
        grid=(t // tm,),
        in_specs=[pl.BlockSpec((tm, d), row)] + [pl.BlockSpec((tm, w), row)] * 4
                 + [pl.BlockSpec((1, d), fix), pl.BlockSpec((d, N_BRANCH * d), fix),
                    pl.BlockSpec((N_BRANCH, w, d), lambda i: (0, 0, 0)), pl.BlockSpec((d, d), fix)],
        out_specs=pl.BlockSpec((tm, d), row),
        out_shape=jax.ShapeDtypeStruct((t, d), F32),
        compiler_params=_params("parallel"),
        name="merge",
    )(x2, oa, ob, oc, od, g, wg, wb, wo)


def _ple(x, pe, gp, wpg, wp):
    gate = _sigmoid(_dot(_rms(x, gp).astype(BF16), wpg))
    return x + gate * _dot(pe.astype(BF16), wp)


def _ffn_prompt_kernel(x_ref, xp_ref, g_ref, wa_ref, wb_ref, cwa_ref, cwb_ref, cba_ref, cbb_ref, wd_ref,
                       pe_ref, gp_ref, wpg_ref, wp_ref, o_ref, sa_ref, sb_ref,
                       xn_scr, xnp_scr, acc_scr, ea_scr, eb_scr, *, tiles_per_seq):
    i, f = pl.program_id(0), pl.program_id(1)
    tm = x_ref.shape[0]
    halo = FFN_HALO

    @pl.when(f == 0)
    def _():
        xn_scr[...] = _rms(x_ref[...], g_ref[...]).astype(BF16)
        xnp_scr[...] = _rms(xp_ref[...], g_ref[...]).astype(BF16)
        acc_scr[...] = jnp.zeros_like(acc_scr)

    seq_start = (i % tiles_per_seq) == 0

    def conv_half(w_ref, cw_ref, cb_ref, e_scr, s_ref):
        u = _dot(xn_scr[...], w_ref[...])
        e_scr[0:halo, :] = jnp.where(seq_start, 0.0, _dot(xnp_scr[...], w_ref[...]))
        e_scr[halo:, :] = u
        y = cb_ref[...] + cw_ref[CONV_F - 1:CONV_F, :] * u
        for j in range(CONV_F - 1):
            y = y + cw_ref[j:j + 1, :] * e_scr[pl.ds(halo - (CONV_F - 1) + j, tm), :]
        s_ref[0] = e_scr[pl.ds(halo + tm - (CONV_F - 1), CONV_F - 1), :]
        return y

    ya = conv_half(wa_ref, cwa_ref, cba_ref, ea_scr, sa_ref)
    yb = conv_half(wb_ref, cwb_ref, cbb_ref, eb_scr, sb_ref)
    acc_scr[...] += _dot((_gelu(ya) * yb).astype(BF16), wd_ref[...])

    @pl.when(f == pl.num_programs(1) - 1)
    def _():
        o_ref[...] = _ple(x_ref[...] + acc_scr[...], pe_ref[...], gp_ref[...], wpg_ref[...], wp_ref[...])


def _ffn_prompt(layer, x2, seq_len, g, w_up, cw, cb, w_down, pe3, gp, wpg, wp):
    t, d = x2.shape
    dff = w_down.shape[0]
    tm = min(TM_FFN, seq_len)
    tf = TF_FFN
    halo = FFN_HALO
    assert seq_len % tm == 0 and dff % tf == 0 and tm % halo == 0
    nf = dff // tf
    tps = seq_len // tm
    dp = pe3.shape[2]
    fix = lambda i, f: (0, 0)
    st_shape = jax.ShapeDtypeStruct((t // tm, CONV_F - 1, dff), F32)
    st_spec = pl.BlockSpec((1, CONV_F - 1, tf), lambda i, f: (i, 0, f))
    return pl.pallas_call(
        functools.partial(_ffn_prompt_kernel, tiles_per_seq=tps),
        grid=(t // tm, nf),
        in_specs=[pl.BlockSpec((tm, d), lambda i, f: (i, 0)),
                  pl.BlockSpec((halo, d), lambda i, f: (jnp.maximum(i * (tm // halo) - 1, 0), 0)),
                  pl.BlockSpec((1, d), fix),
                  pl.BlockSpec((d, tf), lambda i, f: (0, f)),
                  pl.BlockSpec((d, tf), lambda i, f: (0, f + nf)),
                  pl.BlockSpec((CONV_F, tf), lambda i, f: (0, f)),
                  pl.BlockSpec((CONV_F, tf), lambda i, f: (0, f + nf)),
                  pl.BlockSpec((1, tf), lambda i, f: (0, f)),
                  pl.BlockSpec((1, tf), lambda i, f: (0, f + nf)),
                  pl.BlockSpec((tf, d), lambda i, f: (f, 0)),
                  pl.BlockSpec((None, tm, dp), lambda i, f: (layer, i, 0)),
                  pl.BlockSpec((1, d), fix), pl.BlockSpec((d, d), fix), pl.BlockSpec((dp, d), fix)],
        out_specs=[pl.BlockSpec((tm, d), lambda i, f: (i, 0)), st_spec, st_spec],
        out_shape=[jax.ShapeDtypeStruct((t, d), F32), st_shape, st_shape],
        scratch_shapes=[pltpu.VMEM((tm, d), BF16), pltpu.VMEM((halo, d), BF16), pltpu.VMEM((tm, d), F32),
                        pltpu.VMEM((tm + halo, tf), F32), pltpu.VMEM((tm + halo, tf), F32)],
        compiler_params=_params("arbitrary", "arbitrary"),
        name="ffn_prompt",
    )(x2, x2, g, w_up, w_up, cw, cw, cb, cb, w_down, pe3, gp, wpg, wp)


def _ffn_sample_kernel(x_ref, g_ref, wa_ref, wb_ref, cwa_ref, cwb_ref, cba_ref, cbb_ref, wd_ref,
                       ha0_ref, hb0_ref, ha1_ref, hb1_ref, pe_ref, gp_ref, wpg_ref, wp_ref,
                       o_ref, sa0_ref, sb0_ref, sa1_ref, sb1_ref, xn_scr, acc_scr, g_scr, *, n_new, db):
    f = pl.program_id(0)

    @pl.when(f == 0)
    def _():
        xn_scr[...] = _rms(x_ref[...], g_ref[...]).astype(BF16)
        acc_scr[...] = jnp.zeros_like(acc_scr)

    def conv_half(w_ref, cw_ref, cb_ref, h0_ref, h1_ref, s0_ref, s1_ref):
        u = _dot(xn_scr[...], w_ref[...])
        ext = [h0_ref[...], h1_ref[...]] + [u[t * db:(t + 1) * db] for t in range(n_new)]
        s0_ref[...] = ext[n_new]
        s1_ref[...] = ext[n_new + 1]
        ys = []
        for t in range(n_new):
            y = cb_ref[...] + cw_ref[0:1, :] * ext[t]
            for j in range(1, CONV_F):
                y = y + cw_ref[j:j + 1, :] * ext[t + j]
            ys.append(y)
        return ys

    ya = conv_half(wa_ref, cwa_ref, cba_ref, ha0_ref, ha1_ref, sa0_ref, sa1_ref)
    yb = conv_half(wb_ref, cwb_ref, cbb_ref, hb0_ref, hb1_ref, sb0_ref, sb1_ref)
    for t in range(n_new):
        g_scr[t * db:(t + 1) * db, :] = (_gelu(ya[t]) * yb[t]).astype(BF16)
    acc_scr[...] += _dot(g_scr[...], wd_ref[...])

    @pl.when(f == pl.num_programs(0) - 1)
    def _():
        o_ref[...] = _ple(x_ref[...] + acc_scr[...], pe_ref[...], gp_ref[...], wpg_ref[...], wp_ref[...])


def _ffn_sample(layer, x2, g, w_up, cw, cb, w_down, hist2, pe3, gp, wpg, wp, n_new):
    t, d = x2.shape
    db = t // n_new
    dff = w_down.shape[0]
    tf = TF_FFN
    nf = dff // tf
    dp = pe3.shape[2]
    fix = lambda f: (0, 0)
    hist = lambda blk: pl.BlockSpec((None, db, tf), lambda f, blk=blk: (layer, 0, f + blk * nf))
    st_shape = jax.ShapeDtypeStruct((db, dff), F32)
    st_spec = pl.BlockSpec((db, tf), lambda f: (0, f))
    return pl.pallas_call(
        functools.partial(_ffn_sample_kernel, n_new=n_new, db=db),
        grid=(nf,),
        in_specs=[pl.BlockSpec((t, d), fix), pl.BlockSpec((1, d), fix),
                  pl.BlockSpec((d, tf), lambda f: (0, f)), pl.BlockSpec((d, tf), lambda f: (0, f + nf)),
                  pl.BlockSpec((CONV_F, tf), lambda f: (0, f)), pl.BlockSpec((CONV_F, tf), lambda f: (0, f + nf)),
                  pl.BlockSpec((1, tf), lambda f: (0, f)), pl.BlockSpec((1, tf), lambda f: (0, f + nf)),
                  pl.BlockSpec((tf, d), lambda f: (f, 0)),
                  hist(0), hist(1), hist(2), hist(3),
                  pl.BlockSpec((None, t, dp), lambda f: (layer, 0, 0)),
                  pl.BlockSpec((1, d), fix), pl.BlockSpec((d, d), fix), pl.BlockSpec((dp, d), fix)],
        out_specs=[pl.BlockSpec((t, d), fix), st_spec, st_spec, st_spec, st_spec],
        out_shape=[jax.ShapeDtypeStruct((t, d), F32)] + [st_shape] * 4,
        scratch_shapes=[pltpu.VMEM((t, d), BF16), pltpu.VMEM((t, d), F32), pltpu.VMEM((t, tf), BF16)],
        compiler_params=_params("arbitrary"),
        name="ffn_sample",
    )(x2, g, w_up, w_up, cw, cw, cb, cb, w_down, hist2, hist2, hist2, hist2, pe3, gp, wpg, wp)


def _block_diag(w4):
    h, n, _ = w4.shape
    eye = jnp.eye(h, dtype=w4.dtype)
    return (eye[:, None, :, None] * w4[:, :, None, :]).reshape(h * n, h * n)


def kernel(x_prompt, x_sample, cache_k, cache_v, state_hgrn, state_rglru_h, state_rglru_conv, state_pool,
           state_ffn_conv, page_table, p_prompt, p_sample, norm_mix, w_in, q_norm, k_norm, sb_bias, lb_logits,
           hgrn_norm, conv_c_w, conv_c_b, w_rg_a, b_rg_a, w_rg_x, b_rg_x, lam, w_pool, pool_scale, w_branch,
           w_out, norm_ffn, w_up, conv_f_w, conv_f_b, w_down, norm_ple, w_ple_gate, w_ple):
    depth = w_in.shape[0]
    b, l, d = x_prompt.shape
    db, n_new, _ = x_sample.shape
    w = BRANCH_W
    n_mix = 10 * w
    dff = w_down.shape[1]
    dp = p_prompt.shape[-1]
    n_pool, page = cache_k.shape[1], cache_k.shape[2]
    pos0 = page_table.shape[1] * page

    lb_all, loglb_all, log1m_all = _lower_bounds(lb_logits.astype(F32))
    row = lambda a, i: a[i].reshape(1, -1)
    tile_heads = lambda a, i: jnp.tile(a[i], N_HEAD).reshape(1, w)

    xp = x_prompt.reshape(b * l, d)
    xs = x_sample.transpose(1, 0, 2).reshape(n_new * db, d)
    pe_p = p_prompt.reshape(depth, b * l, dp)
    pe_s = p_sample.transpose(0, 2, 1, 3).reshape(depth, n_new * db, dp)
    cache_k4 = cache_k.reshape(depth, n_pool, page, w)
    cache_v4 = cache_v.reshape(depth, n_pool, page, w)
    hgrn_state = state_hgrn.reshape(depth, db, -1)
    conv_state = state_rglru_conv.reshape(depth, db, -1)
    pool_state = state_pool.reshape(depth, db, -1)
    ffn_state = state_ffn_conv.reshape(depth, db, -1)

    outs = {k: [] for k in ("kp", "vp", "ks", "vs", "sp", "ss", "hp", "hs", "cp", "cs", "pp", "ps", "fp", "fs")}
    for i in range(depth):
        w_mix = w_in[i, :, :n_mix].astype(BF16)
        w_gate = w_in[i, :, n_mix:].astype(BF16)
        wg_c = jnp.concatenate([_block_diag(w_rg_a[i]), _block_diag(w_rg_x[i])], axis=1).astype(BF16)
        bg_c = jnp.concatenate([b_rg_a[i], b_rg_x[i]]).reshape(1, 2 * w)
        wp_d = _block_diag(w_pool[i]).astype(BF16)
        g_mix, g_ffn, g_ple = row(norm_mix, i), row(norm_ffn, i), row(norm_ple, i)
        qn, kn, gn = tile_heads(q_norm, i), tile_heads(k_norm, i), tile_heads(hgrn_norm, i)
        lb, loglb, log1m = row(lb_all, i), row(loglb_all, i), row(log1m_all, i)
        cw_c, cb_c, lam_i, sc_d = conv_c_w[i], row(conv_c_b, i), row(lam, i), row(pool_scale, i)
        w_br, w_o = w_branch[i].astype(BF16), w_out[i].astype(BF16)
        w_u, w_d = w_up[i].astype(BF16), w_down[i].astype(BF16)
        cw_f, cb_f = conv_f_w[i], row(conv_f_b, i)
        w_pg, w_pe = w_ple_gate[i].astype(BF16), w_ple[i].astype(BF16)
        bias = sb_bias[i].astype(F32)

        q, k, v, kb, vb, zb, zc, zd = _inproj(xp, g_mix, w_mix, qn, kn, BF16)
        r3 = lambda a: a.reshape(b, l, -1)
        o_a = _attn_prompt(bias, r3(q), r3(kb), r3(vb))
        o_b, s_p = _hgrn_prompt(r3(zb), lb, loglb, log1m, gn)
        o_c, h_p = _rglru_prompt(r3(zc), cw_c, cb_c, wg_c, bg_c, lam_i)
        o_d = _pool_prompt(r3(zd), wp_d, sc_d)
        f2 = lambda a: a.reshape(b * l, w)
        x1 = _merge(xp, f2(o_a), f2(o_b), f2(o_c), f2(o_d), g_mix, w_gate, w_br, w_o)
        xp, fa, fb = _ffn_prompt(i, x1, l, g_ffn, w_u, cw_f, cb_f, w_d, pe_p, g_ple, w_pg, w_pe)
        outs["kp"].append(k.reshape(b, l, N_HEAD, HEAD_W))
        outs["vp"].append(v.reshape(b, l, N_HEAD, HEAD_W))
        s_heads = s_p.reshape(b, N_HEAD, HEAD_W, N_HEAD, HEAD_W)
        outs["sp"].append(jnp.stack([s_heads[:, h, :, h, :] for h in range(N_HEAD)], axis=1))
        outs["hp"].append(h_p.reshape(b, w))
        outs["cp"].append(r3(zc)[:, l - (CONV_C - 1):, :w])
        outs["pp"].append(r3(zd)[:, l - POOL_HIST:, :])
        tps = fa.shape[0] // b
        outs["fp"].append(jnp.concatenate([fa[tps - 1::tps], fb[tps - 1::tps]], axis=-1))

        q, k, v, _, _, zb, zc, zd = _inproj(xs, g_mix, w_mix, qn, kn, F32)
        r4 = lambda a: a.reshape(n_new, db, 1, w)
        o_a = _attn_sample(i, page_table, bias, r4(q), r4(k), r4(v), cache_k4, cache_v4).reshape(n_new * db, w)
        o_b, s_s = _hgrn_sample(i, zb, hgrn_state, lb, gn, n_new)
        o_c, o_d, h_s, c_s, p_s = _cd_sample(zc, zd, state_rglru_h[i], conv_state[i], pool_state[i],
                                             cw_c, cb_c, wg_c, bg_c, lam_i, wp_d, sc_d, n_new, pos0)
        x1 = _merge(xs, o_a, o_b, o_c, o_d, g_mix, w_gate, w_br, w_o)
        xs, fa0, fb0, fa1, fb1 = _ffn_sample(i, x1, g_ffn, w_u, cw_f, cb_f, w_d, ffn_state, pe_s,
                                             g_ple, w_pg, w_pe, n_new)
        outs["ks"].append(k)
        outs["vs"].append(v)
        outs["ss"].append(s_s.reshape(db, N_HEAD, HEAD_W, HEAD_W))
        outs["hs"].append(h_s)
        outs["cs"].append(c_s.reshape(db, CONV_C - 1, w))
        outs["ps"].append(p_s.reshape(db, POOL_HIST, w))
        outs["fs"].append(jnp.stack([jnp.concatenate([fa0, fb0], axis=-1),
                                     jnp.concatenate([fa1, fb1], axis=-1)], axis=1))

    stk = lambda key: jnp.stack(outs[key], axis=0)
    to_batch_major = lambda a: a.reshape(depth, n_new, db, N_HEAD, HEAD_W).transpose(0, 2, 1, 3, 4)
    y_sample = xs.reshape(n_new, db, d).transpose(1, 0, 2)
    return (xp.reshape(b, l, d), y_sample,
            stk("kp"), stk("vp"), to_batch_major(stk("ks")), to_batch_major(stk("vs")),
            stk("sp"), stk("ss"), stk("hp"), stk("hs"), stk("cp"), stk("cs"),
            stk("pp"), stk("ps"), stk("fp"), stk("fs"))
```

```python
import functools
import math

import jax
import jax.numpy as jnp
from jax import lax
from jax.experimental import pallas as pl
from jax.experimental.pallas import tpu as pltpu

F32 = jnp.float32
BF16 = jnp.bfloat16
EPS = 1e-6

N_HEAD = 4
HEAD_W = 64
BRANCH_W = N_HEAD * HEAD_W
N_BRANCH = 4
RG_C = 8.0
POOL_WINDOWS = (2, 4, 8, 16)
POOL_HIST = max(POOL_WINDOWS) - 1
CONV_C = 4
CONV_F = 3
HGRN_CHUNK = 64
HGRN_SUB = 16
NEG_BIG = -1e30
LOG2E = math.log2(math.e)

V7X_VMEM_LIMIT_BYTES = 56 * 1024 * 1024

TM_PROJ = 512
TQ_ATTN = 256
TL_SCAN = 512
TL_HGRN = 256
TM_FFN = 1024
TF_FFN = 256
FFN_HALO = 16


def _params(*sem):
    return pltpu.CompilerParams(dimension_semantics=sem, vmem_limit_bytes=V7X_VMEM_LIMIT_BYTES)


def _dot(a, b):
    return jnp.dot(a, b, preferred_element_type=F32)


def _dot_nt(a, b):
    return lax.dot_general(a, b, (((1,), (1,)), ((), ())), preferred_element_type=F32)


def _dot_tn(a, b):
    return lax.dot_general(a, b, (((0,), (0,)), ((), ())), preferred_element_type=F32)


def _split2(x):
    hi = x.astype(BF16)
    lo = (x - hi.astype(F32)).astype(BF16)
    return hi, lo


def _split3(x):
    hi = x.astype(BF16)
    r = x - hi.astype(F32)
    mid = r.astype(BF16)
    lo = (r - mid.astype(F32)).astype(BF16)
    return hi, mid, lo


def _dot_x2(x, m):
    hi, lo = _split2(x)
    return _dot(hi, m) + _dot(lo, m)


def _sigmoid(x):
    return 1.0 / (1.0 + jnp.exp(-x))


def _softplus_tail(x):
    return jnp.log1p(jnp.exp(-jnp.abs(x)))


def _log_sigmoid(x):
    return jnp.minimum(x, 0.0) - _softplus_tail(x)


def _softplus(x):
    return jnp.maximum(x, 0.0) + _softplus_tail(x)


def _silu(x):
    return x * _sigmoid(x)


def _gelu(x):
    c = math.sqrt(2.0 / math.pi)
    return 0.5 * x * (1.0 + jnp.tanh(c * (x + 0.044715 * (x * x * x))))


def _rms(x, g):
    ms = jnp.mean(x * x, axis=-1, keepdims=True)
    return x * lax.rsqrt(ms + EPS) * g


def _head_id(shape, dim):
    return lax.shift_right_logical(lax.broadcasted_iota(jnp.int32, shape, dim), 6)


def _head_block_ones():
    n = BRANCH_W
    return jnp.where(_head_id((n, n), 0) == _head_id((n, n), 1), 1.0, 0.0).astype(BF16)


def _head_rms(a, g, bd):
    ms = _dot_x2(a * a, bd) * (1.0 / HEAD_W)
    return a * lax.rsqrt(ms + EPS) * g


def _shift_rows(x, d, fill):
    rolled = pltpu.roll(x, d, 0)
    row = lax.broadcasted_iota(jnp.int32, x.shape, 0)
    return jnp.where(row < d, fill, rolled)


def _lb_kernel(lg_ref, lb_ref, loglb_ref, log1m_ref):
    x = lg_ref[...]
    depth = x.shape[0]
    rows = [x[i:i + 1] for i in range(depth)]
    m = functools.reduce(jnp.maximum, rows)
    e = [jnp.exp(r - m) for r in rows]
    tot = functools.reduce(lambda a, b: a + b, e)
    zero = jnp.zeros_like(m)
    lb_ref[0:1, :] = zero
    loglb_ref[0:1, :] = jnp.full_like(m, -jnp.inf)
    log1m_ref[0:1, :] = zero
    acc = zero
    for i in range(1, depth):
        acc = acc + e[i] / tot
        lb_ref[i:i + 1, :] = acc
        loglb_ref[i:i + 1, :] = jnp.log(acc)
        log1m_ref[i:i + 1, :] = jnp.log1p(-acc)


def _lower_bounds(lb_logits):
    shp = jax.ShapeDtypeStruct(lb_logits.shape, F32)
    return pl.pallas_call(_lb_kernel, out_shape=(shp, shp, shp), name="hgrn_lower_bounds")(lb_logits)


def _inproj_common(x, g_ref, w_ref, qn_ref, kn_ref):
    h = _rms(x, g_ref[...]).astype(BF16)
    z = _dot(h, w_ref[...])
    bd = _head_block_ones()
    w = BRANCH_W
    q = _head_rms(z[:, 0:w], qn_ref[...], bd) * (HEAD_W ** -0.5 * LOG2E)
    k = _head_rms(z[:, w:2 * w], kn_ref[...], bd)
    return q, k, z[:, 2 * w:3 * w], z[:, 3 * w:7 * w], z[:, 7 * w:9 * w], z[:, 9 * w:10 * w]


def _inproj_prompt_kernel(x_ref, g_ref, w_ref, qn_ref, kn_ref,
                          q_o, kt_o, vt_o, ktb_o, vtb_o, zb_o, zc_o, zd_o):
    q, k, v, zb, zc, zd = _inproj_common(x_ref[0], g_ref, w_ref, qn_ref, kn_ref)
    q_o[0] = q.astype(BF16)
    kt, vt = k.T, v.T
    kt_o[0] = kt
    vt_o[0] = vt
    tk = ktb_o.shape[2]
    for c in range(ktb_o.shape[0]):
        ktb_o[c] = kt[:, c * tk:(c + 1) * tk].astype(BF16)
        vtb_o[c] = vt[:, c * tk:(c + 1) * tk].astype(BF16)
    zb_o[0] = zb
    zc_o[0] = zc
    zd_o[0] = zd


def _inproj_prompt(x3, g, w, qn, kn):
    b, l, d = x3.shape
    tm = min(TM_PROJ, l)
    tk = min(TQ_ATTN, l)
    n = w.shape[1]
    bw = BRANCH_W
    per = tm // tk
    row = lambda i, j: (i, j, 0)
    fix = lambda i, j: (0, 0)
    col = lambda i, j: (i, 0, j)
    tiles = lambda i, j: (i * (l // tm) + j, 0, 0)
    tok = lambda c, dt: (jax.ShapeDtypeStruct((b, l, c), dt), pl.BlockSpec((1, tm, c), row))
    tr = (jax.ShapeDtypeStruct((b, bw, l), F32), pl.BlockSpec((1, bw, tm), col))
    trb = (jax.ShapeDtypeStruct((b * l // tk, bw, tk), BF16), pl.BlockSpec((per, bw, tk), tiles))
    outs = [tok(bw, BF16), tr, tr, trb, trb, tok(4 * bw, F32), tok(2 * bw, F32), tok(bw, F32)]
    return pl.pallas_call(
        _inproj_prompt_kernel,
        grid=(b, l // tm),
        in_specs=[pl.BlockSpec((1, tm, d), row), pl.BlockSpec((1, d), fix), pl.BlockSpec((d, n), fix),
                  pl.BlockSpec((1, bw), fix), pl.BlockSpec((1, bw), fix)],
        out_specs=[s for _, s in outs],
        out_shape=[s for s, _ in outs],
        compiler_params=_params("parallel", "parallel"),
        name="inproj_prompt",
    )(x3, g, w, qn, kn)


def _inproj_sample_kernel(x_ref, g_ref, w_ref, qn_ref, kn_ref, q_o, k_o, v_o, zb_o, zc_o, zd_o):
    for ref, val in zip((q_o, k_o, v_o, zb_o, zc_o, zd_o),
                        _inproj_common(x_ref[...], g_ref, w_ref, qn_ref, kn_ref)):
        ref[...] = val


def _inproj_sample(x2, g, w, qn, kn):
    t, d = x2.shape
    bw = BRANCH_W
    widths = (bw, bw, bw, 4 * bw, 2 * bw, bw)
    return pl.pallas_call(
        _inproj_sample_kernel,
        out_shape=[jax.ShapeDtypeStruct((t, c), F32) for c in widths],
        compiler_params=pltpu.CompilerParams(vmem_limit_bytes=V7X_VMEM_LIMIT_BYTES),
        name="inproj_sample",
    )(x2, g, w, qn, kn)


def _upper_ones(n):
    r = lax.broadcasted_iota(jnp.int32, (n, n), 0)
    c = lax.broadcasted_iota(jnp.int32, (n, n), 1)
    return jnp.where(r > c, 1.0, 0.0).astype(BF16)


def _sb_weights(z, carry, upper, mask):
    log_keep = -jnp.maximum(z, 0.0) - jnp.log2(1.0 + jnp.exp2(-jnp.abs(z)))
    if mask is not None:
        log_keep = jnp.where(mask, log_keep, 0.0)
    local = _dot(log_keep.astype(BF16), upper)
    w = jnp.exp2(z + log_keep + (local + carry))
    if mask is not None:
        w = jnp.where(mask, w, 0.0)
    return w.astype(BF16), carry + local[:, 0:1] + log_keep[:, 0:1]


def _attn_prompt_kernel(bias_ref, q_ref, kt_ref, vt_ref, o_ref):
    qi = pl.program_id(1)
    tq = q_ref.shape[1]
    q = q_ref[0]
    heads = range(N_HEAD)
    lanes = lambda h: slice(h * HEAD_W, (h + 1) * HEAD_W)
    qs = [q[:, lanes(h)] for h in heads]
    bias = [bias_ref[h] * LOG2E for h in heads]
    upper = _upper_ones(tq)
    causal = (lax.broadcasted_iota(jnp.int32, (tq, tq), 1)
              < lax.broadcasted_iota(jnp.int32, (tq, tq), 0))

    def tile(kidx, carries, mask):
        pvs, new = [], []
        for h in heads:
            z = _dot(qs[h], kt_ref[kidx, lanes(h), :]) + bias[h]
            w, c = _sb_weights(z, carries[h], upper, mask)
            pvs.append(_dot_nt(w, vt_ref[kidx, lanes(h), :]))
            new.append(c)
        return pvs, new

    pvs, carries = tile(qi, [jnp.zeros((tq, 1), F32)] * N_HEAD, causal)

    def body(j, state):
        accs, carries = state
        pvs, carries = tile(qi - j, carries, None)
        return [a + p for a, p in zip(accs, pvs)], carries

    accs, _ = lax.fori_loop(1, qi + 1, body, (pvs, carries))
    for h in heads:
        o_ref[0, :, lanes(h)] = accs[h]


def _attn_prompt(bias, q3, ktb, vtb):
    b, l, w = q3.shape
    tq = ktb.shape[2]
    nk = l // tq
    return pl.pallas_call(
        _attn_prompt_kernel,
        grid=(b, nk),
        in_specs=[pl.BlockSpec(memory_space=pltpu.SMEM),
                  pl.BlockSpec((1, tq, w), lambda i, j: (i, j, 0)),
                  pl.BlockSpec((nk, w, tq), lambda i, j: (i, 0, 0)),
                  pl.BlockSpec((nk, w, tq), lambda i, j: (i, 0, 0))],
        out_specs=pl.BlockSpec((1, tq, w), lambda i, j: (i, j, 0)),
        out_shape=jax.ShapeDtypeStruct((b, l, w), F32),
        compiler_params=_params("parallel", "arbitrary"),
        name="attn_prompt",
    )(bias, q3, ktb, vtb)


def _attn_sample_kernel(pt_ref, bias_ref, q_ref, kn_ref, vn_ref, *refs, n_pages, n_new):
    del pt_ref
    k_refs, v_refs, o_ref = refs[:n_pages], refs[n_pages:2 * n_pages], refs[2 * n_pages]
    page = k_refs[0].shape[-1]
    flat = lambda ref: ref[...].reshape(BRANCH_W, page).astype(BF16)
    rows = N_HEAD * 8
    row = lax.broadcasted_iota(jnp.int32, (rows, 1), 0)
    row_t = jnp.bitwise_and(row, 7)
    row_h = lax.shift_right_logical(row, 3)
    lane_head = _head_id((1, BRANCH_W), 1)
    qs = jnp.zeros((rows, BRANCH_W), F32)
    for t in range(n_new):
        qs = qs + jnp.where(row_t == t, q_ref[t, 0], 0.0)
    qs = jnp.where(row_h == lane_head, qs, 0.0)
    bias = jnp.zeros((rows, 1), F32)
    for h in range(N_HEAD):
        bias = jnp.where(row_h == h, bias_ref[h] * LOG2E, bias)

    carry = jnp.zeros((rows, 1), F32)
    out = jnp.zeros((rows, BRANCH_W), F32)
    for j in reversed(range(n_new)):
        z = jnp.sum(qs * kn_ref[j, 0], axis=-1, keepdims=True) + bias
        log_keep = -jnp.maximum(z, 0.0) - jnp.log2(1.0 + jnp.exp2(-jnp.abs(z)))
        seen = row_t > j
        w = jnp.where(seen, jnp.exp2(z + log_keep + carry), 0.0)
        out = out + w * vn_ref[j, 0]
        carry = carry + jnp.where(seen, log_keep, 0.0)

    qs16 = qs.astype(BF16)
    upper = _upper_ones(page)
    for p in reversed(range(n_pages)):
        w, carry = _sb_weights(_dot(qs16, flat(k_refs[p])) + bias, carry, upper, None)
        out = out + _dot_nt(w, flat(v_refs[p]))

    res = jnp.zeros((8, BRANCH_W), F32)
    for h in range(N_HEAD):
        res = res + jnp.where(lane_head == h, out[h * 8:(h + 1) * 8], 0.0)
    for t in range(n_new):
        o_ref[t, 0] = res[t:t + 1]


def _attn_sample(layer, page_table, bias, q4, kn4, vn4, cache_kt, cache_vt):
    n_new, db, _, w = q4.shape
    n_pages = page_table.shape[1]
    page = cache_kt.shape[4]
    new_spec = pl.BlockSpec((n_new, 1, 1, w), lambda b, pt, bs: (0, b, 0, 0))

    def page_spec(p):
        return pl.BlockSpec((None, None, N_HEAD, HEAD_W, page),
                            lambda b, pt, bs, p=p: (layer, pt[b, p], 0, 0, 0))

    grid_spec = pltpu.PrefetchScalarGridSpec(
        num_scalar_prefetch=2,
        grid=(db,),
        in_specs=[new_spec, new_spec, new_spec] + [page_spec(p) for p in range(n_pages)] * 2,
        out_specs=new_spec,
    )
    return pl.pallas_call(
        functools.partial(_attn_sample_kernel, n_pages=n_pages, n_new=n_new),
        grid_spec=grid_spec,
        out_shape=jax.ShapeDtypeStruct(q4.shape, F32),
        compiler_params=_params("arbitrary"),
        name="attn_sample",
    )(page_table, bias, q4, kn4, vn4, *([cache_kt] * n_pages), *([cache_vt] * n_pages))


def _hgrn_gates(qb, fr, lb, loglb, log1m):
    c = log1m + _log_sigmoid(fr)
    log_f = jnp.maximum(loglb, c) + jnp.log1p(jnp.exp(-jnp.abs(loglb - c)))
    key = (1.0 - lb) * _sigmoid(-fr)
    return _silu(qb), key, log_f


def _hgrn_prompt_kernel(z_ref, lb_ref, loglb_ref, log1m_ref, gn_ref, o_ref, s_ref,
                        st_scr, q_scr, k_scr, v_scr, b_scr, p_scr, vx_scr, od_scr):
    ti = pl.program_id(1)
    w = BRANCH_W
    ck, sub = HGRN_CHUNK, HGRN_SUB
    n_sub = ck // sub
    tl = z_ref.shape[1]

    @pl.when(ti == 0)
    def _():
        st_scr[...] = jnp.zeros_like(st_scr)

    bd = _head_block_ones()
    bd_mask = _head_id((w, w), 0) == _head_id((w, w), 1)
    r = lax.broadcasted_iota(jnp.int32, (ck, ck), 0)
    c = lax.broadcasted_iota(jnp.int32, (ck, ck), 1)
    lower_incl = jnp.where(c <= r, 1.0, 0.0).astype(BF16)
    row_ck = lax.broadcasted_iota(jnp.int32, (ck, 1), 0)
    row_sub = lax.broadcasted_iota(jnp.int32, (sub, 1), 0)
    stack_mask = (lax.shift_right_logical(lax.broadcasted_iota(jnp.int32, (N_HEAD * sub, w), 0), 4)
                  == _head_id((N_HEAD * sub, w), 1))
    lb, loglb, log1m, gn = lb_ref[...], loglb_ref[...], log1m_ref[...], gn_ref[...]

    def chunk(ci, _):
        r0 = pl.multiple_of(ci * ck, ck)
        zz = z_ref[0, pl.ds(r0, ck), :]
        qh, key, log_f = _hgrn_gates(zz[:, 0:w], zz[:, w:2 * w], lb, loglb, log1m)
        val = zz[:, 2 * w:3 * w]
        f_hi, f_mid, f_lo = _split3(log_f)
        b = _dot(lower_incl, f_hi) + _dot(lower_incl, f_mid) + _dot(lower_incl, f_lo)
        q_scr[...] = qh
        k_scr[...] = key
        v_scr[...] = val
        b_scr[...] = b

        st = st_scr[...]
        out = _dot_nt((qh * jnp.exp(b)).astype(BF16), st.astype(BF16))
        for j in range(n_sub - 1):
            e_j = b[(j + 1) * sub - 1:(j + 1) * sub]
            qj = qh * jnp.exp(jnp.where(row_ck >= (j + 1) * sub, b - e_j, NEG_BIG))
            kj = key[j * sub:(j + 1) * sub] * jnp.exp(e_j - b[j * sub:(j + 1) * sub])
            vj = val[j * sub:(j + 1) * sub]
            k_st = jnp.where(stack_mask, jnp.concatenate([kj] * N_HEAD, axis=0), 0.0)
            v_st = jnp.where(stack_mask, jnp.concatenate([vj] * N_HEAD, axis=0), 0.0)
            att = _dot_nt(qj.astype(BF16), k_st.astype(BF16))
            out = out + _dot(att.astype(BF16), v_st.astype(BF16))

        b_last = b[ck - 1:ck]
        k_end = key * jnp.exp(b_last - b)
        st_new = st * jnp.exp(b_last) + _dot_tn(val.astype(BF16), k_end.astype(BF16))
        st_scr[...] = jnp.where(bd_mask, st_new, 0.0)

        def diag(si, _):
            s0 = pl.multiple_of(si * sub, sub)
            q_i = q_scr[pl.ds(s0, sub), :]
            b_i = b_scr[pl.ds(s0, sub), :]
            for s in range(sub):
                k_s = k_scr[pl.ds(s0 + s, 1), :]
                b_s = b_scr[pl.ds(s0 + s, 1), :]
                v_s = v_scr[pl.ds(s0 + s, 1), :]
                p_scr[s * sub:(s + 1) * sub, :] = q_i * k_s * jnp.exp(jnp.where(row_sub >= s, b_i - b_s, NEG_BIG))
                vx_scr[s * sub:(s + 1) * sub, :] = jnp.broadcast_to(v_s, (sub, w))
            att = _dot_x2(p_scr[...], bd)
            od_scr[pl.ds(s0, sub), :] = jnp.sum((att * vx_scr[...]).reshape(sub, sub, w), axis=0)
            return 0

        lax.fori_loop(0, n_sub, diag, 0)
        out = out + od_scr[...]
        o_ref[0, pl.ds(r0, ck), :] = _head_rms(out, gn, bd) * _silu(zz[:, 3 * w:4 * w])
        return 0

    lax.fori_loop(0, tl // ck, chunk, 0)

    @pl.when(ti == pl.num_programs(1) - 1)
    def _():
        s_ref[0] = st_scr[...].T


def _hgrn_prompt(zb3, lb, loglb, log1m, gn):
    b, l, _ = zb3.shape
    w = BRANCH_W
    tl = min(TL_HGRN, l)
    assert l % tl == 0 and tl % HGRN_CHUNK == 0
    fix = lambda i, j: (0, 0)
    scr = lambda rows: pltpu.VMEM((rows, w), F32)
    return pl.pallas_call(
        _hgrn_prompt_kernel,
        grid=(b, l // tl),
        in_specs=[pl.BlockSpec((1, tl, 4 * w), lambda i, j: (i, j, 0))] + [pl.BlockSpec((1, w), fix)] * 4,
        out_specs=[pl.BlockSpec((1, tl, w), lambda i, j: (i, j, 0)),
                   pl.BlockSpec((1, w, w), lambda i, j: (i, 0, 0))],
        out_shape=[jax.ShapeDtypeStruct((b, l, w), F32), jax.ShapeDtypeStruct((b, w, w), F32)],
        scratch_shapes=[scr(w), scr(HGRN_CHUNK), scr(HGRN_CHUNK), scr(HGRN_CHUNK), scr(HGRN_CHUNK),
                        scr(HGRN_SUB * HGRN_SUB), scr(HGRN_SUB * HGRN_SUB), scr(HGRN_CHUNK)],
        compiler_params=_params("parallel", "arbitrary"),
        name="hgrn_prompt",
    )(zb3, lb, loglb, log1m, gn)


def _hgrn_sample_kernel(z_ref, s0_ref, lb_ref, gn_ref, o_ref, s_ref, f_scr, k_scr, q_scr, v_scr, o_scr,
                        *, n_new, db):
    j = pl.program_id(0)
    w = BRANCH_W
    k_per_step = s0_ref.shape[0] // HEAD_W
    steps_per_head = HEAD_W // k_per_step
    lb = lb_ref[...]

    @pl.when(j == 0)
    def _():
        for t in range(n_new):
            zz = z_ref[t * db:(t + 1) * db, :]
            fr = zz[:, w:2 * w]
            f_scr[t] = (lb + (1.0 - lb) * _sigmoid(fr)).T
            k_scr[t] = ((1.0 - lb) * _sigmoid(-fr)).T
            q_scr[t] = _silu(zz[:, 0:w]).T
            v_scr[t] = zz[:, 2 * w:3 * w].T
            o_scr[t] = jnp.zeros((w, db), F32)

    head = j // steps_per_head
    k_base = head * HEAD_W + (j % steps_per_head) * k_per_step
    v0 = pl.multiple_of(head * HEAD_W, HEAD_W)
    for kk in range(k_per_step):
        rows = slice(kk * HEAD_W, (kk + 1) * HEAD_W)
        s_k = s0_ref[rows, :]
        row = k_base + kk
        for t in range(n_new):
            s_k = (f_scr[t, pl.ds(row, 1), :] * s_k
                   + k_scr[t, pl.ds(row, 1), :] * v_scr[t, pl.ds(v0, HEAD_W), :])
            o_scr[t, pl.ds(v0, HEAD_W), :] += q_scr[t, pl.ds(row, 1), :] * s_k
        s_ref[rows, :] = s_k

    @pl.when(j == pl.num_programs(0) - 1)
    def _():
        bd = _head_block_ones()
        for t in range(n_new):
            gate = _silu(z_ref[t * db:(t + 1) * db, 3 * w:4 * w])
            o_ref[t * db:(t + 1) * db, :] = _head_rms(o_scr[t].T, gn_ref[...], bd) * gate


def _hgrn_sample(layer, zb, state_t, lb, gn, n_new):
    t, _ = zb.shape
    db = t // n_new
    w = BRANCH_W
    n_state = state_t.shape[1]
    rows = 16 * HEAD_W
    fix = lambda j: (0, 0)
    scr = pltpu.VMEM((n_new, w, db), F32)
    return pl.pallas_call(
        functools.partial(_hgrn_sample_kernel, n_new=n_new, db=db),
        grid=(n_state // rows,),
        in_specs=[pl.BlockSpec((t, 4 * w), fix),
                  pl.BlockSpec((None, rows, db), lambda j: (layer, j, 0)),
                  pl.BlockSpec((1, w), fix), pl.BlockSpec((1, w), fix)],
        out_specs=[pl.BlockSpec((t, w), fix), pl.BlockSpec((rows, db), lambda j: (j, 0))],
        out_shape=[jax.ShapeDtypeStruct((t, w), F32), jax.ShapeDtypeStruct((n_state, db), F32)],
        scratch_shapes=[scr] * 5,
        compiler_params=_params("arbitrary"),
        name="hgrn_sample",
    )(zb, state_t, lb, gn)


def _rglru_gates(xconv, wg, bg, sp_lam):
    w = BRANCH_W
    g = _sigmoid(_dot(xconv.astype(BF16), wg) + bg)
    log_a = -RG_C * g[:, 0:w] * sp_lam
    a = jnp.exp(log_a)
    one_minus_a2 = -jnp.tanh(log_a) * (a * a + 1.0)
    u = jnp.sqrt(one_minus_a2) * (g[:, w:2 * w] * xconv)
    return a, u


def _rglru_prompt_kernel(z_ref, cw_ref, cb_ref, wg_ref, bg_ref, lam_ref, o_ref, h_ref, ext_scr, h_scr):
    ti = pl.program_id(1)
    w = BRANCH_W
    tl = z_ref.shape[1]
    halo = 8

    @pl.when(ti == 0)
    def _():
        ext_scr[0:halo, :] = jnp.zeros((halo, w), F32)
        h_scr[...] = jnp.zeros_like(h_scr)

    x = z_ref[0, :, 0:w]
    ext_scr[halo:, :] = x
    xconv = cb_ref[...] + cw_ref[CONV_C - 1:CONV_C, :] * x
    for j in range(CONV_C - 1):
        xconv = xconv + cw_ref[j:j + 1, :] * ext_scr[pl.ds(halo - (CONV_C - 1) + j, tl), :]
    ext_scr[0:halo, :] = x[tl - halo:tl]
    a, u = _rglru_gates(xconv, wg_ref[...], bg_ref[...], _softplus(-lam_ref[...]))
    d = 1
    while d < tl:
        u = u + a * _shift_rows(u, d, 0.0)
        a = a * _shift_rows(a, d, 1.0)
        d *= 2
    h = u + a * h_scr[0:1, :]
    h_last = h[tl - 1:tl]
    h_scr[...] = jnp.broadcast_to(h_last, h_scr.shape)
    h_ref[0] = h_last
    o_ref[0] = h * _gelu(z_ref[0, :, w:2 * w])


def _rglru_prompt(zc3, cw, cb, wg, bg, lam):
    b, l, _ = zc3.shape
    w = BRANCH_W
    tl = min(TL_SCAN, l)
    fix = lambda i, j: (0, 0)
    return pl.pallas_call(
        _rglru_prompt_kernel,
        grid=(b, l // tl),
        in_specs=[pl.BlockSpec((1, tl, 2 * w), lambda i, j: (i, j, 0)),
                  pl.BlockSpec((CONV_C, w), fix), pl.BlockSpec((1, w), fix),
                  pl.BlockSpec((w, 2 * w), fix), pl.BlockSpec((1, 2 * w), fix), pl.BlockSpec((1, w), fix)],
        out_specs=[pl.BlockSpec((1, tl, w), lambda i, j: (i, j, 0)),
                   pl.BlockSpec((1, 1, w), lambda i, j: (i, 0, 0))],
        out_shape=[jax.ShapeDtypeStruct((b, l, w), F32), jax.ShapeDtypeStruct((b, 1, w), F32)],
        scratch_shapes=[pltpu.VMEM((tl + 8, w), F32), pltpu.VMEM((8, w), F32)],
        compiler_params=_params("parallel", "arbitrary"),
        name="rglru_prompt",
    )(zc3, cw, cb, wg, bg, lam)


def _pool_select(sums, x, pos, wp, scale):
    lane_group = _head_id((1, BRANCH_W), 1)
    pooled = jnp.zeros_like(x)
    for g, win in enumerate(POOL_WINDOWS):
        cnt = jnp.minimum(pos + 1, win).astype(F32)
        pooled = jnp.where(lane_group == g, sums[g] / cnt, pooled)
    return _dot((pooled - x).astype(BF16), wp) * scale


def _pool_prompt_kernel(x_ref, wp_ref, sc_ref, o_ref, ext_scr):
    ti = pl.program_id(1)
    w = BRANCH_W
    tl = x_ref.shape[1]
    halo = 16

    @pl.when(ti == 0)
    def _():
        ext_scr[0:halo, :] = jnp.zeros((halo, w), F32)

    x = x_ref[0]
    ext_scr[halo:, :] = x
    e = ext_scr[...]
    sums = []
    d = 1
    for _ in POOL_WINDOWS:
        e = e + pltpu.roll(e, d, 0)
        sums.append(e[halo:])
        d *= 2
    ext_scr[0:halo, :] = x[tl - halo:tl]
    pos = ti * tl + lax.broadcasted_iota(jnp.int32, (tl, 1), 0)
    o_ref[0] = _pool_select(sums, x, pos, wp_ref[...], sc_ref[...])


def _pool_prompt(zd3, wp, scale):
    b, l, w = zd3.shape
    tl = min(TL_SCAN, l)
    fix = lambda i, j: (0, 0)
    return pl.pallas_call(
        _pool_prompt_kernel,
        grid=(b, l // tl),
        in_specs=[pl.BlockSpec((1, tl, w), lambda i, j: (i, j, 0)),
                  pl.BlockSpec((w, w), fix), pl.BlockSpec((1, w), fix)],
        out_specs=pl.BlockSpec((1, tl, w), lambda i, j: (i, j, 0)),
        out_shape=jax.ShapeDtypeStruct((b, l, w), F32),
        scratch_shapes=[pltpu.VMEM((tl + 16, w), F32)],
        compiler_params=_params("parallel", "arbitrary"),
        name="pool_prompt",
    )(zd3, wp, scale)


def _cd_sample_kernel(zc_ref, zd_ref, h0_ref, ch_ref, ph_ref, cw_ref, cb_ref, wg_ref, bg_ref, lam_ref,
                      wp_ref, sc_ref, oc_ref, od_ref, h_ref, cn_ref, pn_ref, xc_scr, *, n_new, db, pos0):
    w = BRANCH_W
    slab = lambda ref, t, c0=0: ref[t * db:(t + 1) * db, c0:c0 + w]
    hist = lambda ref, j: ref[j]

    n_hist = CONV_C - 1
    ext = [hist(ch_ref, j) for j in range(n_hist)] + [slab(zc_ref, t) for t in range(n_new)]
    for t in range(n_new):
        acc = cb_ref[...] + cw_ref[0:1, :] * ext[t]
        for j in range(1, CONV_C):
            acc = acc + cw_ref[j:j + 1, :] * ext[t + j]
        xc_scr[t * db:(t + 1) * db, :] = acc
    a, u = _rglru_gates(xc_scr[...], wg_ref[...], bg_ref[...], _softplus(-lam_ref[...]))
    h = h0_ref[...]
    for t in range(n_new):
        h = a[t * db:(t + 1) * db] * h + u[t * db:(t + 1) * db]
        oc_ref[t * db:(t + 1) * db, :] = h * _gelu(slab(zc_ref, t, w))
    h_ref[...] = h
    for j in range(n_hist):
        cn_ref[j] = ext[n_new + j]

    pext = [hist(ph_ref, j) for j in range(POOL_HIST)] + [slab(zd_ref, t) for t in range(n_new)]
    for t in range(n_new):
        sums, run, k = [], None, 0
        for win in POOL_WINDOWS:
            while k < win:
                term = pext[POOL_HIST + t - k]
                run = term if run is None else run + term
                k += 1
            sums.append(run)
        pos = jnp.full((db, 1), pos0 + t, jnp.int32)
        od_ref[t * db:(t + 1) * db, :] = _pool_select(sums, pext[POOL_HIST + t], pos, wp_ref[...], sc_ref[...])
    for j in range(POOL_HIST):
        pn_ref[j] = pext[n_new + j]


def _cd_sample(zc, zd, h0, conv_hist, pool_hist, cw, cb, wg, bg, lam, wp, scale, n_new, pos0):
    t, _ = zc.shape
    db = t // n_new
    w = BRANCH_W
    shp = lambda c: jax.ShapeDtypeStruct((db, c), F32)
    return pl.pallas_call(
        functools.partial(_cd_sample_kernel, n_new=n_new, db=db, pos0=pos0),
        out_shape=[jax.ShapeDtypeStruct((t, w), F32), jax.ShapeDtypeStruct((t, w), F32),
                   shp(w), jax.ShapeDtypeStruct((CONV_C - 1, db, w), F32),
                   jax.ShapeDtypeStruct((POOL_HIST, db, w), F32)],
        scratch_shapes=[pltpu.VMEM((t, w), F32)],
        compiler_params=pltpu.CompilerParams(vmem_limit_bytes=V7X_VMEM_LIMIT_BYTES),
        name="rglru_pool_sample",
    )(zc, zd, h0, conv_hist, pool_hist, cw, cb, wg, bg, lam, wp, scale)


def _merge_kernel(x_ref, oa_ref, ob_ref, oc_ref, od_ref, g_ref, wg_ref, wb_ref, wo_ref, o_ref):
    x = x_ref[...]
    d = x.shape[1]
    h = _rms(x, g_ref[...]).astype(BF16)
    mix = jnp.zeros(x.shape, F32)
    for n, br in enumerate((oa_ref, ob_ref, oc_ref, od_ref)):
        gate = _sigmoid(_dot(h, wg_ref[:, n * d:(n + 1) * d]))
        mix = mix + gate * _dot(br[...].astype(BF16), wb_ref[n])
    o_ref[...] = x + _dot(mix.astype(BF16), wo_ref[...])


def _merge(x2, oa, ob, oc, od, g, wg, wb, wo):
    t, d = x2.shape
    tm = min(TM_PROJ, t)
    w = BRANCH_W
    row = lambda i: (i, 0)
    fix = lambda i: (0, 0)
    return pl.pallas_call(
        _merge_kernel,
        grid=(t // tm,),
        in_specs=[pl.BlockSpec((tm, d), row)] + [pl.BlockSpec((tm, w), row)] * 4
                 + [pl.BlockSpec((1, d), fix), pl.BlockSpec((d, N_BRANCH * d), fix),
                    pl.BlockSpec((N_BRANCH, w, d), lambda i: (0, 0, 0)), pl.BlockSpec((d, d), fix)],
        out_specs=pl.BlockSpec((tm, d), row),
        out_shape=jax.ShapeDtypeStruct((t, d), F32),
        compiler_params=_params("parallel"),
        name="merge",
    )(x2, oa, ob, oc, od, g, wg, wb, wo)


def _ple(x, pe, gp, wpg, wp):
    gate = _sigmoid(_dot(_rms(x, gp).astype(BF16), wpg))
    return x + gate * _dot(pe.astype(BF16), wp)


def _ffn_prompt_kernel(x_ref, xp_ref, g_ref, wa_ref, wb_ref, cwa_ref, cwb_ref, cba_ref, cbb_ref, wd_ref,
                       pe_ref, gp_ref, wpg_ref, wp_ref, o_ref, sa_ref, sb_ref,
                       xn_scr, xnp_scr, acc_scr, ea_scr, eb_scr, *, tiles_per_seq):
    i, f = pl.program_id(0), pl.program_id(1)
    tm = x_ref.shape[0]
    halo = FFN_HALO

    @pl.when(f == 0)
    def _():
        xn_scr[...] = _rms(x_ref[...], g_ref[...]).astype(BF16)
        xnp_scr[...] = _rms(xp_ref[...], g_ref[...]).astype(BF16)
        acc_scr[...] = jnp.zeros_like(acc_scr)

    seq_start = (i % tiles_per_seq) == 0

    def conv_half(w_ref, cw_ref, cb_ref, e_scr, s_ref):
        u = _dot(xn_scr[...], w_ref[...])
        e_scr[0:halo, :] = jnp.where(seq_start, 0.0, _dot(xnp_scr[...], w_ref[...]))
        e_scr[halo:, :] = u
        y = cb_ref[...] + cw_ref[CONV_F - 1:CONV_F, :] * u
        for j in range(CONV_F - 1):
            y = y + cw_ref[j:j + 1, :] * e_scr[pl.ds(halo - (CONV_F - 1) + j, tm), :]
        s_ref[0] = e_scr[pl.ds(halo + tm - (CONV_F - 1), CONV_F - 1), :]
        return y

    ya = conv_half(wa_ref, cwa_ref, cba_ref, ea_scr, sa_ref)
    yb = conv_half(wb_ref, cwb_ref, cbb_ref, eb_scr, sb_ref)
    acc_scr[...] += _dot((_gelu(ya) * yb).astype(BF16), wd_ref[...])

    @pl.when(f == pl.num_programs(1) - 1)
    def _():
        o_ref[...] = _ple(x_ref[...] + acc_scr[...], pe_ref[...], gp_ref[...], wpg_ref[...], wp_ref[...])


def _ffn_prompt(layer, x2, seq_len, g, w_up, cw, cb, w_down, pe3, gp, wpg, wp):
    t, d = x2.shape
    dff = w_down.shape[0]
    tm = min(TM_FFN, seq_len)
    tf = TF_FFN
    halo = FFN_HALO
    assert seq_len % tm == 0 and dff % tf == 0 and tm % halo == 0
    nf = dff // tf
    tps = seq_len // tm
    dp = pe3.shape[2]
    fix = lambda i, f: (0, 0)
    st_shape = jax.ShapeDtypeStruct((t // tm, CONV_F - 1, dff), F32)
    st_spec = pl.BlockSpec((1, CONV_F - 1, tf), lambda i, f: (i, 0, f))
    return pl.pallas_call(
        functools.partial(_ffn_prompt_kernel, tiles_per_seq=tps),
        grid=(t // tm, nf),
        in_specs=[pl.BlockSpec((tm, d), lambda i, f: (i, 0)),
                  pl.BlockSpec((halo, d), lambda i, f: (jnp.maximum(i * (tm // halo) - 1, 0), 0)),
                  pl.BlockSpec((1, d), fix),
                  pl.BlockSpec((d, tf), lambda i, f: (0, f)),
                  pl.BlockSpec((d, tf), lambda i, f: (0, f + nf)),
                  pl.BlockSpec((CONV_F, tf), lambda i, f: (0, f)),
                  pl.BlockSpec((CONV_F, tf), lambda i, f: (0, f + nf)),
                  pl.BlockSpec((1, tf), lambda i, f: (0, f)),
                  pl.BlockSpec((1, tf), lambda i, f: (0, f + nf)),
                  pl.BlockSpec((tf, d), lambda i, f: (f, 0)),
                  pl.BlockSpec((None, tm, dp), lambda i, f: (layer, i, 0)),
                  pl.BlockSpec((1, d), fix), pl.BlockSpec((d, d), fix), pl.BlockSpec((dp, d), fix)],
        out_specs=[pl.BlockSpec((tm, d), lambda i, f: (i, 0)), st_spec, st_spec],
        out_shape=[jax.ShapeDtypeStruct((t, d), F32), st_shape, st_shape],
        scratch_shapes=[pltpu.VMEM((tm, d), BF16), pltpu.VMEM((halo, d), BF16), pltpu.VMEM((tm, d), F32),
                        pltpu.VMEM((tm + halo, tf), F32), pltpu.VMEM((tm + halo, tf), F32)],
        compiler_params=_params("arbitrary", "arbitrary"),
        name="ffn_prompt",
    )(x2, x2, g, w_up, w_up, cw, cw, cb, cb, w_down, pe3, gp, wpg, wp)


def _ffn_sample_kernel(x_ref, g_ref, wa_ref, wb_ref, cwa_ref, cwb_ref, cba_ref, cbb_ref, wd_ref,
                       ha0_ref, hb0_ref, ha1_ref, hb1_ref, pe_ref, gp_ref, wpg_ref, wp_ref,
                       o_ref, sa0_ref, sb0_ref, sa1_ref, sb1_ref, xn_scr, acc_scr, g_scr, *, n_new, db):
    f = pl.program_id(0)

    @pl.when(f == 0)
    def _():
        xn_scr[...] = _rms(x_ref[...], g_ref[...]).astype(BF16)
        acc_scr[...] = jnp.zeros_like(acc_scr)

    def conv_half(w_ref, cw_ref, cb_ref, h0_ref, h1_ref, s0_ref, s1_ref):
        u = _dot(xn_scr[...], w_ref[...])
        ext = [h0_ref[...], h1_ref[...]] + [u[t * db:(t + 1) * db] for t in range(n_new)]
        s0_ref[...] = ext[n_new]
        s1_ref[...] = ext[n_new + 1]
        ys = []
        for t in range(n_new):
            y = cb_ref[...] + cw_ref[0:1, :] * ext[t]
            for j in range(1, CONV_F):
                y = y + cw_ref[j:j + 1, :] * ext[t + j]
            ys.append(y)
        return ys

    ya = conv_half(wa_ref, cwa_ref, cba_ref, ha0_ref, ha1_ref, sa0_ref, sa1_ref)
    yb = conv_half(wb_ref, cwb_ref, cbb_ref, hb0_ref, hb1_ref, sb0_ref, sb1_ref)
    for t in range(n_new):
        g_scr[t * db:(t + 1) * db, :] = (_gelu(ya[t]) * yb[t]).astype(BF16)
    acc_scr[...] += _dot(g_scr[...], wd_ref[...])

    @pl.when(f == pl.num_programs(0) - 1)
    def _():
        o_ref[...] = _ple(x_ref[...] + acc_scr[...], pe_ref[...], gp_ref[...], wpg_ref[...], wp_ref[...])


def _ffn_sample(layer, x2, g, w_up, cw, cb, w_down, hist2, pe3, gp, wpg, wp, n_new):
    t, d = x2.shape
    db = t // n_new
    dff = w_down.shape[0]
    tf = TF_FFN
    nf = dff // tf
    dp = pe3.shape[2]
    fix = lambda f: (0, 0)
    hist = lambda blk: pl.BlockSpec((None, db, tf), lambda f, blk=blk: (layer, 0, f + blk * nf))
    st_shape = jax.ShapeDtypeStruct((db, dff), F32)
    st_spec = pl.BlockSpec((db, tf), lambda f: (0, f))
    return pl.pallas_call(
        functools.partial(_ffn_sample_kernel, n_new=n_new, db=db),
        grid=(nf,),
        in_specs=[pl.BlockSpec((t, d), fix), pl.BlockSpec((1, d), fix),
                  pl.BlockSpec((d, tf), lambda f: (0, f)), pl.BlockSpec((d, tf), lambda f: (0, f + nf)),
                  pl.BlockSpec((CONV_F, tf), lambda f: (0, f)), pl.BlockSpec((CONV_F, tf), lambda f: (0, f + nf)),
                  pl.BlockSpec((1, tf), lambda f: (0, f)), pl.BlockSpec((1, tf), lambda f: (0, f + nf)),
                  pl.BlockSpec((tf, d), lambda f: (f, 0)),
                  hist(0), hist(1), hist(2), hist(3),
                  pl.BlockSpec((None, t, dp), lambda f: (layer, 0, 0)),
                  pl.BlockSpec((1, d), fix), pl.BlockSpec((d, d), fix), pl.BlockSpec((dp, d), fix)],
        out_specs=[pl.BlockSpec((t, d), fix), st_spec, st_spec, st_spec, st_spec],
        out_shape=[jax.ShapeDtypeStruct((t, d), F32)] + [st_shape] * 4,
        scratch_shapes=[pltpu.VMEM((t, d), BF16), pltpu.VMEM((t, d), F32), pltpu.VMEM((t, tf), BF16)],
        compiler_params=_params("arbitrary"),
        name="ffn_sample",
    )(x2, g, w_up, w_up, cw, cw, cb, cb, w_down, hist2, hist2, hist2, hist2, pe3, gp, wpg, wp)


def _block_diag(w4):
    h, n, _ = w4.shape
    eye = jnp.eye(h, dtype=w4.dtype)
    return (eye[:, None, :, None] * w4[:, :, None, :]).reshape(h * n, h * n)


def kernel(x_prompt, x_sample, cache_k, cache_v, state_hgrn, state_rglru_h, state_rglru_conv, state_pool,
           state_ffn_conv, page_table, p_prompt, p_sample, norm_mix, w_in, q_norm, k_norm, sb_bias, lb_logits,
           hgrn_norm, conv_c_w, conv_c_b, w_rg_a, b_rg_a, w_rg_x, b_rg_x, lam, w_pool, pool_scale, w_branch,
           w_out, norm_ffn, w_up, conv_f_w, conv_f_b, w_down, norm_ple, w_ple_gate, w_ple):
    depth = w_in.shape[0]
    b, l, d = x_prompt.shape
    db, n_new, _ = x_sample.shape
    w = BRANCH_W
    n_mix = 10 * w
    dff = w_down.shape[1]
    dp = p_prompt.shape[-1]
    n_pool, page = cache_k.shape[1], cache_k.shape[2]
    pos0 = page_table.shape[1] * page

    lb_all, loglb_all, log1m_all = _lower_bounds(lb_logits.astype(F32))
    row = lambda a, i: a[i].reshape(1, -1)
    tile_heads = lambda a, i: jnp.tile(a[i], N_HEAD).reshape(1, w)

    xp = x_prompt.reshape(b * l, d)
    xs = x_sample.transpose(1, 0, 2).reshape(n_new * db, d)
    pe_p = p_prompt.reshape(depth, b * l, dp)
    pe_s = p_sample.transpose(0, 2, 1, 3).reshape(depth, n_new * db, dp)
    cache_kt = cache_k.transpose(0, 1, 3, 4, 2)
    cache_vt = cache_v.transpose(0, 1, 3, 4, 2)
    hgrn_state = state_hgrn.transpose(0, 2, 3, 4, 1).reshape(depth, -1, db)
    conv_state = state_rglru_conv.transpose(0, 2, 1, 3)
    pool_state = state_pool.transpose(0, 2, 1, 3)
    ffn_state = state_ffn_conv.reshape(depth, db, -1)

    outs = {k: [] for k in ("kp", "vp", "ks", "vs", "sp", "ss", "hp", "hs", "cp", "cs", "pp", "ps", "fp", "fs")}
    for i in range(depth):
        w_mix = w_in[i, :, :n_mix].astype(BF16)
        w_gate = w_in[i, :, n_mix:].astype(BF16)
        wg_c = jnp.concatenate([_block_diag(w_rg_a[i]), _block_diag(w_rg_x[i])], axis=1).astype(BF16)
        bg_c = jnp.concatenate([b_rg_a[i], b_rg_x[i]]).reshape(1, 2 * w)
        wp_d = _block_diag(w_pool[i]).astype(BF16)
        g_mix, g_ffn, g_ple = row(norm_mix, i), row(norm_ffn, i), row(norm_ple, i)
        qn, kn, gn = tile_heads(q_norm, i), tile_heads(k_norm, i), tile_heads(hgrn_norm, i)
        lb, loglb, log1m = row(lb_all, i), row(loglb_all, i), row(log1m_all, i)
        cw_c, cb_c, lam_i, sc_d = conv_c_w[i], row(conv_c_b, i), row(lam, i), row(pool_scale, i)
        w_br, w_o = w_branch[i].astype(BF16), w_out[i].astype(BF16)
        w_u, w_d = w_up[i].astype(BF16), w_down[i].astype(BF16)
        cw_f, cb_f = conv_f_w[i], row(conv_f_b, i)
        w_pg, w_pe = w_ple_gate[i].astype(BF16), w_ple[i].astype(BF16)
        bias = sb_bias[i].astype(F32)

        q, kt, vt, ktb, vtb, zb, zc, zd = _inproj_prompt(xp.reshape(b, l, d), g_mix, w_mix, qn, kn)
        o_a = _attn_prompt(bias, q, ktb, vtb)
        o_b, s_p = _hgrn_prompt(zb, lb, loglb, log1m, gn)
        o_c, h_p = _rglru_prompt(zc, cw_c, cb_c, wg_c, bg_c, lam_i)
        o_d = _pool_prompt(zd, wp_d, sc_d)
        f2 = lambda a: a.reshape(b * l, w)
        x1 = _merge(xp, f2(o_a), f2(o_b), f2(o_c), f2(o_d), g_mix, w_gate, w_br, w_o)
        xp, fa, fb = _ffn_prompt(i, x1, l, g_ffn, w_u, cw_f, cb_f, w_d, pe_p, g_ple, w_pg, w_pe)
        outs["kp"].append(kt)
        outs["vp"].append(vt)
        s_heads = s_p.reshape(b, N_HEAD, HEAD_W, N_HEAD, HEAD_W)
        outs["sp"].append(jnp.stack([s_heads[:, h, :, h, :] for h in range(N_HEAD)], axis=1))
        outs["hp"].append(h_p.reshape(b, w))
        outs["cp"].append(zc[:, l - (CONV_C - 1):, :w])
        outs["pp"].append(zd[:, l - POOL_HIST:, :])
        tps = fa.shape[0] // b
        outs["fp"].append(jnp.concatenate([fa[tps - 1::tps], fb[tps - 1::tps]], axis=-1))

        q, k, v, zb, zc, zd = _inproj_sample(xs, g_mix, w_mix, qn, kn)
        r4 = lambda a: a.reshape(n_new, db, 1, w)
        o_a = _attn_sample(i, page_table, bias, r4(q), r4(k), r4(v), cache_kt, cache_vt).reshape(n_new * db, w)
        o_b, s_s = _hgrn_sample(i, zb, hgrn_state, lb, gn, n_new)
        o_c, o_d, h_s, c_s, p_s = _cd_sample(zc, zd, state_rglru_h[i], conv_state[i], pool_state[i],
                                             cw_c, cb_c, wg_c, bg_c, lam_i, wp_d, sc_d, n_new, pos0)
        x1 = _merge(xs, o_a, o_b, o_c, o_d, g_mix, w_gate, w_br, w_o)
        xs, fa0, fb0, fa1, fb1 = _ffn_sample(i, x1, g_ffn, w_u, cw_f, cb_f, w_d, ffn_state, pe_s,
                                             g_ple, w_pg, w_pe, n_new)
        outs["ks"].append(k)
        outs["vs"].append(v)
        outs["ss"].append(s_s)
        outs["hs"].append(h_s)
        outs["cs"].append(c_s)
        outs["ps"].append(p_s)
        outs["fs"].append(jnp.stack([jnp.concatenate([fa0, fb0], axis=-1),
                                     jnp.concatenate([fa1, fb1], axis=-1)], axis=1))

    stk = lambda key: jnp.stack(outs[key], axis=0)
    to_batch_major = lambda a: a.reshape(depth, n_new, db, N_HEAD, HEAD_W).transpose(0, 2, 1, 3, 4)
    y_sample = xs.reshape(n_new, db, d).transpose(1, 0, 2)
    from_transposed = lambda a: a.reshape(depth, b, N_HEAD, HEAD_W, l).transpose(0, 1, 4, 2, 3)
    hgrn_s = stk("ss").reshape(depth, N_HEAD, HEAD_W, HEAD_W, db).transpose(0, 4, 1, 2, 3)
    rows_to_batch = lambda a: a.transpose(0, 2, 1, 3)
    return (xp.reshape(b, l, d), y_sample,
            from_transposed(stk("kp")), from_transposed(stk("vp")),
            to_batch_major(stk("ks")), to_batch_major(stk("vs")),
            stk("sp"), hgrn_s, stk("hp"), stk("hs"), stk("cp"), rows_to_batch(stk("cs")),
            stk("pp"), rows_to_batch(stk("ps")), stk("fp"), stk("fs"))
```

```python
import functools
import math

import jax
import jax.numpy as jnp
from jax import lax
from jax.experimental import pallas as pl
from jax.experimental.pallas import tpu as pltpu

F32 = jnp.float32
BF16 = jnp.bfloat16
EPS = 1e-6

N_HEAD = 4
HEAD_W = 64
BRANCH_W = N_HEAD * HEAD_W
N_BRANCH = 4
RG_C = 8.0
POOL_WINDOWS = (2, 4, 8, 16)
POOL_HIST = max(POOL_WINDOWS) - 1
CONV_C = 4
CONV_F = 3
HGRN_CHUNK = 64
HGRN_SUB = 16
NEG_BIG = -1e30
LOG2E = math.log2(math.e)

V7X_VMEM_LIMIT_BYTES = 56 * 1024 * 1024

TM_PROJ = 512
TQ_ATTN = 256
TK_ATTN = 256
TL_SCAN = 512
TL_HGRN = 256
TM_FFN = 1024
TF_FFN = 256
FFN_HALO = 16


def _params(*sem):
    return pltpu.CompilerParams(dimension_semantics=sem, vmem_limit_bytes=V7X_VMEM_LIMIT_BYTES)


def _dot(a, b):
    return jnp.dot(a, b, preferred_element_type=F32)


def _dot_nt(a, b):
    return lax.dot_general(a, b, (((1,), (1,)), ((), ())), preferred_element_type=F32)


def _dot_tn(a, b):
    return lax.dot_general(a, b, (((0,), (0,)), ((), ())), preferred_element_type=F32)


def _split2(x):
    hi = x.astype(BF16)
    lo = (x - hi.astype(F32)).astype(BF16)
    return hi, lo


def _split3(x):
    hi = x.astype(BF16)
    r = x - hi.astype(F32)
    mid = r.astype(BF16)
    lo = (r - mid.astype(F32)).astype(BF16)
    return hi, mid, lo


def _dot_x2(x, m):
    hi, lo = _split2(x)
    return _dot(hi, m) + _dot(lo, m)


def _sigmoid(x):
    return 1.0 / (1.0 + jnp.exp(-x))


def _softplus_tail(x):
    return jnp.log1p(jnp.exp(-jnp.abs(x)))


def _log_sigmoid(x):
    return jnp.minimum(x, 0.0) - _softplus_tail(x)


def _softplus(x):
    return jnp.maximum(x, 0.0) + _softplus_tail(x)


def _silu(x):
    return x * _sigmoid(x)


def _gelu(x):
    c = math.sqrt(2.0 / math.pi)
    return 0.5 * x * (1.0 + jnp.tanh(c * (x + 0.044715 * (x * x * x))))


def _rms(x, g):
    ms = jnp.mean(x * x, axis=-1, keepdims=True)
    return x * lax.rsqrt(ms + EPS) * g


def _head_id(shape, dim):
    return lax.shift_right_logical(lax.broadcasted_iota(jnp.int32, shape, dim), 6)


def _head_block_ones():
    n = BRANCH_W
    return jnp.where(_head_id((n, n), 0) == _head_id((n, n), 1), 1.0, 0.0).astype(BF16)


def _head_rms(a, g, bd):
    ms = _dot_x2(a * a, bd) * (1.0 / HEAD_W)
    return a * lax.rsqrt(ms + EPS) * g


def _shift_rows(x, d, fill):
    rolled = pltpu.roll(x, d, 0)
    row = lax.broadcasted_iota(jnp.int32, x.shape, 0)
    return jnp.where(row < d, fill, rolled)


def _lb_kernel(lg_ref, lb_ref, loglb_ref, log1m_ref):
    x = lg_ref[...]
    depth = x.shape[0]
    rows = [x[i:i + 1] for i in range(depth)]
    m = functools.reduce(jnp.maximum, rows)
    e = [jnp.exp(r - m) for r in rows]
    tot = functools.reduce(lambda a, b: a + b, e)
    zero = jnp.zeros_like(m)
    lb_ref[0:1, :] = zero
    loglb_ref[0:1, :] = jnp.full_like(m, -jnp.inf)
    log1m_ref[0:1, :] = zero
    acc = zero
    for i in range(1, depth):
        acc = acc + e[i] / tot
        lb_ref[i:i + 1, :] = acc
        loglb_ref[i:i + 1, :] = jnp.log(acc)
        log1m_ref[i:i + 1, :] = jnp.log1p(-acc)


def _lower_bounds(lb_logits):
    shp = jax.ShapeDtypeStruct(lb_logits.shape, F32)
    return pl.pallas_call(_lb_kernel, out_shape=(shp, shp, shp), name="hgrn_lower_bounds")(lb_logits)


def _inproj_common(x, g_ref, w_ref, qn_ref, kn_ref):
    h = _rms(x, g_ref[...]).astype(BF16)
    z = _dot(h, w_ref[...])
    bd = _head_block_ones()
    w = BRANCH_W
    q = _head_rms(z[:, 0:w], qn_ref[...], bd) * (HEAD_W ** -0.5 * LOG2E)
    k = _head_rms(z[:, w:2 * w], kn_ref[...], bd)
    return q, k, z[:, 2 * w:3 * w], z[:, 3 * w:7 * w], z[:, 7 * w:9 * w], z[:, 9 * w:10 * w]


def _inproj_prompt_kernel(x_ref, g_ref, w_ref, qn_ref, kn_ref,
                          q_o, kt_o, vt_o, ktb_o, vtb_o, zb_o, zc_o, zd_o):
    q, k, v, zb, zc, zd = _inproj_common(x_ref[0], g_ref, w_ref, qn_ref, kn_ref)
    q_o[0] = q.astype(BF16)
    kt, vt = k.T, v.T
    kt_o[0] = kt
    vt_o[0] = vt
    tk = ktb_o.shape[2]
    for c in range(ktb_o.shape[0]):
        ktb_o[c] = kt[:, c * tk:(c + 1) * tk].astype(BF16)
        vtb_o[c] = vt[:, c * tk:(c + 1) * tk].astype(BF16)
    zb_o[0] = zb
    zc_o[0] = zc
    zd_o[0] = zd


def _inproj_prompt(x3, g, w, qn, kn):
    b, l, d = x3.shape
    tm = min(TM_PROJ, l)
    tk = min(TK_ATTN, l)
    n = w.shape[1]
    bw = BRANCH_W
    per = tm // tk
    row = lambda i, j: (i, j, 0)
    fix = lambda i, j: (0, 0)
    col = lambda i, j: (i, 0, j)
    tiles = lambda i, j: (i * (l // tm) + j, 0, 0)
    tok = lambda c, dt: (jax.ShapeDtypeStruct((b, l, c), dt), pl.BlockSpec((1, tm, c), row))
    tr = (jax.ShapeDtypeStruct((b, bw, l), F32), pl.BlockSpec((1, bw, tm), col))
    trb = (jax.ShapeDtypeStruct((b * l // tk, bw, tk), BF16), pl.BlockSpec((per, bw, tk), tiles))
    outs = [tok(bw, BF16), tr, tr, trb, trb, tok(4 * bw, F32), tok(2 * bw, F32), tok(bw, F32)]
    return pl.pallas_call(
        _inproj_prompt_kernel,
        grid=(b, l // tm),
        in_specs=[pl.BlockSpec((1, tm, d), row), pl.BlockSpec((1, d), fix), pl.BlockSpec((d, n), fix),
                  pl.BlockSpec((1, bw), fix), pl.BlockSpec((1, bw), fix)],
        out_specs=[s for _, s in outs],
        out_shape=[s for s, _ in outs],
        compiler_params=_params("parallel", "parallel"),
        name="inproj_prompt",
    )(x3, g, w, qn, kn)


def _inproj_sample_kernel(x_ref, g_ref, w_ref, qn_ref, kn_ref, q_o, k_o, v_o, zb_o, zc_o, zd_o):
    for ref, val in zip((q_o, k_o, v_o, zb_o, zc_o, zd_o),
                        _inproj_common(x_ref[...], g_ref, w_ref, qn_ref, kn_ref)):
        ref[...] = val


def _inproj_sample(x2, g, w, qn, kn):
    t, d = x2.shape
    bw = BRANCH_W
    widths = (bw, bw, bw, 4 * bw, 2 * bw, bw)
    return pl.pallas_call(
        _inproj_sample_kernel,
        out_shape=[jax.ShapeDtypeStruct((t, c), F32) for c in widths],
        compiler_params=pltpu.CompilerParams(vmem_limit_bytes=V7X_VMEM_LIMIT_BYTES),
        name="inproj_sample",
    )(x2, g, w, qn, kn)


def _upper_ones(n):
    r = lax.broadcasted_iota(jnp.int32, (n, n), 0)
    c = lax.broadcasted_iota(jnp.int32, (n, n), 1)
    return jnp.where(r > c, 1.0, 0.0).astype(BF16)


def _sb_log_keep(z, mask):
    log_keep = -jnp.maximum(z, 0.0) - jnp.log2(1.0 + jnp.exp2(-jnp.abs(z)))
    return log_keep if mask is None else jnp.where(mask, log_keep, 0.0)


def _sb_finish(z, log_keep, local, carry, mask):
    w = jnp.exp2(z + log_keep + (local + carry))
    if mask is not None:
        w = jnp.where(mask, w, 0.0)
    return w.astype(BF16), carry + local[:, 0:1] + log_keep[:, 0:1]


def _attn_prompt_kernel(bias_ref, q_ref, kt_ref, vt_ref, o_ref):
    qi = pl.program_id(1)
    tq, tk = q_ref.shape[1], kt_ref.shape[2]
    per = tq // tk
    q = q_ref[0]
    heads = range(N_HEAD)
    lanes = lambda h: slice(h * HEAD_W, (h + 1) * HEAD_W)
    qs = [q[:, lanes(h)] for h in heads]
    bias = [bias_ref[h] * LOG2E for h in heads]
    upper = _upper_ones(tk)
    q_pos = qi * tq + lax.broadcasted_iota(jnp.int32, (tq, tk), 0)
    k_off = lax.broadcasted_iota(jnp.int32, (tq, tk), 1)

    def tile(kidx, carries, masked):
        mask = (kidx * tk + k_off < q_pos) if masked else None
        zs = [_dot(qs[h], kt_ref[kidx, lanes(h), :]) + bias[h] for h in heads]
        lks = [_sb_log_keep(z, mask) for z in zs]
        local = [_dot(lk.astype(BF16), upper) for lk in lks]
        fin = [_sb_finish(zs[h], lks[h], local[h], carries[h], mask) for h in heads]
        pvs = [_dot_nt(fin[h][0], vt_ref[kidx, lanes(h), :]) for h in heads]
        return pvs, [c for _, c in fin]

    add = lambda accs, pvs: [a + p for a, p in zip(accs, pvs)]
    first = qi * per
    accs, carries = tile(first + per - 1, [jnp.zeros((tq, 1), F32)] * N_HEAD, True)
    for m in range(per - 2, -1, -1):
        pvs, carries = tile(first + m, carries, True)
        accs = add(accs, pvs)

    def body(j, state):
        accs, carries = state
        pvs, carries = tile(first - 1 - j, carries, False)
        return add(accs, pvs), carries

    accs, _ = lax.fori_loop(0, first, body, (accs, carries))
    for h in heads:
        o_ref[0, :, lanes(h)] = accs[h]


def _attn_prompt(bias, q3, ktb, vtb):
    b, l, w = q3.shape
    tk = ktb.shape[2]
    tq = min(TQ_ATTN, l)
    nk = l // tk
    return pl.pallas_call(
        _attn_prompt_kernel,
        grid=(b, l // tq),
        in_specs=[pl.BlockSpec(memory_space=pltpu.SMEM),
                  pl.BlockSpec((1, tq, w), lambda i, j: (i, j, 0)),
                  pl.BlockSpec((nk, w, tk), lambda i, j: (i, 0, 0)),
                  pl.BlockSpec((nk, w, tk), lambda i, j: (i, 0, 0))],
        out_specs=pl.BlockSpec((1, tq, w), lambda i, j: (i, j, 0)),
        out_shape=jax.ShapeDtypeStruct((b, l, w), F32),
        compiler_params=_params("parallel", "arbitrary"),
        name="attn_prompt",
    )(bias, q3, ktb, vtb)


def _attn_sample_kernel(pt_ref, bias_ref, q_ref, kn_ref, vn_ref, *refs, n_pages, n_new):
    del pt_ref
    k_refs, v_refs, o_ref = refs[:n_pages], refs[n_pages:2 * n_pages], refs[2 * n_pages]
    page = k_refs[0].shape[-1]
    flat = lambda ref: ref[...].reshape(BRANCH_W, page).astype(BF16)
    rows = N_HEAD * 8
    row = lax.broadcasted_iota(jnp.int32, (rows, 1), 0)
    row_t = jnp.bitwise_and(row, 7)
    row_h = lax.shift_right_logical(row, 3)
    lane_head = _head_id((1, BRANCH_W), 1)
    qs = jnp.zeros((rows, BRANCH_W), F32)
    for t in range(n_new):
        qs = qs + jnp.where(row_t == t, q_ref[t, 0], 0.0)
    qs = jnp.where(row_h == lane_head, qs, 0.0)
    bias = jnp.zeros((rows, 1), F32)
    for h in range(N_HEAD):
        bias = jnp.where(row_h == h, bias_ref[h] * LOG2E, bias)

    carry = jnp.zeros((rows, 1), F32)
    out = jnp.zeros((rows, BRANCH_W), F32)
    for j in reversed(range(n_new)):
        z = jnp.sum(qs * kn_ref[j, 0], axis=-1, keepdims=True) + bias
        log_keep = -jnp.maximum(z, 0.0) - jnp.log2(1.0 + jnp.exp2(-jnp.abs(z)))
        seen = row_t > j
        w = jnp.where(seen, jnp.exp2(z + log_keep + carry), 0.0)
        out = out + w * vn_ref[j, 0]
        carry = carry + jnp.where(seen, log_keep, 0.0)

    kt = jnp.concatenate([flat(r) for r in k_refs], axis=1)
    vt = jnp.concatenate([flat(r) for r in v_refs], axis=1)
    z = _dot(qs.astype(BF16), kt) + bias
    log_keep = _sb_log_keep(z, None)
    cols = lambda a, p: a[:, p * page:(p + 1) * page]
    stacked = jnp.concatenate([cols(log_keep, p) for p in range(n_pages)], axis=0).astype(BF16)
    local = _dot(stacked, _upper_ones(page))
    later = [None] * n_pages
    for p in reversed(range(n_pages)):
        loc = local[p * rows:(p + 1) * rows]
        later[p] = loc + carry
        carry = carry + loc[:, 0:1] + cols(log_keep, p)[:, 0:1]
    w = jnp.exp2(z + log_keep + jnp.concatenate(later, axis=1)).astype(BF16)
    out = out + _dot_nt(w, vt)

    res = jnp.zeros((8, BRANCH_W), F32)
    for h in range(N_HEAD):
        res = res + jnp.where(lane_head == h, out[h * 8:(h + 1) * 8], 0.0)
    for t in range(n_new):
        o_ref[t, 0] = res[t:t + 1]


def _attn_sample(layer, page_table, bias, q4, kn4, vn4, cache_kt, cache_vt):
    n_new, db, _, w = q4.shape
    n_pages = page_table.shape[1]
    page = cache_kt.shape[4]
    new_spec = pl.BlockSpec((n_new, 1, 1, w), lambda b, pt, bs: (0, b, 0, 0))

    def page_spec(p):
        return pl.BlockSpec((None, None, N_HEAD, HEAD_W, page),
                            lambda b, pt, bs, p=p: (layer, pt[b, p], 0, 0, 0))

    grid_spec = pltpu.PrefetchScalarGridSpec(
        num_scalar_prefetch=2,
        grid=(db,),
        in_specs=[new_spec, new_spec, new_spec] + [page_spec(p) for p in range(n_pages)] * 2,
        out_specs=new_spec,
    )
    return pl.pallas_call(
        functools.partial(_attn_sample_kernel, n_pages=n_pages, n_new=n_new),
        grid_spec=grid_spec,
        out_shape=jax.ShapeDtypeStruct(q4.shape, F32),
        compiler_params=_params("arbitrary"),
        name="attn_sample",
    )(page_table, bias, q4, kn4, vn4, *([cache_kt] * n_pages), *([cache_vt] * n_pages))


def _hgrn_gates(qb, fr, lb, loglb, log1m):
    c = log1m + _log_sigmoid(fr)
    log_f = jnp.maximum(loglb, c) + jnp.log1p(jnp.exp(-jnp.abs(loglb - c)))
    key = (1.0 - lb) * _sigmoid(-fr)
    return _silu(qb), key, log_f


def _hgrn_prompt_kernel(z_ref, lb_ref, loglb_ref, log1m_ref, gn_ref, o_ref, s_ref,
                        st_scr, q_scr, k_scr, v_scr, b_scr, p_scr, vx_scr, od_scr):
    ti = pl.program_id(1)
    w = BRANCH_W
    ck, sub = HGRN_CHUNK, HGRN_SUB
    n_sub = ck // sub
    tl = z_ref.shape[1]

    @pl.when(ti == 0)
    def _():
        st_scr[...] = jnp.zeros_like(st_scr)

    bd = _head_block_ones()
    bd_mask = _head_id((w, w), 0) == _head_id((w, w), 1)
    r = lax.broadcasted_iota(jnp.int32, (ck, ck), 0)
    c = lax.broadcasted_iota(jnp.int32, (ck, ck), 1)
    lower_incl = jnp.where(c <= r, 1.0, 0.0).astype(BF16)
    row_ck = lax.broadcasted_iota(jnp.int32, (ck, 1), 0)
    row_sub = lax.broadcasted_iota(jnp.int32, (sub, 1), 0)
    stack_mask = (lax.shift_right_logical(lax.broadcasted_iota(jnp.int32, (N_HEAD * sub, w), 0), 4)
                  == _head_id((N_HEAD * sub, w), 1))
    lb, loglb, log1m, gn = lb_ref[...], loglb_ref[...], log1m_ref[...], gn_ref[...]

    def chunk(ci, _):
        r0 = pl.multiple_of(ci * ck, ck)
        zz = z_ref[0, pl.ds(r0, ck), :]
        qh, key, log_f = _hgrn_gates(zz[:, 0:w], zz[:, w:2 * w], lb, loglb, log1m)
        val = zz[:, 2 * w:3 * w]
        f_hi, f_mid, f_lo = _split3(log_f)
        b = _dot(lower_incl, f_hi) + _dot(lower_incl, f_mid) + _dot(lower_incl, f_lo)
        q_scr[...] = qh
        k_scr[...] = key
        v_scr[...] = val
        b_scr[...] = b

        st = st_scr[...]
        out = _dot_nt((qh * jnp.exp(b)).astype(BF16), st.astype(BF16))
        for j in range(n_sub - 1):
            e_j = b[(j + 1) * sub - 1:(j + 1) * sub]
            qj = qh * jnp.exp(jnp.where(row_ck >= (j + 1) * sub, b - e_j, NEG_BIG))
            kj = key[j * sub:(j + 1) * sub] * jnp.exp(e_j - b[j * sub:(j + 1) * sub])
            vj = val[j * sub:(j + 1) * sub]
            k_st = jnp.where(stack_mask, jnp.concatenate([kj] * N_HEAD, axis=0), 0.0)
            v_st = jnp.where(stack_mask, jnp.concatenate([vj] * N_HEAD, axis=0), 0.0)
            att = _dot_nt(qj.astype(BF16), k_st.astype(BF16))
            out = out + _dot(att.astype(BF16), v_st.astype(BF16))

        b_last = b[ck - 1:ck]
        k_end = key * jnp.exp(b_last - b)
        st_new = st * jnp.exp(b_last) + _dot_tn(val.astype(BF16), k_end.astype(BF16))
        st_scr[...] = jnp.where(bd_mask, st_new, 0.0)

        def diag(si, _):
            s0 = pl.multiple_of(si * sub, sub)
            q_i = q_scr[pl.ds(s0, sub), :]
            b_i = b_scr[pl.ds(s0, sub), :]
            for s in range(sub):
                k_s = k_scr[pl.ds(s0 + s, 1), :]
                b_s = b_scr[pl.ds(s0 + s, 1), :]
                v_s = v_scr[pl.ds(s0 + s, 1), :]
                p_scr[s * sub:(s + 1) * sub, :] = q_i * k_s * jnp.exp(jnp.where(row_sub >= s, b_i - b_s, NEG_BIG))
                vx_scr[s * sub:(s + 1) * sub, :] = jnp.broadcast_to(v_s, (sub, w))
            att = _dot_x2(p_scr[...], bd)
            od_scr[pl.ds(s0, sub), :] = jnp.sum((att * vx_scr[...]).reshape(sub, sub, w), axis=0)
            return 0

        lax.fori_loop(0, n_sub, diag, 0)
        out = out + od_scr[...]
        o_ref[0, pl.ds(r0, ck), :] = _head_rms(out, gn, bd) * _silu(zz[:, 3 * w:4 * w])
        return 0

    lax.fori_loop(0, tl // ck, chunk, 0)

    @pl.when(ti == pl.num_programs(1) - 1)
    def _():
        s_ref[0] = st_scr[...].T


def _hgrn_prompt(zb3, lb, loglb, log1m, gn):
    b, l, _ = zb3.shape
    w = BRANCH_W
    tl = min(TL_HGRN, l)
    assert l % tl == 0 and tl % HGRN_CHUNK == 0
    fix = lambda i, j: (0, 0)
    scr = lambda rows: pltpu.VMEM((rows, w), F32)
    return pl.pallas_call(
        _hgrn_prompt_kernel,
        grid=(b, l // tl),
        in_specs=[pl.BlockSpec((1, tl, 4 * w), lambda i, j: (i, j, 0))] + [pl.BlockSpec((1, w), fix)] * 4,
        out_specs=[pl.BlockSpec((1, tl, w), lambda i, j: (i, j, 0)),
                   pl.BlockSpec((1, w, w), lambda i, j: (i, 0, 0))],
        out_shape=[jax.ShapeDtypeStruct((b, l, w), F32), jax.ShapeDtypeStruct((b, w, w), F32)],
        scratch_shapes=[scr(w), scr(HGRN_CHUNK), scr(HGRN_CHUNK), scr(HGRN_CHUNK), scr(HGRN_CHUNK),
                        scr(HGRN_SUB * HGRN_SUB), scr(HGRN_SUB * HGRN_SUB), scr(HGRN_CHUNK)],
        compiler_params=_params("parallel", "arbitrary"),
        name="hgrn_prompt",
    )(zb3, lb, loglb, log1m, gn)


def _hgrn_sample_kernel(z_ref, s0_ref, lb_ref, gn_ref, o_ref, s_ref, f_scr, k_scr, q_scr, v_scr, o_scr,
                        *, n_new, db):
    j = pl.program_id(0)
    w = BRANCH_W
    k_per_step = s0_ref.shape[0] // HEAD_W
    steps_per_head = HEAD_W // k_per_step
    lb = lb_ref[...]

    @pl.when(j == 0)
    def _():
        for t in range(n_new):
            zz = z_ref[t * db:(t + 1) * db, :]
            fr = zz[:, w:2 * w]
            f_scr[t] = (lb + (1.0 - lb) * _sigmoid(fr)).T
            k_scr[t] = ((1.0 - lb) * _sigmoid(-fr)).T
            q_scr[t] = _silu(zz[:, 0:w]).T
            v_scr[t] = zz[:, 2 * w:3 * w].T
            o_scr[t] = jnp.zeros((w, db), F32)

    head = j // steps_per_head
    k_base = head * HEAD_W + (j % steps_per_head) * k_per_step
    v0 = pl.multiple_of(head * HEAD_W, HEAD_W)
    for kk in range(k_per_step):
        rows = slice(kk * HEAD_W, (kk + 1) * HEAD_W)
        s_k = s0_ref[rows, :]
        row = k_base + kk
        for t in range(n_new):
            s_k = (f_scr[t, pl.ds(row, 1), :] * s_k
                   + k_scr[t, pl.ds(row, 1), :] * v_scr[t, pl.ds(v0, HEAD_W), :])
            o_scr[t, pl.ds(v0, HEAD_W), :] += q_scr[t, pl.ds(row, 1), :] * s_k
        s_ref[rows, :] = s_k

    @pl.when(j == pl.num_programs(0) - 1)
    def _():
        bd = _head_block_ones()
        for t in range(n_new):
            gate = _silu(z_ref[t * db:(t + 1) * db, 3 * w:4 * w])
            o_ref[t * db:(t + 1) * db, :] = _head_rms(o_scr[t].T, gn_ref[...], bd) * gate


def _hgrn_sample(layer, zb, state_t, lb, gn, n_new):
    t, _ = zb.shape
    db = t // n_new
    w = BRANCH_W
    n_state = state_t.shape[1]
    rows = 16 * HEAD_W
    fix = lambda j: (0, 0)
    scr = pltpu.VMEM((n_new, w, db), F32)
    return pl.pallas_call(
        functools.partial(_hgrn_sample_kernel, n_new=n_new, db=db),
        grid=(n_state // rows,),
        in_specs=[pl.BlockSpec((t, 4 * w), fix),
                  pl.BlockSpec((None, rows, db), lambda j: (layer, j, 0)),
                  pl.BlockSpec((1, w), fix), pl.BlockSpec((1, w), fix)],
        out_specs=[pl.BlockSpec((t, w), fix), pl.BlockSpec((rows, db), lambda j: (j, 0))],
        out_shape=[jax.ShapeDtypeStruct((t, w), F32), jax.ShapeDtypeStruct((n_state, db), F32)],
        scratch_shapes=[scr] * 5,
        compiler_params=_params("arbitrary"),
        name="hgrn_sample",
    )(zb, state_t, lb, gn)


def _rglru_gates(xconv, wg, bg, sp_lam):
    w = BRANCH_W
    g = _sigmoid(_dot(xconv.astype(BF16), wg) + bg)
    log_a = -RG_C * g[:, 0:w] * sp_lam
    a = jnp.exp(log_a)
    one_minus_a2 = -jnp.tanh(log_a) * (a * a + 1.0)
    u = jnp.sqrt(one_minus_a2) * (g[:, w:2 * w] * xconv)
    return a, u


def _rglru_prompt_kernel(z_ref, cw_ref, cb_ref, wg_ref, bg_ref, lam_ref, o_ref, h_ref, ext_scr, h_scr):
    ti = pl.program_id(1)
    w = BRANCH_W
    tl = z_ref.shape[1]
    halo = 8

    @pl.when(ti == 0)
    def _():
        ext_scr[0:halo, :] = jnp.zeros((halo, w), F32)
        h_scr[...] = jnp.zeros_like(h_scr)

    x = z_ref[0, :, 0:w]
    ext_scr[halo:, :] = x
    xconv = cb_ref[...] + cw_ref[CONV_C - 1:CONV_C, :] * x
    for j in range(CONV_C - 1):
        xconv = xconv + cw_ref[j:j + 1, :] * ext_scr[pl.ds(halo - (CONV_C - 1) + j, tl), :]
    ext_scr[0:halo, :] = x[tl - halo:tl]
    a, u = _rglru_gates(xconv, wg_ref[...], bg_ref[...], _softplus(-lam_ref[...]))
    d = 1
    while d < tl:
        u = u + a * _shift_rows(u, d, 0.0)
        a = a * _shift_rows(a, d, 1.0)
        d *= 2
    h = u + a * h_scr[0:1, :]
    h_last = h[tl - 1:tl]
    h_scr[...] = jnp.broadcast_to(h_last, h_scr.shape)
    h_ref[0] = h_last
    o_ref[0] = h * _gelu(z_ref[0, :, w:2 * w])


def _rglru_prompt(zc3, cw, cb, wg, bg, lam):
    b, l, _ = zc3.shape
    w = BRANCH_W
    tl = min(TL_SCAN, l)
    fix = lambda i, j: (0, 0)
    return pl.pallas_call(
        _rglru_prompt_kernel,
        grid=(b, l // tl),
        in_specs=[pl.BlockSpec((1, tl, 2 * w), lambda i, j: (i, j, 0)),
                  pl.BlockSpec((CONV_C, w), fix), pl.BlockSpec((1, w), fix),
                  pl.BlockSpec((w, 2 * w), fix), pl.BlockSpec((1, 2 * w), fix), pl.BlockSpec((1, w), fix)],
        out_specs=[pl.BlockSpec((1, tl, w), lambda i, j: (i, j, 0)),
                   pl.BlockSpec((1, 1, w), lambda i, j: (i, 0, 0))],
        out_shape=[jax.ShapeDtypeStruct((b, l, w), F32), jax.ShapeDtypeStruct((b, 1, w), F32)],
        scratch_shapes=[pltpu.VMEM((tl + 8, w), F32), pltpu.VMEM((8, w), F32)],
        compiler_params=_params("parallel", "arbitrary"),
        name="rglru_prompt",
    )(zc3, cw, cb, wg, bg, lam)


def _pool_select(sums, x, pos, wp, scale):
    lane_group = _head_id((1, BRANCH_W), 1)
    pooled = jnp.zeros_like(x)
    for g, win in enumerate(POOL_WINDOWS):
        cnt = jnp.minimum(pos + 1, win).astype(F32)
        pooled = jnp.where(lane_group == g, sums[g] / cnt, pooled)
    return _dot((pooled - x).astype(BF16), wp) * scale


def _pool_prompt_kernel(x_ref, wp_ref, sc_ref, o_ref, ext_scr):
    ti = pl.program_id(1)
    w = BRANCH_W
    tl = x_ref.shape[1]
    halo = 16

    @pl.when(ti == 0)
    def _():
        ext_scr[0:halo, :] = jnp.zeros((halo, w), F32)

    x = x_ref[0]
    ext_scr[halo:, :] = x
    e = ext_scr[...]
    sums = []
    d = 1
    for _ in POOL_WINDOWS:
        e = e + pltpu.roll(e, d, 0)
        sums.append(e[halo:])
        d *= 2
    ext_scr[0:halo, :] = x[tl - halo:tl]
    pos = ti * tl + lax.broadcasted_iota(jnp.int32, (tl, 1), 0)
    o_ref[0] = _pool_select(sums, x, pos, wp_ref[...], sc_ref[...])


def _pool_prompt(zd3, wp, scale):
    b, l, w = zd3.shape
    tl = min(TL_SCAN, l)
    fix = lambda i, j: (0, 0)
    return pl.pallas_call(
        _pool_prompt_kernel,
        grid=(b, l // tl),
        in_specs=[pl.BlockSpec((1, tl, w), lambda i, j: (i, j, 0)),
                  pl.BlockSpec((w, w), fix), pl.BlockSpec((1, w), fix)],
        out_specs=pl.BlockSpec((1, tl, w), lambda i, j: (i, j, 0)),
        out_shape=jax.ShapeDtypeStruct((b, l, w), F32),
        scratch_shapes=[pltpu.VMEM((tl + 16, w), F32)],
        compiler_params=_params("parallel", "arbitrary"),
        name="pool_prompt",
    )(zd3, wp, scale)


def _cd_sample_kernel(zc_ref, zd_ref, h0_ref, ch_ref, ph_ref, cw_ref, cb_ref, wg_ref, bg_ref, lam_ref,
                      wp_ref, sc_ref, oc_ref, od_ref, h_ref, cn_ref, pn_ref, xc_scr, *, n_new, db, pos0):
    w = BRANCH_W
    slab = lambda ref, t, c0=0: ref[t * db:(t + 1) * db, c0:c0 + w]
    hist = lambda ref, j: ref[j]

    n_hist = CONV_C - 1
    ext = [hist(ch_ref, j) for j in range(n_hist)] + [slab(zc_ref, t) for t in range(n_new)]
    for t in range(n_new):
        acc = cb_ref[...] + cw_ref[0:1, :] * ext[t]
        for j in range(1, CONV_C):
            acc = acc + cw_ref[j:j + 1, :] * ext[t + j]
        xc_scr[t * db:(t + 1) * db, :] = acc
    a, u = _rglru_gates(xc_scr[...], wg_ref[...], bg_ref[...], _softplus(-lam_ref[...]))
    h = h0_ref[...]
    for t in range(n_new):
        h = a[t * db:(t + 1) * db] * h + u[t * db:(t + 1) * db]
        oc_ref[t * db:(t + 1) * db, :] = h * _gelu(slab(zc_ref, t, w))
    h_ref[...] = h
    for j in range(n_hist):
        cn_ref[j] = ext[n_new + j]

    pext = [hist(ph_ref, j) for j in range(POOL_HIST)] + [slab(zd_ref, t) for t in range(n_new)]
    for t in range(n_new):
        sums, run, k = [], None, 0
        for win in POOL_WINDOWS:
            while k < win:
                term = pext[POOL_HIST + t - k]
                run = term if run is None else run + term
                k += 1
            sums.append(run)
        pos = jnp.full((db, 1), pos0 + t, jnp.int32)
        od_ref[t * db:(t + 1) * db, :] = _pool_select(sums, pext[POOL_HIST + t], pos, wp_ref[...], sc_ref[...])
    for j in range(POOL_HIST):
        pn_ref[j] = pext[n_new + j]


def _cd_sample(zc, zd, h0, conv_hist, pool_hist, cw, cb, wg, bg, lam, wp, scale, n_new, pos0):
    t, _ = zc.shape
    db = t // n_new
    w = BRANCH_W
    shp = lambda c: jax.ShapeDtypeStruct((db, c), F32)
    return pl.pallas_call(
        functools.partial(_cd_sample_kernel, n_new=n_new, db=db, pos0=pos0),
        out_shape=[jax.ShapeDtypeStruct((t, w), F32), jax.ShapeDtypeStruct((t, w), F32),
                   shp(w), jax.ShapeDtypeStruct((CONV_C - 1, db, w), F32),
                   jax.ShapeDtypeStruct((POOL_HIST, db, w), F32)],
        scratch_shapes=[pltpu.VMEM((t, w), F32)],
        compiler_params=pltpu.CompilerParams(vmem_limit_bytes=V7X_VMEM_LIMIT_BYTES),
        name="rglru_pool_sample",
    )(zc, zd, h0, conv_hist, pool_hist, cw, cb, wg, bg, lam, wp, scale)


def _merge_kernel(x_ref, oa_ref, ob_ref, oc_ref, od_ref, g_ref, wg_ref, wb_ref, wo_ref, o_ref):
    x = x_ref[...]
    d = x.shape[1]
    h = _rms(x, g_ref[...]).astype(BF16)
    mix = jnp.zeros(x.shape, F32)
    for n, br in enumerate((oa_ref, ob_ref, oc_ref, od_ref)):
        gate = _sigmoid(_dot(h, wg_ref[:, n * d:(n + 1) * d]))
        mix = mix + gate * _dot(br[...].astype(BF16), wb_ref[n])
    o_ref[...] = x + _dot(mix.astype(BF16), wo_ref[...])


def _merge(x2, oa, ob, oc, od, g, wg, wb, wo):
    t, d = x2.shape
    tm = min(TM_PROJ, t)
    w = BRANCH_W
    row = lambda i: (i, 0)
    fix = lambda i: (0, 0)
    return pl.pallas_call(
        _merge_kernel,
        grid=(t // tm,),
        in_specs=[pl.BlockSpec((tm, d), row)] + [pl.BlockSpec((tm, w), row)] * 4
                 + [pl.BlockSpec((1, d), fix), pl.BlockSpec((d, N_BRANCH * d), fix),
                    pl.BlockSpec((N_BRANCH, w, d), lambda i: (0, 0, 0)), pl.BlockSpec((d, d), fix)],
        out_specs=pl.BlockSpec((tm, d), row),
        out_shape=jax.ShapeDtypeStruct((t, d), F32),
        compiler_params=_params("parallel"),
        name="merge",
    )(x2, oa, ob, oc, od, g, wg, wb, wo)


def _ple(x, pe, gp, wpg, wp):
    gate = _sigmoid(_dot(_rms(x, gp).astype(BF16), wpg))
    return x + gate * _dot(pe.astype(BF16), wp)


def _ffn_prompt_kernel(x_ref, xp_ref, g_ref, wa_ref, wb_ref, cwa_ref, cwb_ref, cba_ref, cbb_ref, wd_ref,
                       pe_ref, gp_ref, wpg_ref, wp_ref, o_ref, sa_ref, sb_ref,
                       xn_scr, xnp_scr, acc_scr, ea_scr, eb_scr, *, tiles_per_seq):
    i, f = pl.program_id(0), pl.program_id(1)
    tm = x_ref.shape[0]
    halo = FFN_HALO

    @pl.when(f == 0)
    def _():
        xn_scr[...] = _rms(x_ref[...], g_ref[...]).astype(BF16)
        xnp_scr[...] = _rms(xp_ref[...], g_ref[...]).astype(BF16)
        acc_scr[...] = jnp.zeros_like(acc_scr)

    seq_start = (i % tiles_per_seq) == 0

    def conv_half(w_ref, cw_ref, cb_ref, e_scr, s_ref):
        u = _dot(xn_scr[...], w_ref[...])
        e_scr[0:halo, :] = jnp.where(seq_start, 0.0, _dot(xnp_scr[...], w_ref[...]))
        e_scr[halo:, :] = u
        y = cb_ref[...] + cw_ref[CONV_F - 1:CONV_F, :] * u
        for j in range(CONV_F - 1):
            y = y + cw_ref[j:j + 1, :] * e_scr[pl.ds(halo - (CONV_F - 1) + j, tm), :]
        s_ref[0] = e_scr[pl.ds(halo + tm - (CONV_F - 1), CONV_F - 1), :]
        return y

    ya = conv_half(wa_ref, cwa_ref, cba_ref, ea_scr, sa_ref)
    yb = conv_half(wb_ref, cwb_ref, cbb_ref, eb_scr, sb_ref)
    acc_scr[...] += _dot((_gelu(ya) * yb).astype(BF16), wd_ref[...])

    @pl.when(f == pl.num_programs(1) - 1)
    def _():
        o_ref[...] = _ple(x_ref[...] + acc_scr[...], pe_ref[...], gp_ref[...], wpg_ref[...], wp_ref[...])


def _ffn_prompt(layer, x2, seq_len, g, w_up, cw, cb, w_down, pe3, gp, wpg, wp):
    t, d = x2.shape
    dff = w_down.shape[0]
    tm = min(TM_FFN, seq_len)
    tf = TF_FFN
    halo = FFN_HALO
    assert seq_len % tm == 0 and dff % tf == 0 and tm % halo == 0
    nf = dff // tf
    tps = seq_len // tm
    dp = pe3.shape[2]
    fix = lambda i, f: (0, 0)
    st_shape = jax.ShapeDtypeStruct((t // tm, CONV_F - 1, dff), F32)
    st_spec = pl.BlockSpec((1, CONV_F - 1, tf), lambda i, f: (i, 0, f))
    return pl.pallas_call(
        functools.partial(_ffn_prompt_kernel, tiles_per_seq=tps),
        grid=(t // tm, nf),
        in_specs=[pl.BlockSpec((tm, d), lambda i, f: (i, 0)),
                  pl.BlockSpec((halo, d), lambda i, f: (jnp.maximum(i * (tm // halo) - 1, 0), 0)),
                  pl.BlockSpec((1, d), fix),
                  pl.BlockSpec((d, tf), lambda i, f: (0, f)),
                  pl.BlockSpec((d, tf), lambda i, f: (0, f + nf)),
                  pl.BlockSpec((CONV_F, tf), lambda i, f: (0, f)),
                  pl.BlockSpec((CONV_F, tf), lambda i, f: (0, f + nf)),
                  pl.BlockSpec((1, tf), lambda i, f: (0, f)),
                  pl.BlockSpec((1, tf), lambda i, f: (0, f + nf)),
                  pl.BlockSpec((tf, d), lambda i, f: (f, 0)),
                  pl.BlockSpec((None, tm, dp), lambda i, f: (layer, i, 0)),
                  pl.BlockSpec((1, d), fix), pl.BlockSpec((d, d), fix), pl.BlockSpec((dp, d), fix)],
        out_specs=[pl.BlockSpec((tm, d), lambda i, f: (i, 0)), st_spec, st_spec],
        out_shape=[jax.ShapeDtypeStruct((t, d), F32), st_shape, st_shape],
        scratch_shapes=[pltpu.VMEM((tm, d), BF16), pltpu.VMEM((halo, d), BF16), pltpu.VMEM((tm, d), F32),
                        pltpu.VMEM((tm + halo, tf), F32), pltpu.VMEM((tm + halo, tf), F32)],
        compiler_params=_params("arbitrary", "arbitrary"),
        name="ffn_prompt",
    )(x2, x2, g, w_up, w_up, cw, cw, cb, cb, w_down, pe3, gp, wpg, wp)


def _ffn_sample_kernel(x_ref, g_ref, wa_ref, wb_ref, cwa_ref, cwb_ref, cba_ref, cbb_ref, wd_ref,
                       ha0_ref, hb0_ref, ha1_ref, hb1_ref, pe_ref, gp_ref, wpg_ref, wp_ref,
                       o_ref, sa0_ref, sb0_ref, sa1_ref, sb1_ref, xn_scr, acc_scr, g_scr, *, n_new, db):
    f = pl.program_id(0)

    @pl.when(f == 0)
    def _():
        xn_scr[...] = _rms(x_ref[...], g_ref[...]).astype(BF16)
        acc_scr[...] = jnp.zeros_like(acc_scr)

    def conv_half(w_ref, cw_ref, cb_ref, h0_ref, h1_ref, s0_ref, s1_ref):
        u = _dot(xn_scr[...], w_ref[...])
        ext = [h0_ref[...], h1_ref[...]] + [u[t * db:(t + 1) * db] for t in range(n_new)]
        s0_ref[...] = ext[n_new]
        s1_ref[...] = ext[n_new + 1]
        ys = []
        for t in range(n_new):
            y = cb_ref[...] + cw_ref[0:1, :] * ext[t]
            for j in range(1, CONV_F):
                y = y + cw_ref[j:j + 1, :] * ext[t + j]
            ys.append(y)
        return ys

    ya = conv_half(wa_ref, cwa_ref, cba_ref, ha0_ref, ha1_ref, sa0_ref, sa1_ref)
    yb = conv_half(wb_ref, cwb_ref, cbb_ref, hb0_ref, hb1_ref, sb0_ref, sb1_ref)
    for t in range(n_new):
        g_scr[t * db:(t + 1) * db, :] = (_gelu(ya[t]) * yb[t]).astype(BF16)
    acc_scr[...] += _dot(g_scr[...], wd_ref[...])

    @pl.when(f == pl.num_programs(0) - 1)
    def _():
        o_ref[...] = _ple(x_ref[...] + acc_scr[...], pe_ref[...], gp_ref[...], wpg_ref[...], wp_ref[...])


def _ffn_sample(layer, x2, g, w_up, cw, cb, w_down, hist2, pe3, gp, wpg, wp, n_new):
    t, d = x2.shape
    db = t // n_new
    dff = w_down.shape[0]
    tf = TF_FFN
    nf = dff // tf
    dp = pe3.shape[2]
    fix = lambda f: (0, 0)
    hist = lambda blk: pl.BlockSpec((None, db, tf), lambda f, blk=blk: (layer, 0, f + blk * nf))
    st_shape = jax.ShapeDtypeStruct((db, dff), F32)
    st_spec = pl.BlockSpec((db, tf), lambda f: (0, f))
    return pl.pallas_call(
        functools.partial(_ffn_sample_kernel, n_new=n_new, db=db),
        grid=(nf,),
        in_specs=[pl.BlockSpec((t, d), fix), pl.BlockSpec((1, d), fix),
                  pl.BlockSpec((d, tf), lambda f: (0, f)), pl.BlockSpec((d, tf), lambda f: (0, f + nf)),
                  pl.BlockSpec((CONV_F, tf), lambda f: (0, f)), pl.BlockSpec((CONV_F, tf), lambda f: (0, f + nf)),
                  pl.BlockSpec((1, tf), lambda f: (0, f)), pl.BlockSpec((1, tf), lambda f: (0, f + nf)),
                  pl.BlockSpec((tf, d), lambda f: (f, 0)),
                  hist(0), hist(1), hist(2), hist(3),
                  pl.BlockSpec((None, t, dp), lambda f: (layer, 0, 0)),
                  pl.BlockSpec((1, d), fix), pl.BlockSpec((d, d), fix), pl.BlockSpec((dp, d), fix)],
        out_specs=[pl.BlockSpec((t, d), fix), st_spec, st_spec, st_spec, st_spec],
        out_shape=[jax.ShapeDtypeStruct((t, d), F32)] + [st_shape] * 4,
        scratch_shapes=[pltpu.VMEM((t, d), BF16), pltpu.VMEM((t, d), F32), pltpu.VMEM((t, tf), BF16)],
        compiler_params=_params("arbitrary"),
        name="ffn_sample",
    )(x2, g, w_up, w_up, cw, cw, cb, cb, w_down, hist2, hist2, hist2, hist2, pe3, gp, wpg, wp)


def _block_diag(w4):
    h, n, _ = w4.shape
    eye = jnp.eye(h, dtype=w4.dtype)
    return (eye[:, None, :, None] * w4[:, :, None, :]).reshape(h * n, h * n)


def kernel(x_prompt, x_sample, cache_k, cache_v, state_hgrn, state_rglru_h, state_rglru_conv, state_pool,
           state_ffn_conv, page_table, p_prompt, p_sample, norm_mix, w_in, q_norm, k_norm, sb_bias, lb_logits,
           hgrn_norm, conv_c_w, conv_c_b, w_rg_a, b_rg_a, w_rg_x, b_rg_x, lam, w_pool, pool_scale, w_branch,
           w_out, norm_ffn, w_up, conv_f_w, conv_f_b, w_down, norm_ple, w_ple_gate, w_ple):
    depth = w_in.shape[0]
    b, l, d = x_prompt.shape
    db, n_new, _ = x_sample.shape
    w = BRANCH_W
    n_mix = 10 * w
    dff = w_down.shape[1]
    dp = p_prompt.shape[-1]
    n_pool, page = cache_k.shape[1], cache_k.shape[2]
    pos0 = page_table.shape[1] * page

    lb_all, loglb_all, log1m_all = _lower_bounds(lb_logits.astype(F32))
    row = lambda a, i: a[i].reshape(1, -1)
    tile_heads = lambda a, i: jnp.tile(a[i], N_HEAD).reshape(1, w)

    xp = x_prompt.reshape(b * l, d)
    xs = x_sample.transpose(1, 0, 2).reshape(n_new * db, d)
    pe_p = p_prompt.reshape(depth, b * l, dp)
    pe_s = p_sample.transpose(0, 2, 1, 3).reshape(depth, n_new * db, dp)
    cache_kt = cache_k.transpose(0, 1, 3, 4, 2)
    cache_vt = cache_v.transpose(0, 1, 3, 4, 2)
    hgrn_state = state_hgrn.transpose(0, 2, 3, 4, 1).reshape(depth, -1, db)
    conv_state = state_rglru_conv.transpose(0, 2, 1, 3)
    pool_state = state_pool.transpose(0, 2, 1, 3)
    ffn_state = state_ffn_conv.reshape(depth, db, -1)

    outs = {k: [] for k in ("kp", "vp", "ks", "vs", "sp", "ss", "hp", "hs", "cp", "cs", "pp", "ps", "fp", "fs")}
    for i in range(depth):
        w_mix = w_in[i, :, :n_mix].astype(BF16)
        w_gate = w_in[i, :, n_mix:].astype(BF16)
        wg_c = jnp.concatenate([_block_diag(w_rg_a[i]), _block_diag(w_rg_x[i])], axis=1).astype(BF16)
        bg_c = jnp.concatenate([b_rg_a[i], b_rg_x[i]]).reshape(1, 2 * w)
        wp_d = _block_diag(w_pool[i]).astype(BF16)
        g_mix, g_ffn, g_ple = row(norm_mix, i), row(norm_ffn, i), row(norm_ple, i)
        qn, kn, gn = tile_heads(q_norm, i), tile_heads(k_norm, i), tile_heads(hgrn_norm, i)
        lb, loglb, log1m = row(lb_all, i), row(loglb_all, i), row(log1m_all, i)
        cw_c, cb_c, lam_i, sc_d = conv_c_w[i], row(conv_c_b, i), row(lam, i), row(pool_scale, i)
        w_br, w_o = w_branch[i].astype(BF16), w_out[i].astype(BF16)
        w_u, w_d = w_up[i].astype(BF16), w_down[i].astype(BF16)
        cw_f, cb_f = conv_f_w[i], row(conv_f_b, i)
        w_pg, w_pe = w_ple_gate[i].astype(BF16), w_ple[i].astype(BF16)
        bias = sb_bias[i].astype(F32)

        q, kt, vt, ktb, vtb, zb, zc, zd = _inproj_prompt(xp.reshape(b, l, d), g_mix, w_mix, qn, kn)
        o_a = _attn_prompt(bias, q, ktb, vtb)
        o_b, s_p = _hgrn_prompt(zb, lb, loglb, log1m, gn)
        o_c, h_p = _rglru_prompt(zc, cw_c, cb_c, wg_c, bg_c, lam_i)
        o_d = _pool_prompt(zd, wp_d, sc_d)
        f2 = lambda a: a.reshape(b * l, w)
        x1 = _merge(xp, f2(o_a), f2(o_b), f2(o_c), f2(o_d), g_mix, w_gate, w_br, w_o)
        xp, fa, fb = _ffn_prompt(i, x1, l, g_ffn, w_u, cw_f, cb_f, w_d, pe_p, g_ple, w_pg, w_pe)
        outs["kp"].append(kt)
        outs["vp"].append(vt)
        s_heads = s_p.reshape(b, N_HEAD, HEAD_W, N_HEAD, HEAD_W)
        outs["sp"].append(jnp.stack([s_heads[:, h, :, h, :] for h in range(N_HEAD)], axis=1))
        outs["hp"].append(h_p.reshape(b, w))
        outs["cp"].append(zc[:, l - (CONV_C - 1):, :w])
        outs["pp"].append(zd[:, l - POOL_HIST:, :])
        tps = fa.shape[0] // b
        outs["fp"].append(jnp.concatenate([fa[tps - 1::tps], fb[tps - 1::tps]], axis=-1))

        q, k, v, zb, zc, zd = _inproj_sample(xs, g_mix, w_mix, qn, kn)
        r4 = lambda a: a.reshape(n_new, db, 1, w)
        o_a = _attn_sample(i, page_table, bias, r4(q), r4(k), r4(v), cache_kt, cache_vt).reshape(n_new * db, w)
        o_b, s_s = _hgrn_sample(i, zb, hgrn_state, lb, gn, n_new)
        o_c, o_d, h_s, c_s, p_s = _cd_sample(zc, zd, state_rglru_h[i], conv_state[i], pool_state[i],
                                             cw_c, cb_c, wg_c, bg_c, lam_i, wp_d, sc_d, n_new, pos0)
        x1 = _merge(xs, o_a, o_b, o_c, o_d, g_mix, w_gate, w_br, w_o)
        xs, fa0, fb0, fa1, fb1 = _ffn_sample(i, x1, g_ffn, w_u, cw_f, cb_f, w_d, ffn_state, pe_s,
                                             g_ple, w_pg, w_pe, n_new)
        outs["ks"].append(k)
        outs["vs"].append(v)
        outs["ss"].append(s_s)
        outs["hs"].append(h_s)
        outs["cs"].append(c_s)
        outs["ps"].append(p_s)
        outs["fs"].append(jnp.stack([jnp.concatenate([fa0, fb0], axis=-1),
                                     jnp.concatenate([fa1, fb1], axis=-1)], axis=1))

    stk = lambda key: jnp.stack(outs[key], axis=0)
    to_batch_major = lambda a: a.reshape(depth, n_new, db, N_HEAD, HEAD_W).transpose(0, 2, 1, 3, 4)
    y_sample = xs.reshape(n_new, db, d).transpose(1, 0, 2)
    from_transposed = lambda a: a.reshape(depth, b, N_HEAD, HEAD_W, l).transpose(0, 1, 4, 2, 3)
    hgrn_s = stk("ss").reshape(depth, N_HEAD, HEAD_W, HEAD_W, db).transpose(0, 4, 1, 2, 3)
    rows_to_batch = lambda a: a.transpose(0, 2, 1, 3)
    return (xp.reshape(b, l, d), y_sample,
            from_transposed(stk("kp")), from_transposed(stk("vp")),
            to_batch_major(stk("ks")), to_batch_major(stk("vs")),
            stk("sp"), hgrn_s, stk("hp"), stk("hs"), stk("cp"), rows_to_batch(stk("cs")),
            stk("pp"), rows_to_batch(stk("ps")), stk("fp"), stk("fs"))
```

```python
import functools
import math

import jax
import jax.numpy as jnp
from jax import lax
from jax.experimental import pallas as pl
from jax.experimental.pallas import tpu as pltpu

F32 = jnp.float32
BF16 = jnp.bfloat16
EPS = 1e-6

N_HEAD = 4
HEAD_W = 64
BRANCH_W = N_HEAD * HEAD_W
N_BRANCH = 4
RG_C = 8.0
POOL_WINDOWS = (2, 4, 8, 16)
POOL_HIST = max(POOL_WINDOWS) - 1
CONV_C = 4
CONV_F = 3
HGRN_CHUNK = 64
HGRN_SUB = 16
NEG_BIG = -1e30
LOG2E = math.log2(math.e)

V7X_VMEM_LIMIT_BYTES = 56 * 1024 * 1024

TM_PROJ = 512
TQ_ATTN = 256
TK_ATTN = 256
TL_SCAN = 512
TL_HGRN = 256
TM_FFN = 1024
TF_FFN = 256
FFN_HALO = 16
FFN_STRIP = 256


def _params(*sem):
    return pltpu.CompilerParams(dimension_semantics=sem, vmem_limit_bytes=V7X_VMEM_LIMIT_BYTES)


def _dot(a, b):
    return jnp.dot(a, b, preferred_element_type=F32)


def _dot_nt(a, b):
    return lax.dot_general(a, b, (((1,), (1,)), ((), ())), preferred_element_type=F32)


def _dot_tn(a, b):
    return lax.dot_general(a, b, (((0,), (0,)), ((), ())), preferred_element_type=F32)


def _split2(x):
    hi = x.astype(BF16)
    lo = (x - hi.astype(F32)).astype(BF16)
    return hi, lo


def _split3(x):
    hi = x.astype(BF16)
    r = x - hi.astype(F32)
    mid = r.astype(BF16)
    lo = (r - mid.astype(F32)).astype(BF16)
    return hi, mid, lo


def _dot_x2(x, m):
    hi, lo = _split2(x)
    return _dot(hi, m) + _dot(lo, m)


def _sigmoid(x):
    return 1.0 / (1.0 + jnp.exp(-x))


def _softplus_tail(x):
    return jnp.log1p(jnp.exp(-jnp.abs(x)))


def _log_sigmoid(x):
    return jnp.minimum(x, 0.0) - _softplus_tail(x)


def _softplus(x):
    return jnp.maximum(x, 0.0) + _softplus_tail(x)


def _silu(x):
    return x * _sigmoid(x)


def _gelu(x):
    c = math.sqrt(2.0 / math.pi)
    return 0.5 * x * (1.0 + jnp.tanh(c * (x + 0.044715 * (x * x * x))))


def _rms(x, g):
    ms = jnp.mean(x * x, axis=-1, keepdims=True)
    return x * lax.rsqrt(ms + EPS) * g


def _head_id(shape, dim):
    return lax.shift_right_logical(lax.broadcasted_iota(jnp.int32, shape, dim), 6)


def _head_block_ones():
    n = BRANCH_W
    return jnp.where(_head_id((n, n), 0) == _head_id((n, n), 1), 1.0, 0.0).astype(BF16)


def _head_rms(a, g, bd):
    ms = _dot_x2(a * a, bd) * (1.0 / HEAD_W)
    return a * lax.rsqrt(ms + EPS) * g


def _shift_rows(x, d, fill):
    rolled = pltpu.roll(x, d, 0)
    row = lax.broadcasted_iota(jnp.int32, x.shape, 0)
    return jnp.where(row < d, fill, rolled)


def _lb_kernel(lg_ref, lb_ref, loglb_ref, log1m_ref):
    x = lg_ref[...]
    depth = x.shape[0]
    rows = [x[i:i + 1] for i in range(depth)]
    m = functools.reduce(jnp.maximum, rows)
    e = [jnp.exp(r - m) for r in rows]
    tot = functools.reduce(lambda a, b: a + b, e)
    zero = jnp.zeros_like(m)
    lb_ref[0:1, :] = zero
    loglb_ref[0:1, :] = jnp.full_like(m, -jnp.inf)
    log1m_ref[0:1, :] = zero
    acc = zero
    for i in range(1, depth):
        acc = acc + e[i] / tot
        lb_ref[i:i + 1, :] = acc
        loglb_ref[i:i + 1, :] = jnp.log(acc)
        log1m_ref[i:i + 1, :] = jnp.log1p(-acc)


def _lower_bounds(lb_logits):
    shp = jax.ShapeDtypeStruct(lb_logits.shape, F32)
    return pl.pallas_call(_lb_kernel, out_shape=(shp, shp, shp), name="hgrn_lower_bounds")(lb_logits)


def _inproj_common(x, g_ref, w_ref, qn_ref, kn_ref):
    h = _rms(x, g_ref[...]).astype(BF16)
    z = _dot(h, w_ref[...])
    bd = _head_block_ones()
    w = BRANCH_W
    q = _head_rms(z[:, 0:w], qn_ref[...], bd) * (HEAD_W ** -0.5 * LOG2E)
    k = _head_rms(z[:, w:2 * w], kn_ref[...], bd)
    return q, k, z[:, 2 * w:3 * w], z[:, 3 * w:7 * w], z[:, 7 * w:9 * w], z[:, 9 * w:10 * w]


def _inproj_prompt_kernel(x_ref, g_ref, w_ref, qn_ref, kn_ref,
                          q_o, kt_o, vt_o, ktb_o, vtb_o, zb_o, zc_o, zd_o):
    q, k, v, zb, zc, zd = _inproj_common(x_ref[0], g_ref, w_ref, qn_ref, kn_ref)
    q_o[0] = q.astype(BF16)
    kt, vt = k.T, v.T
    kt_o[0] = kt
    vt_o[0] = vt
    tk = ktb_o.shape[2]
    for c in range(ktb_o.shape[0]):
        ktb_o[c] = kt[:, c * tk:(c + 1) * tk].astype(BF16)
        vtb_o[c] = vt[:, c * tk:(c + 1) * tk].astype(BF16)
    zb_o[0] = zb
    zc_o[0] = zc
    zd_o[0] = zd


def _inproj_prompt(x3, g, w, qn, kn):
    b, l, d = x3.shape
    tm = min(TM_PROJ, l)
    tk = min(TK_ATTN, l)
    n = w.shape[1]
    bw = BRANCH_W
    per = tm // tk
    row = lambda i, j: (i, j, 0)
    fix = lambda i, j: (0, 0)
    col = lambda i, j: (i, 0, j)
    tiles = lambda i, j: (i * (l // tm) + j, 0, 0)
    tok = lambda c, dt: (jax.ShapeDtypeStruct((b, l, c), dt), pl.BlockSpec((1, tm, c), row))
    tr = (jax.ShapeDtypeStruct((b, bw, l), F32), pl.BlockSpec((1, bw, tm), col))
    trb = (jax.ShapeDtypeStruct((b * l // tk, bw, tk), BF16), pl.BlockSpec((per, bw, tk), tiles))
    outs = [tok(bw, BF16), tr, tr, trb, trb, tok(4 * bw, F32), tok(2 * bw, F32), tok(bw, F32)]
    return pl.pallas_call(
        _inproj_prompt_kernel,
        grid=(b, l // tm),
        in_specs=[pl.BlockSpec((1, tm, d), row), pl.BlockSpec((1, d), fix), pl.BlockSpec((d, n), fix),
                  pl.BlockSpec((1, bw), fix), pl.BlockSpec((1, bw), fix)],
        out_specs=[s for _, s in outs],
        out_shape=[s for s, _ in outs],
        compiler_params=_params("parallel", "parallel"),
        name="inproj_prompt",
    )(x3, g, w, qn, kn)


def _inproj_sample_kernel(x_ref, g_ref, w_ref, qn_ref, kn_ref, q_o, k_o, v_o, zb_o, zc_o, zd_o):
    for ref, val in zip((q_o, k_o, v_o, zb_o, zc_o, zd_o),
                        _inproj_common(x_ref[...], g_ref, w_ref, qn_ref, kn_ref)):
        ref[...] = val


def _inproj_sample(x2, g, w, qn, kn):
    t, d = x2.shape
    bw = BRANCH_W
    widths = (bw, bw, bw, 4 * bw, 2 * bw, bw)
    return pl.pallas_call(
        _inproj_sample_kernel,
        out_shape=[jax.ShapeDtypeStruct((t, c), F32) for c in widths],
        compiler_params=pltpu.CompilerParams(vmem_limit_bytes=V7X_VMEM_LIMIT_BYTES),
        name="inproj_sample",
    )(x2, g, w, qn, kn)


def _upper_ones(n):
    r = lax.broadcasted_iota(jnp.int32, (n, n), 0)
    c = lax.broadcasted_iota(jnp.int32, (n, n), 1)
    return jnp.where(r > c, 1.0, 0.0).astype(BF16)


def _sb_log_keep(z, mask):
    log_keep = -jnp.maximum(z, 0.0) - jnp.log2(1.0 + jnp.exp2(-jnp.abs(z)))
    return log_keep if mask is None else jnp.where(mask, log_keep, 0.0)


def _sb_finish(z, log_keep, local, carry, mask):
    w = jnp.exp2(z + log_keep + (local + carry))
    if mask is not None:
        w = jnp.where(mask, w, 0.0)
    return w.astype(BF16), carry + local[:, 0:1] + log_keep[:, 0:1]


def _attn_prompt_kernel(bias_ref, q_ref, kt_ref, vt_ref, o_ref):
    qi = pl.program_id(1)
    tq, tk = q_ref.shape[1], kt_ref.shape[2]
    per = tq // tk
    q = q_ref[0]
    heads = range(N_HEAD)
    lanes = lambda h: slice(h * HEAD_W, (h + 1) * HEAD_W)
    qs = [q[:, lanes(h)] for h in heads]
    bias = [bias_ref[h] * LOG2E for h in heads]
    upper = _upper_ones(tk)
    q_pos = qi * tq + lax.broadcasted_iota(jnp.int32, (tq, tk), 0)
    k_off = lax.broadcasted_iota(jnp.int32, (tq, tk), 1)

    def tile(kidx, carries, masked):
        mask = (kidx * tk + k_off < q_pos) if masked else None
        zs = [_dot(qs[h], kt_ref[kidx, lanes(h), :]) + bias[h] for h in heads]
        lks = [_sb_log_keep(z, mask) for z in zs]
        local = [_dot(lk.astype(BF16), upper) for lk in lks]
        fin = [_sb_finish(zs[h], lks[h], local[h], carries[h], mask) for h in heads]
        pvs = [_dot_nt(fin[h][0], vt_ref[kidx, lanes(h), :]) for h in heads]
        return pvs, [c for _, c in fin]

    add = lambda accs, pvs: [a + p for a, p in zip(accs, pvs)]
    first = qi * per
    accs, carries = tile(first + per - 1, [jnp.zeros((tq, 1), F32)] * N_HEAD, True)
    for m in range(per - 2, -1, -1):
        pvs, carries = tile(first + m, carries, True)
        accs = add(accs, pvs)

    def body(j, state):
        accs, carries = state
        pvs, carries = tile(first - 1 - j, carries, False)
        return add(accs, pvs), carries

    accs, _ = lax.fori_loop(0, first, body, (accs, carries))
    for h in heads:
        o_ref[0, :, lanes(h)] = accs[h]


def _attn_prompt(bias, q3, ktb, vtb):
    b, l, w = q3.shape
    tk = ktb.shape[2]
    tq = min(TQ_ATTN, l)
    nk = l // tk
    return pl.pallas_call(
        _attn_prompt_kernel,
        grid=(b, l // tq),
        in_specs=[pl.BlockSpec(memory_space=pltpu.SMEM),
                  pl.BlockSpec((1, tq, w), lambda i, j: (i, j, 0)),
                  pl.BlockSpec((nk, w, tk), lambda i, j: (i, 0, 0)),
                  pl.BlockSpec((nk, w, tk), lambda i, j: (i, 0, 0))],
        out_specs=pl.BlockSpec((1, tq, w), lambda i, j: (i, j, 0)),
        out_shape=jax.ShapeDtypeStruct((b, l, w), F32),
        compiler_params=_params("parallel", "arbitrary"),
        name="attn_prompt",
    )(bias, q3, ktb, vtb)


def _attn_sample_kernel(pt_ref, bias_ref, q_ref, kn_ref, vn_ref, *refs, n_pages, n_new):
    del pt_ref
    k_refs, v_refs, o_ref = refs[:n_pages], refs[n_pages:2 * n_pages], refs[2 * n_pages]
    page = k_refs[0].shape[-1]
    flat = lambda ref: ref[...].reshape(BRANCH_W, page).astype(BF16)
    rows = N_HEAD * 8
    row = lax.broadcasted_iota(jnp.int32, (rows, 1), 0)
    row_t = jnp.bitwise_and(row, 7)
    row_h = lax.shift_right_logical(row, 3)
    lane_head = _head_id((1, BRANCH_W), 1)
    qs = jnp.zeros((rows, BRANCH_W), F32)
    for t in range(n_new):
        qs = qs + jnp.where(row_t == t, q_ref[t, 0], 0.0)
    qs = jnp.where(row_h == lane_head, qs, 0.0)
    bias = jnp.zeros((rows, 1), F32)
    for h in range(N_HEAD):
        bias = jnp.where(row_h == h, bias_ref[h] * LOG2E, bias)

    carry = jnp.zeros((rows, 1), F32)
    out = jnp.zeros((rows, BRANCH_W), F32)
    for j in reversed(range(n_new)):
        z = jnp.sum(qs * kn_ref[j, 0], axis=-1, keepdims=True) + bias
        log_keep = -jnp.maximum(z, 0.0) - jnp.log2(1.0 + jnp.exp2(-jnp.abs(z)))
        seen = row_t > j
        w = jnp.where(seen, jnp.exp2(z + log_keep + carry), 0.0)
        out = out + w * vn_ref[j, 0]
        carry = carry + jnp.where(seen, log_keep, 0.0)

    kt = jnp.concatenate([flat(r) for r in k_refs], axis=1)
    vt = jnp.concatenate([flat(r) for r in v_refs], axis=1)
    z = _dot(qs.astype(BF16), kt) + bias
    log_keep = _sb_log_keep(z, None)
    cols = lambda a, p: a[:, p * page:(p + 1) * page]
    stacked = jnp.concatenate([cols(log_keep, p) for p in range(n_pages)], axis=0).astype(BF16)
    local = _dot(stacked, _upper_ones(page))
    later = [None] * n_pages
    for p in reversed(range(n_pages)):
        loc = local[p * rows:(p + 1) * rows]
        later[p] = loc + carry
        carry = carry + loc[:, 0:1] + cols(log_keep, p)[:, 0:1]
    w = jnp.exp2(z + log_keep + jnp.concatenate(later, axis=1)).astype(BF16)
    out = out + _dot_nt(w, vt)

    res = jnp.zeros((8, BRANCH_W), F32)
    for h in range(N_HEAD):
        res = res + jnp.where(lane_head == h, out[h * 8:(h + 1) * 8], 0.0)
    for t in range(n_new):
        o_ref[t, 0] = res[t:t + 1]


def _attn_sample(layer, page_table, bias, q4, kn4, vn4, cache_kt, cache_vt):
    n_new, db, _, w = q4.shape
    n_pages = page_table.shape[1]
    page = cache_kt.shape[4]
    new_spec = pl.BlockSpec((n_new, 1, 1, w), lambda b, pt, bs: (0, b, 0, 0))

    def page_spec(p):
        return pl.BlockSpec((None, None, N_HEAD, HEAD_W, page),
                            lambda b, pt, bs, p=p: (layer, pt[b, p], 0, 0, 0))

    grid_spec = pltpu.PrefetchScalarGridSpec(
        num_scalar_prefetch=2,
        grid=(db,),
        in_specs=[new_spec, new_spec, new_spec] + [page_spec(p) for p in range(n_pages)] * 2,
        out_specs=new_spec,
    )
    return pl.pallas_call(
        functools.partial(_attn_sample_kernel, n_pages=n_pages, n_new=n_new),
        grid_spec=grid_spec,
        out_shape=jax.ShapeDtypeStruct(q4.shape, F32),
        compiler_params=_params("arbitrary"),
        name="attn_sample",
    )(page_table, bias, q4, kn4, vn4, *([cache_kt] * n_pages), *([cache_vt] * n_pages))


def _hgrn_gates(qb, fr, lb, loglb, log1m):
    c = log1m + _log_sigmoid(fr)
    log_f = jnp.maximum(loglb, c) + jnp.log1p(jnp.exp(-jnp.abs(loglb - c)))
    key = (1.0 - lb) * _sigmoid(-fr)
    return _silu(qb), key, log_f


def _hgrn_prompt_kernel(z_ref, lb_ref, loglb_ref, log1m_ref, gn_ref, o_ref, s_ref,
                        st_scr, q_scr, k_scr, v_scr, b_scr, p_scr, vx_scr, od_scr):
    ti = pl.program_id(0)
    w = BRANCH_W
    ck, sub = HGRN_CHUNK, HGRN_SUB
    n_sub = ck // sub
    nb, tl = z_ref.shape[0], z_ref.shape[1]
    rows = range(nb)

    @pl.when(ti == 0)
    def _():
        st_scr[...] = jnp.zeros_like(st_scr)

    bd = _head_block_ones()
    bd_mask = _head_id((w, w), 0) == _head_id((w, w), 1)
    r = lax.broadcasted_iota(jnp.int32, (ck, ck), 0)
    c = lax.broadcasted_iota(jnp.int32, (ck, ck), 1)
    lower_incl = jnp.where(c <= r, 1.0, 0.0).astype(BF16)
    row_ck = lax.broadcasted_iota(jnp.int32, (ck, 1), 0)
    row_sub = lax.broadcasted_iota(jnp.int32, (sub, 1), 0)
    stack_mask = (lax.shift_right_logical(lax.broadcasted_iota(jnp.int32, (N_HEAD * sub, w), 0), 4)
                  == _head_id((N_HEAD * sub, w), 1))
    lb, loglb, log1m, gn = lb_ref[...], loglb_ref[...], log1m_ref[...], gn_ref[...]
    stack = lambda a: jnp.where(stack_mask, jnp.concatenate([a] * N_HEAD, axis=0), 0.0).astype(BF16)

    def chunk(ci, _):
        r0 = pl.multiple_of(ci * ck, ck)
        zz = [z_ref[n, pl.ds(r0, ck), :] for n in rows]
        gates = [_hgrn_gates(z[:, 0:w], z[:, w:2 * w], lb, loglb, log1m) for z in zz]
        qh, key = [g[0] for g in gates], [g[1] for g in gates]
        val = [z[:, 2 * w:3 * w] for z in zz]
        parts = [_split3(g[2]) for g in gates]
        b = [_dot(lower_incl, hi) + _dot(lower_incl, mid) + _dot(lower_incl, lo)
             for hi, mid, lo in parts]
        for n in rows:
            q_scr[n] = qh[n]
            k_scr[n] = key[n]
            v_scr[n] = val[n]
            b_scr[n] = b[n]

        st = [st_scr[n] for n in rows]
        out = [_dot_nt((qh[n] * jnp.exp(b[n])).astype(BF16), st[n].astype(BF16)) for n in rows]
        for j in range(n_sub - 1):
            blk = slice(j * sub, (j + 1) * sub)
            e_j = [x[(j + 1) * sub - 1:(j + 1) * sub] for x in b]
            qj = [(qh[n] * jnp.exp(jnp.where(row_ck >= (j + 1) * sub, b[n] - e_j[n], NEG_BIG))).astype(BF16)
                  for n in rows]
            k_st = [stack(key[n][blk] * jnp.exp(e_j[n] - b[n][blk])) for n in rows]
            att = [_dot_nt(qj[n], k_st[n]).astype(BF16) for n in rows]
            out = [out[n] + _dot(att[n], stack(val[n][blk])) for n in rows]

        b_last = [x[ck - 1:ck] for x in b]
        upd = [_dot_tn(val[n].astype(BF16), (key[n] * jnp.exp(b_last[n] - b[n])).astype(BF16)) for n in rows]
        for n in rows:
            st_scr[n] = jnp.where(bd_mask, st[n] * jnp.exp(b_last[n]) + upd[n], 0.0)

        def diag(n, _):
            for si in range(n_sub):
                s0 = si * sub
                q_i = q_scr[n, s0:s0 + sub, :]
                b_i = b_scr[n, s0:s0 + sub, :]
                for s in range(sub):
                    k_s = k_scr[n, s0 + s:s0 + s + 1, :]
                    b_s = b_scr[n, s0 + s:s0 + s + 1, :]
                    v_s = v_scr[n, s0 + s:s0 + s + 1, :]
                    dst = slice((s0 + s) * sub, (s0 + s + 1) * sub)
                    p_scr[dst, :] = (q_i * k_s * jnp.exp(jnp.where(row_sub >= s, b_i - b_s, NEG_BIG))).astype(BF16)
                    vx_scr[dst, :] = jnp.broadcast_to(v_s, (sub, w))
            att = _dot(p_scr[...], bd)
            od_scr[n] = jnp.sum((att * vx_scr[...]).reshape(n_sub, sub, sub, w), axis=1).reshape(ck, w)
            return 0

        lax.fori_loop(0, nb, diag, 0)
        for n in rows:
            o_ref[n, pl.ds(r0, ck), :] = (_head_rms(out[n] + od_scr[n], gn, bd) * _silu(zz[n][:, 3 * w:4 * w]))
        return 0

    lax.fori_loop(0, tl // ck, chunk, 0)

    @pl.when(ti == pl.num_programs(0) - 1)
    def _():
        for n in rows:
            s_ref[n] = st_scr[n].T


def _hgrn_prompt(zb3, lb, loglb, log1m, gn):
    b, l, _ = zb3.shape
    w = BRANCH_W
    ck = HGRN_CHUNK
    tl = min(TL_HGRN, l)
    assert l % tl == 0 and tl % ck == 0
    fix = lambda j: (0, 0)
    scr = lambda rows, dt=F32: pltpu.VMEM((b, rows, w), dt)
    pairs = ck * HGRN_SUB
    return pl.pallas_call(
        _hgrn_prompt_kernel,
        grid=(l // tl,),
        in_specs=[pl.BlockSpec((b, tl, 4 * w), lambda j: (0, j, 0))] + [pl.BlockSpec((1, w), fix)] * 4,
        out_specs=[pl.BlockSpec((b, tl, w), lambda j: (0, j, 0)),
                   pl.BlockSpec((b, w, w), lambda j: (0, 0, 0))],
        out_shape=[jax.ShapeDtypeStruct((b, l, w), F32), jax.ShapeDtypeStruct((b, w, w), F32)],
        scratch_shapes=[scr(w), scr(ck), scr(ck), scr(ck), scr(ck),
                        pltpu.VMEM((pairs, w), BF16), pltpu.VMEM((pairs, w), F32), scr(ck)],
        compiler_params=_params("arbitrary"),
        name="hgrn_prompt",
    )(zb3, lb, loglb, log1m, gn)


def _hgrn_sample_kernel(z_ref, s0_ref, lb_ref, gn_ref, o_ref, s_ref, f_scr, k_scr, q_scr, v_scr, o_scr,
                        *, n_new, db):
    j = pl.program_id(0)
    w = BRANCH_W
    k_per_step = s0_ref.shape[0] // HEAD_W
    steps_per_head = HEAD_W // k_per_step
    lb = lb_ref[...]

    @pl.when(j == 0)
    def _():
        for t in range(n_new):
            zz = z_ref[t * db:(t + 1) * db, :]
            fr = zz[:, w:2 * w]
            f_scr[t] = (lb + (1.0 - lb) * _sigmoid(fr)).T
            k_scr[t] = ((1.0 - lb) * _sigmoid(-fr)).T
            q_scr[t] = _silu(zz[:, 0:w]).T
            v_scr[t] = zz[:, 2 * w:3 * w].T
            o_scr[t] = jnp.zeros((w, db), F32)

    head = j // steps_per_head
    k_base = head * HEAD_W + (j % steps_per_head) * k_per_step
    v0 = pl.multiple_of(head * HEAD_W, HEAD_W)
    for kk in range(k_per_step):
        rows = slice(kk * HEAD_W, (kk + 1) * HEAD_W)
        s_k = s0_ref[rows, :]
        row = k_base + kk
        for t in range(n_new):
            s_k = (f_scr[t, pl.ds(row, 1), :] * s_k
                   + k_scr[t, pl.ds(row, 1), :] * v_scr[t, pl.ds(v0, HEAD_W), :])
            o_scr[t, pl.ds(v0, HEAD_W), :] += q_scr[t, pl.ds(row, 1), :] * s_k
        s_ref[rows, :] = s_k

    @pl.when(j == pl.num_programs(0) - 1)
    def _():
        bd = _head_block_ones()
        for t in range(n_new):
            gate = _silu(z_ref[t * db:(t + 1) * db, 3 * w:4 * w])
            o_ref[t * db:(t + 1) * db, :] = _head_rms(o_scr[t].T, gn_ref[...], bd) * gate


def _hgrn_sample(layer, zb, state_t, lb, gn, n_new):
    t, _ = zb.shape
    db = t // n_new
    w = BRANCH_W
    n_state = state_t.shape[1]
    rows = 16 * HEAD_W
    fix = lambda j: (0, 0)
    scr = pltpu.VMEM((n_new, w, db), F32)
    return pl.pallas_call(
        functools.partial(_hgrn_sample_kernel, n_new=n_new, db=db),
        grid=(n_state // rows,),
        in_specs=[pl.BlockSpec((t, 4 * w), fix),
                  pl.BlockSpec((None, rows, db), lambda j: (layer, j, 0)),
                  pl.BlockSpec((1, w), fix), pl.BlockSpec((1, w), fix)],
        out_specs=[pl.BlockSpec((t, w), fix), pl.BlockSpec((rows, db), lambda j: (j, 0))],
        out_shape=[jax.ShapeDtypeStruct((t, w), F32), jax.ShapeDtypeStruct((n_state, db), F32)],
        scratch_shapes=[scr] * 5,
        compiler_params=_params("arbitrary"),
        name="hgrn_sample",
    )(zb, state_t, lb, gn)


def _rglru_gates(xconv, wg, bg, sp_lam):
    w = BRANCH_W
    g = _sigmoid(_dot(xconv.astype(BF16), wg) + bg)
    log_a = -RG_C * g[:, 0:w] * sp_lam
    a = jnp.exp(log_a)
    one_minus_a2 = -jnp.tanh(log_a) * (a * a + 1.0)
    u = jnp.sqrt(one_minus_a2) * (g[:, w:2 * w] * xconv)
    return a, u


def _rglru_prompt_kernel(z_ref, cw_ref, cb_ref, wg_ref, bg_ref, lam_ref, o_ref, h_ref, ext_scr, h_scr):
    ti = pl.program_id(1)
    w = BRANCH_W
    tl = z_ref.shape[1]
    halo = 8

    @pl.when(ti == 0)
    def _():
        ext_scr[0:halo, :] = jnp.zeros((halo, w), F32)
        h_scr[...] = jnp.zeros_like(h_scr)

    x = z_ref[0, :, 0:w]
    ext_scr[halo:, :] = x
    xconv = cb_ref[...] + cw_ref[CONV_C - 1:CONV_C, :] * x
    for j in range(CONV_C - 1):
        xconv = xconv + cw_ref[j:j + 1, :] * ext_scr[pl.ds(halo - (CONV_C - 1) + j, tl), :]
    ext_scr[0:halo, :] = x[tl - halo:tl]
    a, u = _rglru_gates(xconv, wg_ref[...], bg_ref[...], _softplus(-lam_ref[...]))
    d = 1
    while d < tl:
        u = u + a * _shift_rows(u, d, 0.0)
        a = a * _shift_rows(a, d, 1.0)
        d *= 2
    h = u + a * h_scr[0:1, :]
    h_last = h[tl - 1:tl]
    h_scr[...] = jnp.broadcast_to(h_last, h_scr.shape)
    h_ref[0] = h_last
    o_ref[0] = h * _gelu(z_ref[0, :, w:2 * w])


def _rglru_prompt(zc3, cw, cb, wg, bg, lam):
    b, l, _ = zc3.shape
    w = BRANCH_W
    tl = min(TL_SCAN, l)
    fix = lambda i, j: (0, 0)
    return pl.pallas_call(
        _rglru_prompt_kernel,
        grid=(b, l // tl),
        in_specs=[pl.BlockSpec((1, tl, 2 * w), lambda i, j: (i, j, 0)),
                  pl.BlockSpec((CONV_C, w), fix), pl.BlockSpec((1, w), fix),
                  pl.BlockSpec((w, 2 * w), fix), pl.BlockSpec((1, 2 * w), fix), pl.BlockSpec((1, w), fix)],
        out_specs=[pl.BlockSpec((1, tl, w), lambda i, j: (i, j, 0)),
                   pl.BlockSpec((1, 1, w), lambda i, j: (i, 0, 0))],
        out_shape=[jax.ShapeDtypeStruct((b, l, w), F32), jax.ShapeDtypeStruct((b, 1, w), F32)],
        scratch_shapes=[pltpu.VMEM((tl + 8, w), F32), pltpu.VMEM((8, w), F32)],
        compiler_params=_params("parallel", "arbitrary"),
        name="rglru_prompt",
    )(zc3, cw, cb, wg, bg, lam)


def _pool_select(sums, x, pos, wp, scale):
    lane_group = _head_id((1, BRANCH_W), 1)
    pooled = jnp.zeros_like(x)
    for g, win in enumerate(POOL_WINDOWS):
        cnt = jnp.minimum(pos + 1, win).astype(F32)
        pooled = jnp.where(lane_group == g, sums[g] / cnt, pooled)
    return _dot((pooled - x).astype(BF16), wp) * scale


def _pool_prompt_kernel(x_ref, wp_ref, sc_ref, o_ref, ext_scr):
    ti = pl.program_id(1)
    w = BRANCH_W
    tl = x_ref.shape[1]
    halo = 16

    @pl.when(ti == 0)
    def _():
        ext_scr[0:halo, :] = jnp.zeros((halo, w), F32)

    x = x_ref[0]
    ext_scr[halo:, :] = x
    e = ext_scr[...]
    sums = []
    d = 1
    for _ in POOL_WINDOWS:
        e = e + pltpu.roll(e, d, 0)
        sums.append(e[halo:])
        d *= 2
    ext_scr[0:halo, :] = x[tl - halo:tl]
    pos = ti * tl + lax.broadcasted_iota(jnp.int32, (tl, 1), 0)
    o_ref[0] = _pool_select(sums, x, pos, wp_ref[...], sc_ref[...])


def _pool_prompt(zd3, wp, scale):
    b, l, w = zd3.shape
    tl = min(TL_SCAN, l)
    fix = lambda i, j: (0, 0)
    return pl.pallas_call(
        _pool_prompt_kernel,
        grid=(b, l // tl),
        in_specs=[pl.BlockSpec((1, tl, w), lambda i, j: (i, j, 0)),
                  pl.BlockSpec((w, w), fix), pl.BlockSpec((1, w), fix)],
        out_specs=pl.BlockSpec((1, tl, w), lambda i, j: (i, j, 0)),
        out_shape=jax.ShapeDtypeStruct((b, l, w), F32),
        scratch_shapes=[pltpu.VMEM((tl + 16, w), F32)],
        compiler_params=_params("parallel", "arbitrary"),
        name="pool_prompt",
    )(zd3, wp, scale)


def _cd_sample_kernel(zc_ref, zd_ref, h0_ref, ch_ref, ph_ref, cw_ref, cb_ref, wg_ref, bg_ref, lam_ref,
                      wp_ref, sc_ref, oc_ref, od_ref, h_ref, cn_ref, pn_ref, xc_scr, *, n_new, db, pos0):
    w = BRANCH_W
    slab = lambda ref, t, c0=0: ref[t * db:(t + 1) * db, c0:c0 + w]
    hist = lambda ref, j: ref[j]

    n_hist = CONV_C - 1
    ext = [hist(ch_ref, j) for j in range(n_hist)] + [slab(zc_ref, t) for t in range(n_new)]
    for t in range(n_new):
        acc = cb_ref[...] + cw_ref[0:1, :] * ext[t]
        for j in range(1, CONV_C):
            acc = acc + cw_ref[j:j + 1, :] * ext[t + j]
        xc_scr[t * db:(t + 1) * db, :] = acc
    a, u = _rglru_gates(xc_scr[...], wg_ref[...], bg_ref[...], _softplus(-lam_ref[...]))
    h = h0_ref[...]
    for t in range(n_new):
        h = a[t * db:(t + 1) * db] * h + u[t * db:(t + 1) * db]
        oc_ref[t * db:(t + 1) * db, :] = h * _gelu(slab(zc_ref, t, w))
    h_ref[...] = h
    for j in range(n_hist):
        cn_ref[j] = ext[n_new + j]

    pext = [hist(ph_ref, j) for j in range(POOL_HIST)] + [slab(zd_ref, t) for t in range(n_new)]
    for t in range(n_new):
        sums, run, k = [], None, 0
        for win in POOL_WINDOWS:
            while k < win:
                term = pext[POOL_HIST + t - k]
                run = term if run is None else run + term
                k += 1
            sums.append(run)
        pos = jnp.full((db, 1), pos0 + t, jnp.int32)
        od_ref[t * db:(t + 1) * db, :] = _pool_select(sums, pext[POOL_HIST + t], pos, wp_ref[...], sc_ref[...])
    for j in range(POOL_HIST):
        pn_ref[j] = pext[n_new + j]


def _cd_sample(zc, zd, h0, conv_hist, pool_hist, cw, cb, wg, bg, lam, wp, scale, n_new, pos0):
    t, _ = zc.shape
    db = t // n_new
    w = BRANCH_W
    shp = lambda c: jax.ShapeDtypeStruct((db, c), F32)
    return pl.pallas_call(
        functools.partial(_cd_sample_kernel, n_new=n_new, db=db, pos0=pos0),
        out_shape=[jax.ShapeDtypeStruct((t, w), F32), jax.ShapeDtypeStruct((t, w), F32),
                   shp(w), jax.ShapeDtypeStruct((CONV_C - 1, db, w), F32),
                   jax.ShapeDtypeStruct((POOL_HIST, db, w), F32)],
        scratch_shapes=[pltpu.VMEM((t, w), F32)],
        compiler_params=pltpu.CompilerParams(vmem_limit_bytes=V7X_VMEM_LIMIT_BYTES),
        name="rglru_pool_sample",
    )(zc, zd, h0, conv_hist, pool_hist, cw, cb, wg, bg, lam, wp, scale)


def _merge_kernel(x_ref, oa_ref, ob_ref, oc_ref, od_ref, g_ref, wg_ref, wb_ref, wo_ref, o_ref):
    x = x_ref[...]
    d = x.shape[1]
    h = _rms(x, g_ref[...]).astype(BF16)
    mix = jnp.zeros(x.shape, F32)
    for n, br in enumerate((oa_ref, ob_ref, oc_ref, od_ref)):
        gate = _sigmoid(_dot(h, wg_ref[:, n * d:(n + 1) * d]))
        mix = mix + gate * _dot(br[...].astype(BF16), wb_ref[n])
    o_ref[...] = x + _dot(mix.astype(BF16), wo_ref[...])


def _merge(x2, oa, ob, oc, od, g, wg, wb, wo):
    t, d = x2.shape
    tm = min(TM_PROJ, t)
    w = BRANCH_W
    row = lambda i: (i, 0)
    fix = lambda i: (0, 0)
    return pl.pallas_call(
        _merge_kernel,
        grid=(t // tm,),
        in_specs=[pl.BlockSpec((tm, d), row)] + [pl.BlockSpec((tm, w), row)] * 4
                 + [pl.BlockSpec((1, d), fix), pl.BlockSpec((d, N_BRANCH * d), fix),
                    pl.BlockSpec((N_BRANCH, w, d), lambda i: (0, 0, 0)), pl.BlockSpec((d, d), fix)],
        out_specs=pl.BlockSpec((tm, d), row),
        out_shape=jax.ShapeDtypeStruct((t, d), F32),
        compiler_params=_params("parallel"),
        name="merge",
    )(x2, oa, ob, oc, od, g, wg, wb, wo)


def _ple(x, pe, gp, wpg, wp):
    gate = _sigmoid(_dot(_rms(x, gp).astype(BF16), wpg))
    return x + gate * _dot(pe.astype(BF16), wp)


def _ffn_prompt_kernel(x_ref, xp_ref, g_ref, wa_ref, wb_ref, cwa_ref, cwb_ref, cba_ref, cbb_ref, wd_ref,
                       pe_ref, gp_ref, wpg_ref, wp_ref, o_ref, sa_ref, sb_ref,
                       xn_scr, xnp_scr, acc_scr, ea_scr, eb_scr, *, tiles_per_seq):
    i, f = pl.program_id(0), pl.program_id(1)
    tm = x_ref.shape[0]
    halo = FFN_HALO

    @pl.when(f == 0)
    def _():
        xn_scr[...] = _rms(x_ref[...], g_ref[...]).astype(BF16)
        xnp_scr[...] = _rms(xp_ref[...], g_ref[...]).astype(BF16)
        acc_scr[...] = jnp.zeros_like(acc_scr)

    seq_start = (i % tiles_per_seq) == 0

    ts = min(FFN_STRIP, tm)
    strips = [slice(r0, r0 + ts) for r0 in range(0, tm, ts)]

    def up_proj(w_ref, e_scr, s_ref):
        e_scr[0:halo, :] = jnp.where(seq_start, 0.0, _dot(xnp_scr[...], w_ref[...]))
        for rows in strips:
            e_scr[halo + rows.start:halo + rows.stop, :] = _dot(xn_scr[rows, :], w_ref[...])
        s_ref[0] = e_scr[pl.ds(halo + tm - (CONV_F - 1), CONV_F - 1), :]

    def conv(cw_ref, cb_ref, e_scr, rows):
        y = cb_ref[...]
        for j in range(CONV_F):
            y = y + cw_ref[j:j + 1, :] * e_scr[pl.ds(halo - (CONV_F - 1) + j + rows.start, ts), :]
        return y

    up_proj(wa_ref, ea_scr, sa_ref)
    up_proj(wb_ref, eb_scr, sb_ref)
    for rows in strips:
        gated = _gelu(conv(cwa_ref, cba_ref, ea_scr, rows)) * conv(cwb_ref, cbb_ref, eb_scr, rows)
        acc_scr[rows, :] += _dot(gated.astype(BF16), wd_ref[...])

    @pl.when(f == pl.num_programs(1) - 1)
    def _():
        o_ref[...] = _ple(x_ref[...] + acc_scr[...], pe_ref[...], gp_ref[...], wpg_ref[...], wp_ref[...])


def _ffn_prompt(layer, x2, seq_len, g, w_up, cw, cb, w_down, pe3, gp, wpg, wp):
    t, d = x2.shape
    dff = w_down.shape[0]
    tm = min(TM_FFN, seq_len)
    tf = TF_FFN
    halo = FFN_HALO
    assert seq_len % tm == 0 and dff % tf == 0 and tm % halo == 0
    nf = dff // tf
    tps = seq_len // tm
    dp = pe3.shape[2]
    fix = lambda i, f: (0, 0)
    st_shape = jax.ShapeDtypeStruct((t // tm, CONV_F - 1, dff), F32)
    st_spec = pl.BlockSpec((1, CONV_F - 1, tf), lambda i, f: (i, 0, f))
    return pl.pallas_call(
        functools.partial(_ffn_prompt_kernel, tiles_per_seq=tps),
        grid=(t // tm, nf),
        in_specs=[pl.BlockSpec((tm, d), lambda i, f: (i, 0)),
                  pl.BlockSpec((halo, d), lambda i, f: (jnp.maximum(i * (tm // halo) - 1, 0), 0)),
                  pl.BlockSpec((1, d), fix),
                  pl.BlockSpec((d, tf), lambda i, f: (0, f)),
                  pl.BlockSpec((d, tf), lambda i, f: (0, f + nf)),
                  pl.BlockSpec((CONV_F, tf), lambda i, f: (0, f)),
                  pl.BlockSpec((CONV_F, tf), lambda i, f: (0, f + nf)),
                  pl.BlockSpec((1, tf), lambda i, f: (0, f)),
                  pl.BlockSpec((1, tf), lambda i, f: (0, f + nf)),
                  pl.BlockSpec((tf, d), lambda i, f: (f, 0)),
                  pl.BlockSpec((None, tm, dp), lambda i, f: (layer, i, 0)),
                  pl.BlockSpec((1, d), fix), pl.BlockSpec((d, d), fix), pl.BlockSpec((dp, d), fix)],
        out_specs=[pl.BlockSpec((tm, d), lambda i, f: (i, 0)), st_spec, st_spec],
        out_shape=[jax.ShapeDtypeStruct((t, d), F32), st_shape, st_shape],
        scratch_shapes=[pltpu.VMEM((tm, d), BF16), pltpu.VMEM((halo, d), BF16), pltpu.VMEM((tm, d), F32),
                        pltpu.VMEM((tm + halo, tf), F32), pltpu.VMEM((tm + halo, tf), F32)],
        compiler_params=_params("arbitrary", "arbitrary"),
        name="ffn_prompt",
    )(x2, x2, g, w_up, w_up, cw, cw, cb, cb, w_down, pe3, gp, wpg, wp)


def _ffn_sample_kernel(x_ref, g_ref, wa_ref, wb_ref, cwa_ref, cwb_ref, cba_ref, cbb_ref, wd_ref,
                       ha0_ref, hb0_ref, ha1_ref, hb1_ref, pe_ref, gp_ref, wpg_ref, wp_ref,
                       o_ref, sa0_ref, sb0_ref, sa1_ref, sb1_ref, xn_scr, acc_scr, g_scr, *, n_new, db):
    f = pl.program_id(0)

    @pl.when(f == 0)
    def _():
        xn_scr[...] = _rms(x_ref[...], g_ref[...]).astype(BF16)
        acc_scr[...] = jnp.zeros_like(acc_scr)

    def conv_half(w_ref, cw_ref, cb_ref, h0_ref, h1_ref, s0_ref, s1_ref):
        u = _dot(xn_scr[...], w_ref[...])
        ext = [h0_ref[...], h1_ref[...]] + [u[t * db:(t + 1) * db] for t in range(n_new)]
        s0_ref[...] = ext[n_new]
        s1_ref[...] = ext[n_new + 1]
        ys = []
        for t in range(n_new):
            y = cb_ref[...] + cw_ref[0:1, :] * ext[t]
            for j in range(1, CONV_F):
                y = y + cw_ref[j:j + 1, :] * ext[t + j]
            ys.append(y)
        return ys

    ya = conv_half(wa_ref, cwa_ref, cba_ref, ha0_ref, ha1_ref, sa0_ref, sa1_ref)
    yb = conv_half(wb_ref, cwb_ref, cbb_ref, hb0_ref, hb1_ref, sb0_ref, sb1_ref)
    for t in range(n_new):
        g_scr[t * db:(t + 1) * db, :] = (_gelu(ya[t]) * yb[t]).astype(BF16)
    acc_scr[...] += _dot(g_scr[...], wd_ref[...])

    @pl.when(f == pl.num_programs(0) - 1)
    def _():
        o_ref[...] = _ple(x_ref[...] + acc_scr[...], pe_ref[...], gp_ref[...], wpg_ref[...], wp_ref[...])


def _ffn_sample(layer, x2, g, w_up, cw, cb, w_down, hist2, pe3, gp, wpg, wp, n_new):
    t, d = x2.shape
    db = t // n_new
    dff = w_down.shape[0]
    tf = TF_FFN
    nf = dff // tf
    dp = pe3.shape[2]
    fix = lambda f: (0, 0)
    hist = lambda blk: pl.BlockSpec((None, db, tf), lambda f, blk=blk: (layer, 0, f + blk * nf))
    st_shape = jax.ShapeDtypeStruct((db, dff), F32)
    st_spec = pl.BlockSpec((db, tf), lambda f: (0, f))
    return pl.pallas_call(
        functools.partial(_ffn_sample_kernel, n_new=n_new, db=db),
        grid=(nf,),
        in_specs=[pl.BlockSpec((t, d), fix), pl.BlockSpec((1, d), fix),
                  pl.BlockSpec((d, tf), lambda f: (0, f)), pl.BlockSpec((d, tf), lambda f: (0, f + nf)),
                  pl.BlockSpec((CONV_F, tf), lambda f: (0, f)), pl.BlockSpec((CONV_F, tf), lambda f: (0, f + nf)),
                  pl.BlockSpec((1, tf), lambda f: (0, f)), pl.BlockSpec((1, tf), lambda f: (0, f + nf)),
                  pl.BlockSpec((tf, d), lambda f: (f, 0)),
                  hist(0), hist(1), hist(2), hist(3),
                  pl.BlockSpec((None, t, dp), lambda f: (layer, 0, 0)),
                  pl.BlockSpec((1, d), fix), pl.BlockSpec((d, d), fix), pl.BlockSpec((dp, d), fix)],
        out_specs=[pl.BlockSpec((t, d), fix), st_spec, st_spec, st_spec, st_spec],
        out_shape=[jax.ShapeDtypeStruct((t, d), F32)] + [st_shape] * 4,
        scratch_shapes=[pltpu.VMEM((t, d), BF16), pltpu.VMEM((t, d), F32), pltpu.VMEM((t, tf), BF16)],
        compiler_params=_params("arbitrary"),
        name="ffn_sample",
    )(x2, g, w_up, w_up, cw, cw, cb, cb, w_down, hist2, hist2, hist2, hist2, pe3, gp, wpg, wp)


def _block_diag(w4):
    h, n, _ = w4.shape
    eye = jnp.eye(h, dtype=w4.dtype)
    return (eye[:, None, :, None] * w4[:, :, None, :]).reshape(h * n, h * n)


def kernel(x_prompt, x_sample, cache_k, cache_v, state_hgrn, state_rglru_h, state_rglru_conv, state_pool,
           state_ffn_conv, page_table, p_prompt, p_sample, norm_mix, w_in, q_norm, k_norm, sb_bias, lb_logits,
           hgrn_norm, conv_c_w, conv_c_b, w_rg_a, b_rg_a, w_rg_x, b_rg_x, lam, w_pool, pool_scale, w_branch,
           w_out, norm_ffn, w_up, conv_f_w, conv_f_b, w_down, norm_ple, w_ple_gate, w_ple):
    depth = w_in.shape[0]
    b, l, d = x_prompt.shape
    db, n_new, _ = x_sample.shape
    w = BRANCH_W
    n_mix = 10 * w
    dff = w_down.shape[1]
    dp = p_prompt.shape[-1]
    n_pool, page = cache_k.shape[1], cache_k.shape[2]
    pos0 = page_table.shape[1] * page

    lb_all, loglb_all, log1m_all = _lower_bounds(lb_logits.astype(F32))
    row = lambda a, i: a[i].reshape(1, -1)
    tile_heads = lambda a, i: jnp.tile(a[i], N_HEAD).reshape(1, w)

    xp = x_prompt.reshape(b * l, d)
    xs = x_sample.transpose(1, 0, 2).reshape(n_new * db, d)
    pe_p = p_prompt.reshape(depth, b * l, dp)
    pe_s = p_sample.transpose(0, 2, 1, 3).reshape(depth, n_new * db, dp)
    cache_kt = cache_k.transpose(0, 1, 3, 4, 2)
    cache_vt = cache_v.transpose(0, 1, 3, 4, 2)
    hgrn_state = state_hgrn.transpose(0, 2, 3, 4, 1).reshape(depth, -1, db)
    conv_state = state_rglru_conv.transpose(0, 2, 1, 3)
    pool_state = state_pool.transpose(0, 2, 1, 3)
    ffn_state = state_ffn_conv.reshape(depth, db, -1)

    outs = {k: [] for k in ("kp", "vp", "ks", "vs", "sp", "ss", "hp", "hs", "cp", "cs", "pp", "ps", "fp", "fs")}
    for i in range(depth):
        w_mix = w_in[i, :, :n_mix].astype(BF16)
        w_gate = w_in[i, :, n_mix:].astype(BF16)
        wg_c = jnp.concatenate([_block_diag(w_rg_a[i]), _block_diag(w_rg_x[i])], axis=1).astype(BF16)
        bg_c = jnp.concatenate([b_rg_a[i], b_rg_x[i]]).reshape(1, 2 * w)
        wp_d = _block_diag(w_pool[i]).astype(BF16)
        g_mix, g_ffn, g_ple = row(norm_mix, i), row(norm_ffn, i), row(norm_ple, i)
        qn, kn, gn = tile_heads(q_norm, i), tile_heads(k_norm, i), tile_heads(hgrn_norm, i)
        lb, loglb, log1m = row(lb_all, i), row(loglb_all, i), row(log1m_all, i)
        cw_c, cb_c, lam_i, sc_d = conv_c_w[i], row(conv_c_b, i), row(lam, i), row(pool_scale, i)
        w_br, w_o = w_branch[i].astype(BF16), w_out[i].astype(BF16)
        w_u, w_d = w_up[i].astype(BF16), w_down[i].astype(BF16)
        cw_f, cb_f = conv_f_w[i], row(conv_f_b, i)
        w_pg, w_pe = w_ple_gate[i].astype(BF16), w_ple[i].astype(BF16)
        bias = sb_bias[i].astype(F32)

        q, kt, vt, ktb, vtb, zb, zc, zd = _inproj_prompt(xp.reshape(b, l, d), g_mix, w_mix, qn, kn)
        o_a = _attn_prompt(bias, q, ktb, vtb)
        o_b, s_p = _hgrn_prompt(zb, lb, loglb, log1m, gn)
        o_c, h_p = _rglru_prompt(zc, cw_c, cb_c, wg_c, bg_c, lam_i)
        o_d = _pool_prompt(zd, wp_d, sc_d)
        f2 = lambda a: a.reshape(b * l, w)
        x1 = _merge(xp, f2(o_a), f2(o_b), f2(o_c), f2(o_d), g_mix, w_gate, w_br, w_o)
        xp, fa, fb = _ffn_prompt(i, x1, l, g_ffn, w_u, cw_f, cb_f, w_d, pe_p, g_ple, w_pg, w_pe)
        outs["kp"].append(kt)
        outs["vp"].append(vt)
        s_heads = s_p.reshape(b, N_HEAD, HEAD_W, N_HEAD, HEAD_W)
        outs["sp"].append(jnp.stack([s_heads[:, h, :, h, :] for h in range(N_HEAD)], axis=1))
        outs["hp"].append(h_p.reshape(b, w))
        outs["cp"].append(zc[:, l - (CONV_C - 1):, :w])
        outs["pp"].append(zd[:, l - POOL_HIST:, :])
        tps = fa.shape[0] // b
        outs["fp"].append(jnp.concatenate([fa[tps - 1::tps], fb[tps - 1::tps]], axis=-1))

        q, k, v, zb, zc, zd = _inproj_sample(xs, g_mix, w_mix, qn, kn)
        r4 = lambda a: a.reshape(n_new, db, 1, w)
        o_a = _attn_sample(i, page_table, bias, r4(q), r4(k), r4(v), cache_kt, cache_vt).reshape(n_new * db, w)
        o_b, s_s = _hgrn_sample(i, zb, hgrn_state, lb, gn, n_new)
        o_c, o_d, h_s, c_s, p_s = _cd_sample(zc, zd, state_rglru_h[i], conv_state[i], pool_state[i],
                                             cw_c, cb_c, wg_c, bg_c, lam_i, wp_d, sc_d, n_new, pos0)
        x1 = _merge(xs, o_a, o_b, o_c, o_d, g_mix, w_gate, w_br, w_o)
        xs, fa0, fb0, fa1, fb1 = _ffn_sample(i, x1, g_ffn, w_u, cw_f, cb_f, w_d, ffn_state, pe_s,
                                             g_ple, w_pg, w_pe, n_new)
        outs["ks"].append(k)
        outs["vs"].append(v)
        outs["ss"].append(s_s)
        outs["hs"].append(h_s)
        outs["cs"].append(c_s)
        outs["ps"].append(p_s)
        outs["fs"].append(jnp.stack([jnp.concatenate([fa0, fb0], axis=-1),
                                     jnp.concatenate([fa1, fb1], axis=-1)], axis=1))

    stk = lambda key: jnp.stack(outs[key], axis=0)
    to_batch_major = lambda a: a.reshape(depth, n_new, db, N_HEAD, HEAD_W).transpose(0, 2, 1, 3, 4)
    y_sample = xs.reshape(n_new, db, d).transpose(1, 0, 2)
    from_transposed = lambda a: a.reshape(depth, b, N_HEAD, HEAD_W, l).transpose(0, 1, 4, 2, 3)
    hgrn_s = stk("ss").reshape(depth, N_HEAD, HEAD_W, HEAD_W, db).transpose(0, 4, 1, 2, 3)
    rows_to_batch = lambda a: a.transpose(0, 2, 1, 3)
    return (xp.reshape(b, l, d), y_sample,
            from_transposed(stk("kp")), from_transposed(stk("vp")),
            to_batch_major(stk("ks")), to_batch_major(stk("vs")),
            stk("sp"), hgrn_s, stk("hp"), stk("hs"), stk("cp"), rows_to_batch(stk("cs")),
            stk("pp"), rows_to_batch(stk("ps")), stk("fp"), stk("fs"))
```

```python
import functools
import math

import jax
import jax.numpy as jnp
from jax import lax
from jax.experimental import pallas as pl
from jax.experimental.pallas import tpu as pltpu

F32 = jnp.float32
BF16 = jnp.bfloat16
EPS = 1e-6

N_HEAD = 4
HEAD_W = 64
BRANCH_W = N_HEAD * HEAD_W
N_BRANCH = 4
RG_C = 8.0
POOL_WINDOWS = (2, 4, 8, 16)
POOL_HIST = max(POOL_WINDOWS) - 1
CONV_C = 4
CONV_F = 3
HGRN_CHUNK = 64
HGRN_SUB = 16
NEG_BIG = -1e30
LOG2E = math.log2(math.e)

V7X_VMEM_LIMIT_BYTES = 56 * 1024 * 1024

TM_PROJ = 512
TQ_ATTN = 256
TK_ATTN = 256
ATTN_HEAD_GROUP = 4
TL_SCAN = 512
TL_HGRN = 256
TM_FFN = 1024
TF_FFN = 256
FFN_HALO = 16
FFN_STRIP = 256
SAMPLES_PER_STEP = 2


def _params(*sem):
    return pltpu.CompilerParams(dimension_semantics=sem, vmem_limit_bytes=V7X_VMEM_LIMIT_BYTES)


def _dot(a, b):
    return jnp.dot(a, b, preferred_element_type=F32)


def _dot_nt(a, b):
    return lax.dot_general(a, b, (((1,), (1,)), ((), ())), preferred_element_type=F32)


def _dot_tn(a, b):
    return lax.dot_general(a, b, (((0,), (0,)), ((), ())), preferred_element_type=F32)


def _split2(x):
    hi = x.astype(BF16)
    lo = (x - hi.astype(F32)).astype(BF16)
    return hi, lo


def _split3(x):
    hi = x.astype(BF16)
    r = x - hi.astype(F32)
    mid = r.astype(BF16)
    lo = (r - mid.astype(F32)).astype(BF16)
    return hi, mid, lo


def _dot_x2(x, m):
    hi, lo = _split2(x)
    return _dot(hi, m) + _dot(lo, m)


def _sigmoid(x):
    return 1.0 / (1.0 + jnp.exp(-x))


def _softplus_tail(x):
    return jnp.log1p(jnp.exp(-jnp.abs(x)))


def _log_sigmoid(x):
    return jnp.minimum(x, 0.0) - _softplus_tail(x)


def _softplus(x):
    return jnp.maximum(x, 0.0) + _softplus_tail(x)


def _silu(x):
    return x * _sigmoid(x)


def _gelu(x):
    c = math.sqrt(2.0 / math.pi)
    return 0.5 * x * (1.0 + jnp.tanh(c * (x + 0.044715 * (x * x * x))))


def _rms(x, g):
    ms = jnp.mean(x * x, axis=-1, keepdims=True)
    return x * lax.rsqrt(ms + EPS) * g


def _head_id(shape, dim):
    return lax.shift_right_logical(lax.broadcasted_iota(jnp.int32, shape, dim), 6)


def _head_block_ones():
    n = BRANCH_W
    return jnp.where(_head_id((n, n), 0) == _head_id((n, n), 1), 1.0, 0.0).astype(BF16)


def _head_rms(a, g, bd):
    ms = _dot_x2(a * a, bd) * (1.0 / HEAD_W)
    return a * lax.rsqrt(ms + EPS) * g


def _shift_rows(x, d, fill):
    rolled = pltpu.roll(x, d, 0)
    row = lax.broadcasted_iota(jnp.int32, x.shape, 0)
    return jnp.where(row < d, fill, rolled)


def _lb_kernel(lg_ref, lb_ref, loglb_ref, log1m_ref):
    x = lg_ref[...]
    depth = x.shape[0]
    rows = [x[i:i + 1] for i in range(depth)]
    m = functools.reduce(jnp.maximum, rows)
    e = [jnp.exp(r - m) for r in rows]
    tot = functools.reduce(lambda a, b: a + b, e)
    zero = jnp.zeros_like(m)
    lb_ref[0:1, :] = zero
    loglb_ref[0:1, :] = jnp.full_like(m, -jnp.inf)
    log1m_ref[0:1, :] = zero
    acc = zero
    for i in range(1, depth):
        acc = acc + e[i] / tot
        lb_ref[i:i + 1, :] = acc
        loglb_ref[i:i + 1, :] = jnp.log(acc)
        log1m_ref[i:i + 1, :] = jnp.log1p(-acc)


def _lower_bounds(lb_logits):
    shp = jax.ShapeDtypeStruct(lb_logits.shape, F32)
    return pl.pallas_call(_lb_kernel, out_shape=(shp, shp, shp), name="hgrn_lower_bounds")(lb_logits)


def _inproj_common(x, g_ref, w_ref, qn_ref, kn_ref):
    h = _rms(x, g_ref[...]).astype(BF16)
    z = _dot(h, w_ref[...])
    bd = _head_block_ones()
    w = BRANCH_W
    q = _head_rms(z[:, 0:w], qn_ref[...], bd) * (HEAD_W ** -0.5 * LOG2E)
    k = _head_rms(z[:, w:2 * w], kn_ref[...], bd)
    return q, k, z[:, 2 * w:3 * w], z[:, 3 * w:7 * w], z[:, 7 * w:9 * w], z[:, 9 * w:10 * w]


def _inproj_prompt_kernel(x_ref, g_ref, w_ref, qn_ref, kn_ref,
                          q_o, kt_o, vt_o, ktb_o, vtb_o, zb_o, zc_o, zd_o):
    q, k, v, zb, zc, zd = _inproj_common(x_ref[0], g_ref, w_ref, qn_ref, kn_ref)
    q_o[0] = q.astype(BF16)
    kt, vt = k.T, v.T
    kt_o[0] = kt
    vt_o[0] = vt
    tk = ktb_o.shape[2]
    for c in range(ktb_o.shape[0]):
        ktb_o[c] = kt[:, c * tk:(c + 1) * tk].astype(BF16)
        vtb_o[c] = vt[:, c * tk:(c + 1) * tk].astype(BF16)
    zb_o[0] = zb
    zc_o[0] = zc
    zd_o[0] = zd


def _inproj_prompt(x3, g, w, qn, kn):
    b, l, d = x3.shape
    tm = min(TM_PROJ, l)
    tk = min(TK_ATTN, l)
    n = w.shape[1]
    bw = BRANCH_W
    per = tm // tk
    row = lambda i, j: (i, j, 0)
    fix = lambda i, j: (0, 0)
    col = lambda i, j: (i, 0, j)
    tiles = lambda i, j: (i * (l // tm) + j, 0, 0)
    tok = lambda c, dt: (jax.ShapeDtypeStruct((b, l, c), dt), pl.BlockSpec((1, tm, c), row))
    tr = (jax.ShapeDtypeStruct((b, bw, l), F32), pl.BlockSpec((1, bw, tm), col))
    trb = (jax.ShapeDtypeStruct((b * l // tk, bw, tk), BF16), pl.BlockSpec((per, bw, tk), tiles))
    outs = [tok(bw, BF16), tr, tr, trb, trb, tok(4 * bw, F32), tok(2 * bw, F32), tok(bw, F32)]
    return pl.pallas_call(
        _inproj_prompt_kernel,
        grid=(b, l // tm),
        in_specs=[pl.BlockSpec((1, tm, d), row), pl.BlockSpec((1, d), fix), pl.BlockSpec((d, n), fix),
                  pl.BlockSpec((1, bw), fix), pl.BlockSpec((1, bw), fix)],
        out_specs=[s for _, s in outs],
        out_shape=[s for s, _ in outs],
        compiler_params=_params("parallel", "parallel"),
        name="inproj_prompt",
    )(x3, g, w, qn, kn)


def _inproj_sample_kernel(x_ref, g_ref, w_ref, qn_ref, kn_ref, q_o, k_o, v_o, zb_o, zc_o, zd_o):
    for ref, val in zip((q_o, k_o, v_o, zb_o, zc_o, zd_o),
                        _inproj_common(x_ref[...], g_ref, w_ref, qn_ref, kn_ref)):
        ref[...] = val


def _inproj_sample(x2, g, w, qn, kn):
    t, d = x2.shape
    bw = BRANCH_W
    widths = (bw, bw, bw, 4 * bw, 2 * bw, bw)
    return pl.pallas_call(
        _inproj_sample_kernel,
        out_shape=[jax.ShapeDtypeStruct((t, c), F32) for c in widths],
        compiler_params=pltpu.CompilerParams(vmem_limit_bytes=V7X_VMEM_LIMIT_BYTES),
        name="inproj_sample",
    )(x2, g, w, qn, kn)


def _upper_ones(n):
    r = lax.broadcasted_iota(jnp.int32, (n, n), 0)
    c = lax.broadcasted_iota(jnp.int32, (n, n), 1)
    return jnp.where(r > c, 1.0, 0.0).astype(BF16)


def _sb_logs(z, mask):
    m = jnp.maximum(z, 0.0)
    n = z - m
    t = jnp.log2(1.0 + jnp.exp2(n - m))
    drop = m + t
    return n - t, (drop if mask is None else jnp.where(mask, drop, 0.0))


def _sb_finish(log_beta, drop, local, carry, mask):
    w = jnp.exp2(log_beta - (local + carry))
    if mask is not None:
        w = jnp.where(mask, w, 0.0)
    return w.astype(BF16), carry + local[:, 0:1] + drop[:, 0:1]


def _attn_prompt_kernel(bias_ref, q_ref, kt_ref, vt_ref, o_ref):
    qi = pl.program_id(1)
    tq, tk = q_ref.shape[1], kt_ref.shape[2]
    per = tq // tk
    q = q_ref[0]
    heads = range(N_HEAD)
    lanes = lambda h: slice(h * HEAD_W, (h + 1) * HEAD_W)
    qs = [q[:, lanes(h)] for h in heads]
    bias = [bias_ref[h] * LOG2E for h in heads]
    upper = _upper_ones(tk)
    q_pos = qi * tq + lax.broadcasted_iota(jnp.int32, (tq, tk), 0)
    k_off = lax.broadcasted_iota(jnp.int32, (tq, tk), 1)

    def tile(kidx, carries, masked):
        mask = (kidx * tk + k_off < q_pos) if masked else None
        pvs, new = [None] * N_HEAD, [None] * N_HEAD
        for g0 in range(0, N_HEAD, ATTN_HEAD_GROUP):
            group = range(g0, g0 + ATTN_HEAD_GROUP)
            logs = {h: _sb_logs(_dot(qs[h], kt_ref[kidx, lanes(h), :]) + bias[h], mask) for h in group}
            local = {h: _dot(logs[h][1].astype(BF16), upper) for h in group}
            fin = {h: _sb_finish(logs[h][0], logs[h][1], local[h], carries[h], mask) for h in group}
            for h in group:
                pvs[h] = _dot_nt(fin[h][0], vt_ref[kidx, lanes(h), :])
                new[h] = fin[h][1]
        return pvs, new

    add = lambda accs, pvs: [a + p for a, p in zip(accs, pvs)]
    first = qi * per
    accs, carries = tile(first + per - 1, [jnp.zeros((tq, 1), F32)] * N_HEAD, True)
    for m in range(per - 2, -1, -1):
        pvs, carries = tile(first + m, carries, True)
        accs = add(accs, pvs)

    def body(j, state):
        accs, carries = state
        pvs, carries = tile(first - 1 - j, carries, False)
        return add(accs, pvs), carries

    accs, _ = lax.fori_loop(0, first, body, (accs, carries))
    for h in heads:
        o_ref[0, :, lanes(h)] = accs[h]


def _attn_prompt(bias, q3, ktb, vtb):
    b, l, w = q3.shape
    tk = ktb.shape[2]
    tq = min(TQ_ATTN, l)
    nk = l // tk
    return pl.pallas_call(
        _attn_prompt_kernel,
        grid=(b, l // tq),
        in_specs=[pl.BlockSpec(memory_space=pltpu.SMEM),
                  pl.BlockSpec((1, tq, w), lambda i, j: (i, j, 0)),
                  pl.BlockSpec((nk, w, tk), lambda i, j: (i, 0, 0)),
                  pl.BlockSpec((nk, w, tk), lambda i, j: (i, 0, 0))],
        out_specs=pl.BlockSpec((1, tq, w), lambda i, j: (i, j, 0)),
        out_shape=jax.ShapeDtypeStruct((b, l, w), F32),
        compiler_params=_params("parallel", "arbitrary"),
        name="attn_prompt",
    )(bias, q3, ktb, vtb)


def _attn_sample_kernel(pt_ref, bias_ref, q_ref, kn_ref, vn_ref, *refs, n_pages, n_new, n_samp):
    del pt_ref
    o_ref = refs[-1]
    samples = range(n_samp)
    k_refs = [refs[s * n_pages:(s + 1) * n_pages] for s in samples]
    v_refs = [refs[(n_samp + s) * n_pages:(n_samp + s + 1) * n_pages] for s in samples]
    page = refs[0].shape[-1]
    flat = lambda ref: ref[...].reshape(BRANCH_W, page).astype(BF16)
    rows = N_HEAD * 8
    row = lax.broadcasted_iota(jnp.int32, (rows, 1), 0)
    row_t = jnp.bitwise_and(row, 7)
    row_h = lax.shift_right_logical(row, 3)
    lane_head = _head_id((1, BRANCH_W), 1)
    bias = jnp.zeros((rows, 1), F32)
    for h in range(N_HEAD):
        bias = jnp.where(row_h == h, bias_ref[h] * LOG2E, bias)

    qs, carry, out = [], [], []
    for s in samples:
        q = jnp.zeros((rows, BRANCH_W), F32)
        for t in range(n_new):
            q = q + jnp.where(row_t == t, q_ref[t, s], 0.0)
        q = jnp.where(row_h == lane_head, q, 0.0)
        c = jnp.zeros((rows, 1), F32)
        o = jnp.zeros((rows, BRANCH_W), F32)
        for j in reversed(range(n_new)):
            seen = row_t > j
            log_beta, drop = _sb_logs(jnp.sum(q * kn_ref[j, s], axis=-1, keepdims=True) + bias, seen)
            o = o + jnp.where(seen, jnp.exp2(log_beta - c), 0.0) * vn_ref[j, s]
            c = c + drop
        qs.append(q)
        carry.append(c)
        out.append(o)

    cols = lambda a, p: a[:, p * page:(p + 1) * page]
    kt = [jnp.concatenate([flat(r) for r in k_refs[s]], axis=1) for s in samples]
    logs = [_sb_logs(_dot(qs[s].astype(BF16), kt[s]) + bias, None) for s in samples]
    stacked = [jnp.concatenate([cols(drop, p) for p in range(n_pages)], axis=0).astype(BF16) for _, drop in logs]
    upper = _upper_ones(page)
    local = [_dot(st, upper) for st in stacked]
    ws = []
    for s in samples:
        later, c = [None] * n_pages, carry[s]
        for p in reversed(range(n_pages)):
            loc = local[s][p * rows:(p + 1) * rows]
            later[p] = loc + c
            c = c + loc[:, 0:1] + cols(logs[s][1], p)[:, 0:1]
        ws.append(jnp.exp2(logs[s][0] - jnp.concatenate(later, axis=1)).astype(BF16))
    vt = [jnp.concatenate([flat(r) for r in v_refs[s]], axis=1) for s in samples]
    out = [out[s] + _dot_nt(ws[s], vt[s]) for s in samples]

    for s in samples:
        res = jnp.zeros((8, BRANCH_W), F32)
        for h in range(N_HEAD):
            res = res + jnp.where(lane_head == h, out[s][h * 8:(h + 1) * 8], 0.0)
        for t in range(n_new):
            o_ref[t, s] = res[t:t + 1]


def _attn_sample(layer, page_table, bias, q4, kn4, vn4, cache_kt, cache_vt):
    n_new, db, _, w = q4.shape
    n_pages = page_table.shape[1]
    page = cache_kt.shape[4]
    n_samp = SAMPLES_PER_STEP
    assert db % n_samp == 0
    new_spec = pl.BlockSpec((n_new, n_samp, 1, w), lambda g, pt, bs: (0, g, 0, 0))

    def page_spec(s, p):
        return pl.BlockSpec((None, None, N_HEAD, HEAD_W, page),
                            lambda g, pt, bs: (layer, pt[g * n_samp + s, p], 0, 0, 0))

    page_specs = [page_spec(s, p) for s in range(n_samp) for p in range(n_pages)]
    grid_spec = pltpu.PrefetchScalarGridSpec(
        num_scalar_prefetch=2,
        grid=(db // n_samp,),
        in_specs=[new_spec, new_spec, new_spec] + page_specs * 2,
        out_specs=new_spec,
    )
    n_ops = n_samp * n_pages
    return pl.pallas_call(
        functools.partial(_attn_sample_kernel, n_pages=n_pages, n_new=n_new, n_samp=n_samp),
        grid_spec=grid_spec,
        out_shape=jax.ShapeDtypeStruct(q4.shape, F32),
        compiler_params=_params("arbitrary"),
        name="attn_sample",
    )(page_table, bias, q4, kn4, vn4, *([cache_kt] * n_ops), *([cache_vt] * n_ops))


def _hgrn_gates(qb, fr, lb, loglb, log1m):
    c = log1m + _log_sigmoid(fr)
    log_f = jnp.maximum(loglb, c) + jnp.log1p(jnp.exp(-jnp.abs(loglb - c)))
    key = (1.0 - lb) * _sigmoid(-fr)
    return _silu(qb), key, log_f


def _hgrn_prompt_kernel(z_ref, lb_ref, loglb_ref, log1m_ref, gn_ref, o_ref, s_ref,
                        st_scr, q_scr, k_scr, v_scr, b_scr, p_scr, vx_scr, od_scr):
    ti = pl.program_id(0)
    w = BRANCH_W
    ck, sub = HGRN_CHUNK, HGRN_SUB
    n_sub = ck // sub
    nb, tl = z_ref.shape[0], z_ref.shape[1]
    rows = range(nb)

    @pl.when(ti == 0)
    def _():
        st_scr[...] = jnp.zeros_like(st_scr)

    bd = _head_block_ones()
    bd_mask = _head_id((w, w), 0) == _head_id((w, w), 1)
    r = lax.broadcasted_iota(jnp.int32, (ck, ck), 0)
    c = lax.broadcasted_iota(jnp.int32, (ck, ck), 1)
    lower_incl = jnp.where(c <= r, 1.0, 0.0).astype(BF16)
    row_ck = lax.broadcasted_iota(jnp.int32, (ck, 1), 0)
    row_sub = lax.broadcasted_iota(jnp.int32, (sub, 1), 0)
    stack_mask = (lax.shift_right_logical(lax.broadcasted_iota(jnp.int32, (N_HEAD * sub, w), 0), 4)
                  == _head_id((N_HEAD * sub, w), 1))
    lb, loglb, log1m, gn = lb_ref[...], loglb_ref[...], log1m_ref[...], gn_ref[...]
    stack = lambda a: jnp.where(stack_mask, jnp.concatenate([a] * N_HEAD, axis=0), 0.0).astype(BF16)

    def chunk(ci, _):
        r0 = pl.multiple_of(ci * ck, ck)
        zz = [z_ref[n, pl.ds(r0, ck), :] for n in rows]
        gates = [_hgrn_gates(z[:, 0:w], z[:, w:2 * w], lb, loglb, log1m) for z in zz]
        qh, key = [g[0] for g in gates], [g[1] for g in gates]
        val = [z[:, 2 * w:3 * w] for z in zz]
        parts = [_split3(g[2]) for g in gates]
        b = [_dot(lower_incl, hi) + _dot(lower_incl, mid) + _dot(lower_incl, lo)
             for hi, mid, lo in parts]
        for n in rows:
            q_scr[n] = qh[n]
            k_scr[n] = key[n]
            v_scr[n] = val[n]
            b_scr[n] = b[n]

        st = [st_scr[n] for n in rows]
        out = [_dot_nt((qh[n] * jnp.exp(b[n])).astype(BF16), st[n].astype(BF16)) for n in rows]
        for j in range(n_sub - 1):
            blk = slice(j * sub, (j + 1) * sub)
            e_j = [x[(j + 1) * sub - 1:(j + 1) * sub] for x in b]
            qj = [(qh[n] * jnp.exp(jnp.where(row_ck >= (j + 1) * sub, b[n] - e_j[n], NEG_BIG))).astype(BF16)
                  for n in rows]
            k_st = [stack(key[n][blk] * jnp.exp(e_j[n] - b[n][blk])) for n in rows]
            att = [_dot_nt(qj[n], k_st[n]).astype(BF16) for n in rows]
            out = [out[n] + _dot(att[n], stack(val[n][blk])) for n in rows]

        b_last = [x[ck - 1:ck] for x in b]
        upd = [_dot_tn(val[n].astype(BF16), (key[n] * jnp.exp(b_last[n] - b[n])).astype(BF16)) for n in rows]
        for n in rows:
            st_scr[n] = jnp.where(bd_mask, st[n] * jnp.exp(b_last[n]) + upd[n], 0.0)

        def diag(n, _):
            for si in range(n_sub):
                s0 = si * sub
                q_i = q_scr[n, s0:s0 + sub, :]
                b_i = b_scr[n, s0:s0 + sub, :]
                for s in range(sub):
                    k_s = k_scr[n, s0 + s:s0 + s + 1, :]
                    b_s = b_scr[n, s0 + s:s0 + s + 1, :]
                    v_s = v_scr[n, s0 + s:s0 + s + 1, :]
                    dst = slice((s0 + s) * sub, (s0 + s + 1) * sub)
                    p_scr[dst, :] = (q_i * k_s * jnp.exp(jnp.where(row_sub >= s, b_i - b_s, NEG_BIG))).astype(BF16)
                    vx_scr[dst, :] = jnp.broadcast_to(v_s, (sub, w))
            att = _dot(p_scr[...], bd)
            od_scr[n] = jnp.sum((att * vx_scr[...]).reshape(n_sub, sub, sub, w), axis=1).reshape(ck, w)
            return 0

        lax.fori_loop(0, nb, diag, 0)
        for n in rows:
            o_ref[n, pl.ds(r0, ck), :] = (_head_rms(out[n] + od_scr[n], gn, bd) * _silu(zz[n][:, 3 * w:4 * w]))
        return 0

    lax.fori_loop(0, tl // ck, chunk, 0)

    @pl.when(ti == pl.num_programs(0) - 1)
    def _():
        for n in rows:
            s_ref[n] = st_scr[n].T


def _hgrn_prompt(zb3, lb, loglb, log1m, gn):
    b, l, _ = zb3.shape
    w = BRANCH_W
    ck = HGRN_CHUNK
    tl = min(TL_HGRN, l)
    assert l % tl == 0 and tl % ck == 0
    fix = lambda j: (0, 0)
    scr = lambda rows, dt=F32: pltpu.VMEM((b, rows, w), dt)
    pairs = ck * HGRN_SUB
    return pl.pallas_call(
        _hgrn_prompt_kernel,
        grid=(l // tl,),
        in_specs=[pl.BlockSpec((b, tl, 4 * w), lambda j: (0, j, 0))] + [pl.BlockSpec((1, w), fix)] * 4,
        out_specs=[pl.BlockSpec((b, tl, w), lambda j: (0, j, 0)),
                   pl.BlockSpec((b, w, w), lambda j: (0, 0, 0))],
        out_shape=[jax.ShapeDtypeStruct((b, l, w), F32), jax.ShapeDtypeStruct((b, w, w), F32)],
        scratch_shapes=[scr(w), scr(ck), scr(ck), scr(ck), scr(ck),
                        pltpu.VMEM((pairs, w), BF16), pltpu.VMEM((pairs, w), F32), scr(ck)],
        compiler_params=_params("arbitrary"),
        name="hgrn_prompt",
    )(zb3, lb, loglb, log1m, gn)


def _hgrn_sample_kernel(z_ref, s0_ref, lb_ref, gn_ref, o_ref, s_ref, f_scr, k_scr, q_scr, v_scr, o_scr,
                        *, n_new, db):
    j = pl.program_id(0)
    w = BRANCH_W
    k_per_step = s0_ref.shape[0] // HEAD_W
    steps_per_head = HEAD_W // k_per_step
    lb = lb_ref[...]

    @pl.when(j == 0)
    def _():
        for t in range(n_new):
            zz = z_ref[t * db:(t + 1) * db, :]
            fr = zz[:, w:2 * w]
            f_scr[t] = (lb + (1.0 - lb) * _sigmoid(fr)).T
            k_scr[t] = ((1.0 - lb) * _sigmoid(-fr)).T
            q_scr[t] = _silu(zz[:, 0:w]).T
            v_scr[t] = zz[:, 2 * w:3 * w].T
            o_scr[t] = jnp.zeros((w, db), F32)

    head = j // steps_per_head
    k_base = head * HEAD_W + (j % steps_per_head) * k_per_step
    v0 = pl.multiple_of(head * HEAD_W, HEAD_W)
    for kk in range(k_per_step):
        rows = slice(kk * HEAD_W, (kk + 1) * HEAD_W)
        s_k = s0_ref[rows, :]
        row = k_base + kk
        for t in range(n_new):
            s_k = (f_scr[t, pl.ds(row, 1), :] * s_k
                   + k_scr[t, pl.ds(row, 1), :] * v_scr[t, pl.ds(v0, HEAD_W), :])
            o_scr[t, pl.ds(v0, HEAD_W), :] += q_scr[t, pl.ds(row, 1), :] * s_k
        s_ref[rows, :] = s_k

    @pl.when(j == pl.num_programs(0) - 1)
    def _():
        bd = _head_block_ones()
        for t in range(n_new):
            gate = _silu(z_ref[t * db:(t + 1) * db, 3 * w:4 * w])
            o_ref[t * db:(t + 1) * db, :] = _head_rms(o_scr[t].T, gn_ref[...], bd) * gate


def _hgrn_sample(layer, zb, state_t, lb, gn, n_new):
    t, _ = zb.shape
    db = t // n_new
    w = BRANCH_W
    n_state = state_t.shape[1]
    rows = 16 * HEAD_W
    fix = lambda j: (0, 0)
    scr = pltpu.VMEM((n_new, w, db), F32)
    return pl.pallas_call(
        functools.partial(_hgrn_sample_kernel, n_new=n_new, db=db),
        grid=(n_state // rows,),
        in_specs=[pl.BlockSpec((t, 4 * w), fix),
                  pl.BlockSpec((None, rows, db), lambda j: (layer, j, 0)),
                  pl.BlockSpec((1, w), fix), pl.BlockSpec((1, w), fix)],
        out_specs=[pl.BlockSpec((t, w), fix), pl.BlockSpec((rows, db), lambda j: (j, 0))],
        out_shape=[jax.ShapeDtypeStruct((t, w), F32), jax.ShapeDtypeStruct((n_state, db), F32)],
        scratch_shapes=[scr] * 5,
        compiler_params=_params("arbitrary"),
        name="hgrn_sample",
    )(zb, state_t, lb, gn)


def _rglru_gates(xconv, wg, bg, sp_lam):
    w = BRANCH_W
    g = _sigmoid(_dot(xconv.astype(BF16), wg) + bg)
    log_a = -RG_C * g[:, 0:w] * sp_lam
    a = jnp.exp(log_a)
    one_minus_a2 = -jnp.tanh(log_a) * (a * a + 1.0)
    u = jnp.sqrt(one_minus_a2) * (g[:, w:2 * w] * xconv)
    return a, u


def _rglru_prompt_kernel(z_ref, cw_ref, cb_ref, wg_ref, bg_ref, lam_ref, o_ref, h_ref, ext_scr, h_scr):
    ti = pl.program_id(1)
    w = BRANCH_W
    tl = z_ref.shape[1]
    halo = 8

    @pl.when(ti == 0)
    def _():
        ext_scr[0:halo, :] = jnp.zeros((halo, w), F32)
        h_scr[...] = jnp.zeros_like(h_scr)

    x = z_ref[0, :, 0:w]
    ext_scr[halo:, :] = x
    xconv = cb_ref[...] + cw_ref[CONV_C - 1:CONV_C, :] * x
    for j in range(CONV_C - 1):
        xconv = xconv + cw_ref[j:j + 1, :] * ext_scr[pl.ds(halo - (CONV_C - 1) + j, tl), :]
    ext_scr[0:halo, :] = x[tl - halo:tl]
    a, u = _rglru_gates(xconv, wg_ref[...], bg_ref[...], _softplus(-lam_ref[...]))
    d = 1
    while d < tl:
        u = u + a * _shift_rows(u, d, 0.0)
        a = a * _shift_rows(a, d, 1.0)
        d *= 2
    h = u + a * h_scr[0:1, :]
    h_last = h[tl - 1:tl]
    h_scr[...] = jnp.broadcast_to(h_last, h_scr.shape)
    h_ref[0] = h_last
    o_ref[0] = h * _gelu(z_ref[0, :, w:2 * w])


def _rglru_prompt(zc3, cw, cb, wg, bg, lam):
    b, l, _ = zc3.shape
    w = BRANCH_W
    tl = min(TL_SCAN, l)
    fix = lambda i, j: (0, 0)
    return pl.pallas_call(
        _rglru_prompt_kernel,
        grid=(b, l // tl),
        in_specs=[pl.BlockSpec((1, tl, 2 * w), lambda i, j: (i, j, 0)),
                  pl.BlockSpec((CONV_C, w), fix), pl.BlockSpec((1, w), fix),
                  pl.BlockSpec((w, 2 * w), fix), pl.BlockSpec((1, 2 * w), fix), pl.BlockSpec((1, w), fix)],
        out_specs=[pl.BlockSpec((1, tl, w), lambda i, j: (i, j, 0)),
                   pl.BlockSpec((1, 1, w), lambda i, j: (i, 0, 0))],
        out_shape=[jax.ShapeDtypeStruct((b, l, w), F32), jax.ShapeDtypeStruct((b, 1, w), F32)],
        scratch_shapes=[pltpu.VMEM((tl + 8, w), F32), pltpu.VMEM((8, w), F32)],
        compiler_params=_params("parallel", "arbitrary"),
        name="rglru_prompt",
    )(zc3, cw, cb, wg, bg, lam)


def _pool_select(sums, x, pos, wp, scale):
    lane_group = _head_id((1, BRANCH_W), 1)
    pooled = jnp.zeros_like(x)
    for g, win in enumerate(POOL_WINDOWS):
        cnt = jnp.minimum(pos + 1, win).astype(F32)
        pooled = jnp.where(lane_group == g, sums[g] / cnt, pooled)
    return _dot((pooled - x).astype(BF16), wp) * scale


def _pool_prompt_kernel(x_ref, wp_ref, sc_ref, o_ref, ext_scr):
    ti = pl.program_id(1)
    w = BRANCH_W
    tl = x_ref.shape[1]
    halo = 16

    @pl.when(ti == 0)
    def _():
        ext_scr[0:halo, :] = jnp.zeros((halo, w), F32)

    x = x_ref[0]
    ext_scr[halo:, :] = x
    e = ext_scr[...]
    sums = []
    d = 1
    for _ in POOL_WINDOWS:
        e = e + pltpu.roll(e, d, 0)
        sums.append(e[halo:])
        d *= 2
    ext_scr[0:halo, :] = x[tl - halo:tl]
    pos = ti * tl + lax.broadcasted_iota(jnp.int32, (tl, 1), 0)
    o_ref[0] = _pool_select(sums, x, pos, wp_ref[...], sc_ref[...])


def _pool_prompt(zd3, wp, scale):
    b, l, w = zd3.shape
    tl = min(TL_SCAN, l)
    fix = lambda i, j: (0, 0)
    return pl.pallas_call(
        _pool_prompt_kernel,
        grid=(b, l // tl),
        in_specs=[pl.BlockSpec((1, tl, w), lambda i, j: (i, j, 0)),
                  pl.BlockSpec((w, w), fix), pl.BlockSpec((1, w), fix)],
        out_specs=pl.BlockSpec((1, tl, w), lambda i, j: (i, j, 0)),
        out_shape=jax.ShapeDtypeStruct((b, l, w), F32),
        scratch_shapes=[pltpu.VMEM((tl + 16, w), F32)],
        compiler_params=_params("parallel", "arbitrary"),
        name="pool_prompt",
    )(zd3, wp, scale)


def _cd_sample_kernel(zc_ref, zd_ref, h0_ref, ch_ref, ph_ref, cw_ref, cb_ref, wg_ref, bg_ref, lam_ref,
                      wp_ref, sc_ref, oc_ref, od_ref, h_ref, cn_ref, pn_ref, xc_scr, *, n_new, db, pos0):
    w = BRANCH_W
    slab = lambda ref, t, c0=0: ref[t * db:(t + 1) * db, c0:c0 + w]
    hist = lambda ref, j: ref[j]

    n_hist = CONV_C - 1
    ext = [hist(ch_ref, j) for j in range(n_hist)] + [slab(zc_ref, t) for t in range(n_new)]
    for t in range(n_new):
        acc = cb_ref[...] + cw_ref[0:1, :] * ext[t]
        for j in range(1, CONV_C):
            acc = acc + cw_ref[j:j + 1, :] * ext[t + j]
        xc_scr[t * db:(t + 1) * db, :] = acc
    a, u = _rglru_gates(xc_scr[...], wg_ref[...], bg_ref[...], _softplus(-lam_ref[...]))
    h = h0_ref[...]
    for t in range(n_new):
        h = a[t * db:(t + 1) * db] * h + u[t * db:(t + 1) * db]
        oc_ref[t * db:(t + 1) * db, :] = h * _gelu(slab(zc_ref, t, w))
    h_ref[...] = h
    for j in range(n_hist):
        cn_ref[j] = ext[n_new + j]

    pext = [hist(ph_ref, j) for j in range(POOL_HIST)] + [slab(zd_ref, t) for t in range(n_new)]
    for t in range(n_new):
        sums, run, k = [], None, 0
        for win in POOL_WINDOWS:
            while k < win:
                term = pext[POOL_HIST + t - k]
                run = term if run is None else run + term
                k += 1
            sums.append(run)
        pos = jnp.full((db, 1), pos0 + t, jnp.int32)
        od_ref[t * db:(t + 1) * db, :] = _pool_select(sums, pext[POOL_HIST + t], pos, wp_ref[...], sc_ref[...])
    for j in range(POOL_HIST):
        pn_ref[j] = pext[n_new + j]


def _cd_sample(zc, zd, h0, conv_hist, pool_hist, cw, cb, wg, bg, lam, wp, scale, n_new, pos0):
    t, _ = zc.shape
    db = t // n_new
    w = BRANCH_W
    shp = lambda c: jax.ShapeDtypeStruct((db, c), F32)
    return pl.pallas_call(
        functools.partial(_cd_sample_kernel, n_new=n_new, db=db, pos0=pos0),
        out_shape=[jax.ShapeDtypeStruct((t, w), F32), jax.ShapeDtypeStruct((t, w), F32),
                   shp(w), jax.ShapeDtypeStruct((CONV_C - 1, db, w), F32),
                   jax.ShapeDtypeStruct((POOL_HIST, db, w), F32)],
        scratch_shapes=[pltpu.VMEM((t, w), F32)],
        compiler_params=pltpu.CompilerParams(vmem_limit_bytes=V7X_VMEM_LIMIT_BYTES),
        name="rglru_pool_sample",
    )(zc, zd, h0, conv_hist, pool_hist, cw, cb, wg, bg, lam, wp, scale)


def _merge_kernel(x_ref, oa_ref, ob_ref, oc_ref, od_ref, g_ref, wg_ref, wb_ref, wo_ref, o_ref):
    x = x_ref[...]
    d = x.shape[1]
    h = _rms(x, g_ref[...]).astype(BF16)
    mix = jnp.zeros(x.shape, F32)
    for n, br in enumerate((oa_ref, ob_ref, oc_ref, od_ref)):
        gate = _sigmoid(_dot(h, wg_ref[:, n * d:(n + 1) * d]))
        mix = mix + gate * _dot(br[...].astype(BF16), wb_ref[n])
    o_ref[...] = x + _dot(mix.astype(BF16), wo_ref[...])


def _merge(layer, x2, oa, ob, oc, od, g, wg, wb, wo):
    t, d = x2.shape
    tm = min(TM_PROJ, t)
    w = BRANCH_W
    row = lambda i: (i, 0)
    fix = lambda i: (0, 0)
    return pl.pallas_call(
        _merge_kernel,
        grid=(t // tm,),
        in_specs=[pl.BlockSpec((tm, d), row)] + [pl.BlockSpec((tm, w), row)] * 4
                 + [pl.BlockSpec((1, d), fix), pl.BlockSpec((d, N_BRANCH * d), fix),
                    pl.BlockSpec((None, N_BRANCH, w, d), lambda i: (layer, 0, 0, 0)),
                    pl.BlockSpec((None, d, d), lambda i: (layer, 0, 0))],
        out_specs=pl.BlockSpec((tm, d), row),
        out_shape=jax.ShapeDtypeStruct((t, d), F32),
        compiler_params=_params("parallel"),
        name="merge",
    )(x2, oa, ob, oc, od, g, wg, wb, wo)


def _ple(x, pe, gp, wpg, wp):
    gate = _sigmoid(_dot(_rms(x, gp).astype(BF16), wpg))
    return x + gate * _dot(pe.astype(BF16), wp)


def _ffn_prompt_kernel(x_ref, xp_ref, g_ref, wa_ref, wb_ref, cwa_ref, cwb_ref, cba_ref, cbb_ref, wd_ref,
                       pe_ref, gp_ref, wpg_ref, wp_ref, o_ref, sa_ref, sb_ref,
                       xn_scr, xnp_scr, acc_scr, ea_scr, eb_scr, *, tiles_per_seq):
    i, f = pl.program_id(0), pl.program_id(1)
    tm = x_ref.shape[0]
    halo = FFN_HALO

    @pl.when(f == 0)
    def _():
        xn_scr[...] = _rms(x_ref[...], g_ref[...]).astype(BF16)
        xnp_scr[...] = _rms(xp_ref[...], g_ref[...]).astype(BF16)
        acc_scr[...] = jnp.zeros_like(acc_scr)

    seq_start = (i % tiles_per_seq) == 0

    ts = min(FFN_STRIP, tm)
    strips = [slice(r0, r0 + ts) for r0 in range(0, tm, ts)]

    def up_proj(w_ref, e_scr, s_ref):
        e_scr[0:halo, :] = jnp.where(seq_start, 0.0, _dot(xnp_scr[...], w_ref[...]))
        for rows in strips:
            e_scr[halo + rows.start:halo + rows.stop, :] = _dot(xn_scr[rows, :], w_ref[...])
        s_ref[0] = e_scr[pl.ds(halo + tm - (CONV_F - 1), CONV_F - 1), :]

    def conv(cw_ref, cb_ref, e_scr, rows):
        y = cb_ref[...]
        for j in range(CONV_F):
            y = y + cw_ref[j:j + 1, :] * e_scr[pl.ds(halo - (CONV_F - 1) + j + rows.start, ts), :]
        return y

    up_proj(wa_ref, ea_scr, sa_ref)
    up_proj(wb_ref, eb_scr, sb_ref)
    for rows in strips:
        gated = _gelu(conv(cwa_ref, cba_ref, ea_scr, rows)) * conv(cwb_ref, cbb_ref, eb_scr, rows)
        acc_scr[rows, :] += _dot(gated.astype(BF16), wd_ref[...])

    @pl.when(f == pl.num_programs(1) - 1)
    def _():
        o_ref[...] = _ple(x_ref[...] + acc_scr[...], pe_ref[...], gp_ref[...], wpg_ref[...], wp_ref[...])


def _ffn_prompt(layer, x2, seq_len, g, w_up, cw, cb, w_down, pe3, gp, wpg, wp):
    t, d = x2.shape
    dff = w_down.shape[1]
    tm = min(TM_FFN, seq_len)
    tf = TF_FFN
    halo = FFN_HALO
    assert seq_len % tm == 0 and dff % tf == 0 and tm % halo == 0
    nf = dff // tf
    tps = seq_len // tm
    dp = pe3.shape[2]
    fix = lambda i, f: (0, 0)
    st_shape = jax.ShapeDtypeStruct((t // tm, CONV_F - 1, dff), F32)
    st_spec = pl.BlockSpec((1, CONV_F - 1, tf), lambda i, f: (i, 0, f))
    return pl.pallas_call(
        functools.partial(_ffn_prompt_kernel, tiles_per_seq=tps),
        grid=(t // tm, nf),
        in_specs=[pl.BlockSpec((tm, d), lambda i, f: (i, 0)),
                  pl.BlockSpec((halo, d), lambda i, f: (jnp.maximum(i * (tm // halo) - 1, 0), 0)),
                  pl.BlockSpec((1, d), fix),
                  pl.BlockSpec((None, d, tf), lambda i, f: (layer, 0, f)),
                  pl.BlockSpec((None, d, tf), lambda i, f: (layer, 0, f + nf)),
                  pl.BlockSpec((CONV_F, tf), lambda i, f: (0, f)),
                  pl.BlockSpec((CONV_F, tf), lambda i, f: (0, f + nf)),
                  pl.BlockSpec((1, tf), lambda i, f: (0, f)),
                  pl.BlockSpec((1, tf), lambda i, f: (0, f + nf)),
                  pl.BlockSpec((None, tf, d), lambda i, f: (layer, f, 0)),
                  pl.BlockSpec((None, tm, dp), lambda i, f: (layer, i, 0)),
                  pl.BlockSpec((1, d), fix), pl.BlockSpec((None, d, d), lambda i, f: (layer, 0, 0)),
                  pl.BlockSpec((None, dp, d), lambda i, f: (layer, 0, 0))],
        out_specs=[pl.BlockSpec((tm, d), lambda i, f: (i, 0)), st_spec, st_spec],
        out_shape=[jax.ShapeDtypeStruct((t, d), F32), st_shape, st_shape],
        scratch_shapes=[pltpu.VMEM((tm, d), BF16), pltpu.VMEM((halo, d), BF16), pltpu.VMEM((tm, d), F32),
                        pltpu.VMEM((tm + halo, tf), F32), pltpu.VMEM((tm + halo, tf), F32)],
        compiler_params=_params("arbitrary", "arbitrary"),
        name="ffn_prompt",
    )(x2, x2, g, w_up, w_up, cw, cw, cb, cb, w_down, pe3, gp, wpg, wp)


def _ffn_sample_kernel(x_ref, g_ref, wa_ref, wb_ref, cwa_ref, cwb_ref, cba_ref, cbb_ref, wd_ref,
                       ha_ref, hb_ref, pe_ref, gp_ref, wpg_ref, wp_ref,
                       o_ref, sa_ref, sb_ref, xn_scr, acc_scr, g_scr, *, n_new, db):
    f = pl.program_id(0)

    @pl.when(f == 0)
    def _():
        xn_scr[...] = _rms(x_ref[...], g_ref[...]).astype(BF16)
        acc_scr[...] = jnp.zeros_like(acc_scr)

    def conv_half(w_ref, cw_ref, cb_ref, h_ref, s_ref):
        u = _dot(xn_scr[...], w_ref[...])
        ext = [h_ref[:, j, :] for j in range(CONV_F - 1)] + [u[t * db:(t + 1) * db] for t in range(n_new)]
        for j in range(CONV_F - 1):
            s_ref[:, j, :] = ext[n_new + j]
        ys = []
        for t in range(n_new):
            y = cb_ref[...] + cw_ref[0:1, :] * ext[t]
            for j in range(1, CONV_F):
                y = y + cw_ref[j:j + 1, :] * ext[t + j]
            ys.append(y)
        return ys

    ya = conv_half(wa_ref, cwa_ref, cba_ref, ha_ref, sa_ref)
    yb = conv_half(wb_ref, cwb_ref, cbb_ref, hb_ref, sb_ref)
    for t in range(n_new):
        g_scr[t * db:(t + 1) * db, :] = (_gelu(ya[t]) * yb[t]).astype(BF16)
    acc_scr[...] += _dot(g_scr[...], wd_ref[...])

    @pl.when(f == pl.num_programs(0) - 1)
    def _():
        o_ref[...] = _ple(x_ref[...] + acc_scr[...], pe_ref[...], gp_ref[...], wpg_ref[...], wp_ref[...])


def _ffn_sample(layer, x2, g, w_up, cw, cb, w_down, hist4, pe3, gp, wpg, wp, n_new):
    t, d = x2.shape
    db = t // n_new
    dff = w_down.shape[1]
    tf = TF_FFN
    nf = dff // tf
    dp = pe3.shape[2]
    nh = CONV_F - 1
    fix = lambda f: (0, 0)
    hist = lambda half: pl.BlockSpec((None, db, nh, tf), lambda f: (layer, 0, 0, f + half * nf))
    st_shape = jax.ShapeDtypeStruct((db, nh, dff), F32)
    st_spec = pl.BlockSpec((db, nh, tf), lambda f: (0, 0, f))
    return pl.pallas_call(
        functools.partial(_ffn_sample_kernel, n_new=n_new, db=db),
        grid=(nf,),
        in_specs=[pl.BlockSpec((t, d), fix), pl.BlockSpec((1, d), fix),
                  pl.BlockSpec((None, d, tf), lambda f: (layer, 0, f)),
                  pl.BlockSpec((None, d, tf), lambda f: (layer, 0, f + nf)),
                  pl.BlockSpec((CONV_F, tf), lambda f: (0, f)), pl.BlockSpec((CONV_F, tf), lambda f: (0, f + nf)),
                  pl.BlockSpec((1, tf), lambda f: (0, f)), pl.BlockSpec((1, tf), lambda f: (0, f + nf)),
                  pl.BlockSpec((None, tf, d), lambda f: (layer, f, 0)),
                  hist(0), hist(1),
                  pl.BlockSpec((None, t, dp), lambda f: (layer, 0, 0)),
                  pl.BlockSpec((1, d), fix), pl.BlockSpec((None, d, d), lambda f: (layer, 0, 0)),
                  pl.BlockSpec((None, dp, d), lambda f: (layer, 0, 0))],
        out_specs=[pl.BlockSpec((t, d), fix), st_spec, st_spec],
        out_shape=[jax.ShapeDtypeStruct((t, d), F32), st_shape, st_shape],
        scratch_shapes=[pltpu.VMEM((t, d), BF16), pltpu.VMEM((t, d), F32), pltpu.VMEM((t, tf), BF16)],
        compiler_params=_params("arbitrary"),
        name="ffn_sample",
    )(x2, g, w_up, w_up, cw, cw, cb, cb, w_down, hist4, hist4, pe3, gp, wpg, wp)


def _block_diag(w4):
    h, n, _ = w4.shape
    eye = jnp.eye(h, dtype=w4.dtype)
    return (eye[:, None, :, None] * w4[:, :, None, :]).reshape(h * n, h * n)


def kernel(x_prompt, x_sample, cache_k, cache_v, state_hgrn, state_rglru_h, state_rglru_conv, state_pool,
           state_ffn_conv, page_table, p_prompt, p_sample, norm_mix, w_in, q_norm, k_norm, sb_bias, lb_logits,
           hgrn_norm, conv_c_w, conv_c_b, w_rg_a, b_rg_a, w_rg_x, b_rg_x, lam, w_pool, pool_scale, w_branch,
           w_out, norm_ffn, w_up, conv_f_w, conv_f_b, w_down, norm_ple, w_ple_gate, w_ple):
    depth = w_in.shape[0]
    b, l, d = x_prompt.shape
    db, n_new, _ = x_sample.shape
    w = BRANCH_W
    n_mix = 10 * w
    dff = w_down.shape[1]
    dp = p_prompt.shape[-1]
    n_pool, page = cache_k.shape[1], cache_k.shape[2]
    pos0 = page_table.shape[1] * page

    lb_all, loglb_all, log1m_all = _lower_bounds(lb_logits.astype(F32))
    row = lambda a, i: a[i].reshape(1, -1)
    tile_heads = lambda a, i: jnp.tile(a[i], N_HEAD).reshape(1, w)

    xp = x_prompt.reshape(b * l, d)
    xs = x_sample.transpose(1, 0, 2).reshape(n_new * db, d)
    pe_p = p_prompt.reshape(depth, b * l, dp)
    pe_s = p_sample.transpose(0, 2, 1, 3).reshape(depth, n_new * db, dp)
    cache_kt = cache_k.transpose(0, 1, 3, 4, 2)
    cache_vt = cache_v.transpose(0, 1, 3, 4, 2)
    hgrn_state = state_hgrn.transpose(0, 2, 3, 4, 1).reshape(depth, -1, db)
    conv_state = state_rglru_conv.transpose(0, 2, 1, 3)
    pool_state = state_pool.transpose(0, 2, 1, 3)

    w_br, w_o = w_branch.astype(BF16), w_out.astype(BF16)
    w_u, w_d = w_up.astype(BF16), w_down.astype(BF16)
    w_pg, w_pe = w_ple_gate.astype(BF16), w_ple.astype(BF16)

    outs = {k: [] for k in ("kp", "vp", "ks", "vs", "sp", "ss", "hp", "hs", "cp", "cs", "pp", "ps", "fp", "fs")}
    for i in range(depth):
        w_mix = w_in[i, :, :n_mix].astype(BF16)
        w_gate = w_in[i, :, n_mix:].astype(BF16)
        wg_c = jnp.concatenate([_block_diag(w_rg_a[i]), _block_diag(w_rg_x[i])], axis=1).astype(BF16)
        bg_c = jnp.concatenate([b_rg_a[i], b_rg_x[i]]).reshape(1, 2 * w)
        wp_d = _block_diag(w_pool[i]).astype(BF16)
        g_mix, g_ffn, g_ple = row(norm_mix, i), row(norm_ffn, i), row(norm_ple, i)
        qn, kn, gn = tile_heads(q_norm, i), tile_heads(k_norm, i), tile_heads(hgrn_norm, i)
        lb, loglb, log1m = row(lb_all, i), row(loglb_all, i), row(log1m_all, i)
        cw_c, cb_c, lam_i, sc_d = conv_c_w[i], row(conv_c_b, i), row(lam, i), row(pool_scale, i)
        cw_f, cb_f = conv_f_w[i], row(conv_f_b, i)
        bias = sb_bias[i].astype(F32)

        q, kt, vt, ktb, vtb, zb, zc, zd = _inproj_prompt(xp.reshape(b, l, d), g_mix, w_mix, qn, kn)
        o_a = _attn_prompt(bias, q, ktb, vtb)
        o_b, s_p = _hgrn_prompt(zb, lb, loglb, log1m, gn)
        o_c, h_p = _rglru_prompt(zc, cw_c, cb_c, wg_c, bg_c, lam_i)
        o_d = _pool_prompt(zd, wp_d, sc_d)
        f2 = lambda a: a.reshape(b * l, w)
        x1 = _merge(i, xp, f2(o_a), f2(o_b), f2(o_c), f2(o_d), g_mix, w_gate, w_br, w_o)
        xp, fa, fb = _ffn_prompt(i, x1, l, g_ffn, w_u, cw_f, cb_f, w_d, pe_p, g_ple, w_pg, w_pe)
        outs["kp"].append(kt)
        outs["vp"].append(vt)
        s_heads = s_p.reshape(b, N_HEAD, HEAD_W, N_HEAD, HEAD_W)
        outs["sp"].append(jnp.stack([s_heads[:, h, :, h, :] for h in range(N_HEAD)], axis=1))
        outs["hp"].append(h_p.reshape(b, w))
        outs["cp"].append(zc[:, l - (CONV_C - 1):, :w])
        outs["pp"].append(zd[:, l - POOL_HIST:, :])
        tps = fa.shape[0] // b
        outs["fp"].append(jnp.concatenate([fa[tps - 1::tps], fb[tps - 1::tps]], axis=-1))

        q, k, v, zb, zc, zd = _inproj_sample(xs, g_mix, w_mix, qn, kn)
        r4 = lambda a: a.reshape(n_new, db, 1, w)
        o_a = _attn_sample(i, page_table, bias, r4(q), r4(k), r4(v), cache_kt, cache_vt).reshape(n_new * db, w)
        o_b, s_s = _hgrn_sample(i, zb, hgrn_state, lb, gn, n_new)
        o_c, o_d, h_s, c_s, p_s = _cd_sample(zc, zd, state_rglru_h[i], conv_state[i], pool_state[i],
                                             cw_c, cb_c, wg_c, bg_c, lam_i, wp_d, sc_d, n_new, pos0)
        x1 = _merge(i, xs, o_a, o_b, o_c, o_d, g_mix, w_gate, w_br, w_o)
        xs, fa, fb = _ffn_sample(i, x1, g_ffn, w_u, cw_f, cb_f, w_d, state_ffn_conv, pe_s,
                                 g_ple, w_pg, w_pe, n_new)
        outs["ks"].append(k)
        outs["vs"].append(v)
        outs["ss"].append(s_s)
        outs["hs"].append(h_s)
        outs["cs"].append(c_s)
        outs["ps"].append(p_s)
        outs["fs"].append(jnp.concatenate([fa, fb], axis=-1))

    stk = lambda key: jnp.stack(outs[key], axis=0)
    to_batch_major = lambda a: a.reshape(depth, n_new, db, N_HEAD, HEAD_W).transpose(0, 2, 1, 3, 4)
    y_sample = xs.reshape(n_new, db, d).transpose(1, 0, 2)
    from_transposed = lambda a: a.reshape(depth, b, N_HEAD, HEAD_W, l).transpose(0, 1, 4, 2, 3)
    hgrn_s = stk("ss").reshape(depth, N_HEAD, HEAD_W, HEAD_W, db).transpose(0, 4, 1, 2, 3)
    rows_to_batch = lambda a: a.transpose(0, 2, 1, 3)
    return (xp.reshape(b, l, d), y_sample,
            from_transposed(stk("kp")), from_transposed(stk("vp")),
            to_batch_major(stk("ks")), to_batch_major(stk("vs")),
            stk("sp"), hgrn_s, stk("hp"), stk("hs"), stk("cp"), rows_to_batch(stk("cs")),
            stk("pp"), rows_to_batch(stk("ps")), stk("fp"), stk("fs"))
```

```python
import functools
import math

import jax
import jax.numpy as jnp
from jax import lax
from jax.experimental import pallas as pl
from jax.experimental.pallas import tpu as pltpu

F32 = jnp.float32
BF16 = jnp.bfloat16
EPS = 1e-6

N_HEAD = 4
HEAD_W = 64
BRANCH_W = N_HEAD * HEAD_W
N_BRANCH = 4
RG_C = 8.0
POOL_WINDOWS = (2, 4, 8, 16)
POOL_HIST = max(POOL_WINDOWS) - 1
CONV_C = 4
CONV_F = 3
HGRN_CHUNK = 64
HGRN_SUB = 16
NEG_BIG = -1e30
LOG2E = math.log2(math.e)

V7X_VMEM_LIMIT_BYTES = 56 * 1024 * 1024

TM_PROJ = 512
TQ_ATTN = 256
TK_ATTN = 256
TL_SCAN = 512
TL_HGRN = 256
TM_FFN = 1024
TF_FFN = 256
TF_FFN_SAMPLE = 1408
FFN_HALO = 16
FFN_STRIP = 256
SAMPLES_PER_STEP = 2


def _params(*sem):
    return pltpu.CompilerParams(dimension_semantics=sem, vmem_limit_bytes=V7X_VMEM_LIMIT_BYTES)


def _dot(a, b):
    return jnp.dot(a, b, preferred_element_type=F32)


def _dot_nt(a, b):
    return lax.dot_general(a, b, (((1,), (1,)), ((), ())), preferred_element_type=F32)


def _dot_tn(a, b):
    return lax.dot_general(a, b, (((0,), (0,)), ((), ())), preferred_element_type=F32)


def _split2(x):
    hi = x.astype(BF16)
    lo = (x - hi.astype(F32)).astype(BF16)
    return hi, lo


def _split3(x):
    hi = x.astype(BF16)
    r = x - hi.astype(F32)
    mid = r.astype(BF16)
    lo = (r - mid.astype(F32)).astype(BF16)
    return hi, mid, lo


def _dot_x2(x, m):
    hi, lo = _split2(x)
    return _dot(hi, m) + _dot(lo, m)


def _sigmoid(x):
    return 1.0 / (1.0 + jnp.exp(-x))


def _softplus_tail(x):
    return jnp.log1p(jnp.exp(-jnp.abs(x)))


def _log_sigmoid(x):
    return jnp.minimum(x, 0.0) - _softplus_tail(x)


def _softplus(x):
    return jnp.maximum(x, 0.0) + _softplus_tail(x)


def _silu(x):
    return x * _sigmoid(x)


def _gelu(x):
    c = math.sqrt(2.0 / math.pi)
    return 0.5 * x * (1.0 + jnp.tanh(c * (x + 0.044715 * (x * x * x))))


def _rms(x, g):
    ms = jnp.mean(x * x, axis=-1, keepdims=True)
    return x * lax.rsqrt(ms + EPS) * g


def _head_id(shape, dim):
    return lax.shift_right_logical(lax.broadcasted_iota(jnp.int32, shape, dim), 6)


def _head_block_ones():
    n = BRANCH_W
    return jnp.where(_head_id((n, n), 0) == _head_id((n, n), 1), 1.0, 0.0).astype(BF16)


def _head_rms(a, g, bd):
    ms = _dot_x2(a * a, bd) * (1.0 / HEAD_W)
    return a * lax.rsqrt(ms + EPS) * g


def _shift_rows(x, d, fill):
    rolled = pltpu.roll(x, d, 0)
    row = lax.broadcasted_iota(jnp.int32, x.shape, 0)
    return jnp.where(row < d, fill, rolled)


def _lb_kernel(lg_ref, lb_ref, loglb_ref, log1m_ref):
    x = lg_ref[...]
    depth = x.shape[0]
    rows = [x[i:i + 1] for i in range(depth)]
    m = functools.reduce(jnp.maximum, rows)
    e = [jnp.exp(r - m) for r in rows]
    tot = functools.reduce(lambda a, b: a + b, e)
    zero = jnp.zeros_like(m)
    lb_ref[0:1, :] = zero
    loglb_ref[0:1, :] = jnp.full_like(m, -jnp.inf)
    log1m_ref[0:1, :] = zero
    acc = zero
    for i in range(1, depth):
        acc = acc + e[i] / tot
        lb_ref[i:i + 1, :] = acc
        loglb_ref[i:i + 1, :] = jnp.log(acc)
        log1m_ref[i:i + 1, :] = jnp.log1p(-acc)


def _lower_bounds(lb_logits):
    shp = jax.ShapeDtypeStruct(lb_logits.shape, F32)
    return pl.pallas_call(_lb_kernel, out_shape=(shp, shp, shp), name="hgrn_lower_bounds")(lb_logits)


def _inproj_common(x, g_ref, w_ref, qn_ref, kn_ref):
    h = _rms(x, g_ref[...]).astype(BF16)
    z = _dot(h, w_ref[...])
    bd = _head_block_ones()
    w = BRANCH_W
    q = _head_rms(z[:, 0:w], qn_ref[...], bd) * (HEAD_W ** -0.5 * LOG2E)
    k = _head_rms(z[:, w:2 * w], kn_ref[...], bd)
    return q, k, z[:, 2 * w:3 * w], z[:, 3 * w:7 * w], z[:, 7 * w:9 * w], z[:, 9 * w:10 * w]


def _inproj_prompt_kernel(x_ref, g_ref, w_ref, qn_ref, kn_ref,
                          q_o, kt_o, vt_o, ktb_o, vtb_o, zb_o, zc_o, zd_o):
    q, k, v, zb, zc, zd = _inproj_common(x_ref[0], g_ref, w_ref, qn_ref, kn_ref)
    q_o[0] = q.astype(BF16)
    kt, vt = k.T, v.T
    kt_o[0] = kt
    vt_o[0] = vt
    tk = ktb_o.shape[2]
    for c in range(ktb_o.shape[0]):
        ktb_o[c] = kt[:, c * tk:(c + 1) * tk].astype(BF16)
        vtb_o[c] = vt[:, c * tk:(c + 1) * tk].astype(BF16)
    zb_o[0] = zb
    zc_o[0] = zc
    zd_o[0] = zd


def _inproj_prompt(x3, g, w, qn, kn):
    b, l, d = x3.shape
    tm = min(TM_PROJ, l)
    tk = min(TK_ATTN, l)
    n = w.shape[1]
    bw = BRANCH_W
    per = tm // tk
    row = lambda i, j: (i, j, 0)
    fix = lambda i, j: (0, 0)
    col = lambda i, j: (i, 0, j)
    tiles = lambda i, j: (i * (l // tm) + j, 0, 0)
    tok = lambda c, dt: (jax.ShapeDtypeStruct((b, l, c), dt), pl.BlockSpec((1, tm, c), row))
    tr = (jax.ShapeDtypeStruct((b, bw, l), F32), pl.BlockSpec((1, bw, tm), col))
    trb = (jax.ShapeDtypeStruct((b * l // tk, bw, tk), BF16), pl.BlockSpec((per, bw, tk), tiles))
    outs = [tok(bw, BF16), tr, tr, trb, trb, tok(4 * bw, F32), tok(2 * bw, F32), tok(bw, F32)]
    return pl.pallas_call(
        _inproj_prompt_kernel,
        grid=(b, l // tm),
        in_specs=[pl.BlockSpec((1, tm, d), row), pl.BlockSpec((1, d), fix), pl.BlockSpec((d, n), fix),
                  pl.BlockSpec((1, bw), fix), pl.BlockSpec((1, bw), fix)],
        out_specs=[s for _, s in outs],
        out_shape=[s for s, _ in outs],
        compiler_params=_params("parallel", "parallel"),
        name="inproj_prompt",
    )(x3, g, w, qn, kn)


def _inproj_sample_kernel(x_ref, g_ref, w_ref, qn_ref, kn_ref, q_o, k_o, v_o, zb_o, zc_o, zd_o):
    for ref, val in zip((q_o, k_o, v_o, zb_o, zc_o, zd_o),
                        _inproj_common(x_ref[...], g_ref, w_ref, qn_ref, kn_ref)):
        ref[...] = val


def _inproj_sample(x2, g, w, qn, kn):
    t, d = x2.shape
    bw = BRANCH_W
    widths = (bw, bw, bw, 4 * bw, 2 * bw, bw)
    return pl.pallas_call(
        _inproj_sample_kernel,
        out_shape=[jax.ShapeDtypeStruct((t, c), F32) for c in widths],
        compiler_params=pltpu.CompilerParams(vmem_limit_bytes=V7X_VMEM_LIMIT_BYTES),
        name="inproj_sample",
    )(x2, g, w, qn, kn)


def _upper_ones(n):
    r = lax.broadcasted_iota(jnp.int32, (n, n), 0)
    c = lax.broadcasted_iota(jnp.int32, (n, n), 1)
    return jnp.where(r > c, 1.0, 0.0).astype(BF16)


def _sb_logs(z, mask):
    m = jnp.maximum(z, 0.0)
    n = z - m
    t = jnp.log2(1.0 + jnp.exp2(n - m))
    drop = m + t
    return n - t, (drop if mask is None else jnp.where(mask, drop, 0.0))


def _attn_prompt_kernel(bias_ref, q_ref, kt_ref, vt_ref, o_ref):
    qi = pl.program_id(1)
    tq, tk = q_ref.shape[1], kt_ref.shape[2]
    per = tq // tk
    q = q_ref[0]
    heads = range(N_HEAD)
    lanes = lambda h: slice(h * HEAD_W, (h + 1) * HEAD_W)
    qs = [q[:, lanes(h)] for h in heads]
    bias = [bias_ref[h] * LOG2E for h in heads]
    upper = _upper_ones(tk)
    q_pos = qi * tq + lax.broadcasted_iota(jnp.int32, (tq, tk), 0)
    k_off = lax.broadcasted_iota(jnp.int32, (tq, tk), 1)

    def logits(kidx):
        return [_dot(qs[h], kt_ref[kidx, lanes(h), :]) + bias[h] for h in heads]

    def sums(zs, mask):
        logs = [_sb_logs(z, mask) for z in zs]
        local = [_dot(drop.astype(BF16), upper) for _, drop in logs]
        return ([lb for lb, _ in logs], local, [local[h][:, 0:1] + logs[h][1][:, 0:1] for h in heads])

    def values(kidx, log_beta, local, carries, mask):
        pvs = []
        for h in heads:
            w = jnp.exp2(log_beta[h] - (local[h] + carries[h]))
            if mask is not None:
                w = jnp.where(mask, w, 0.0)
            pvs.append(_dot_nt(w.astype(BF16), vt_ref[kidx, lanes(h), :]))
        return pvs

    add = lambda xs, ys: [x + y for x, y in zip(xs, ys)]
    first = qi * per
    accs = [jnp.zeros((tq, HEAD_W), F32)] * N_HEAD
    carries = [jnp.zeros((tq, 1), F32)] * N_HEAD
    for m in range(per - 1, -1, -1):
        kidx = first + m
        mask = kidx * tk + k_off < q_pos
        log_beta, local, totals = sums(logits(kidx), mask)
        accs = add(accs, values(kidx, log_beta, local, carries, mask))
        carries = add(carries, totals)

    def body(j, state):
        accs, carries = state
        kidx = first - 1 - j
        log_beta, local, totals = sums(logits(kidx), None)
        return add(accs, values(kidx, log_beta, local, carries, None)), add(carries, totals)

    accs = lax.fori_loop(0, first, body, (accs, carries))[0]
    for h in heads:
        o_ref[0, :, lanes(h)] = accs[h]


def _attn_prompt(bias, q3, ktb, vtb):
    b, l, w = q3.shape
    tk = ktb.shape[2]
    tq = min(TQ_ATTN, l)
    nk = l // tk
    return pl.pallas_call(
        _attn_prompt_kernel,
        grid=(b, l // tq),
        in_specs=[pl.BlockSpec(memory_space=pltpu.SMEM),
                  pl.BlockSpec((1, tq, w), lambda i, j: (i, j, 0)),
                  pl.BlockSpec((nk, w, tk), lambda i, j: (i, 0, 0)),
                  pl.BlockSpec((nk, w, tk), lambda i, j: (i, 0, 0))],
        out_specs=pl.BlockSpec((1, tq, w), lambda i, j: (i, j, 0)),
        out_shape=jax.ShapeDtypeStruct((b, l, w), F32),
        compiler_params=_params("parallel", "arbitrary"),
        name="attn_prompt",
    )(bias, q3, ktb, vtb)


def _attn_sample_kernel(pt_ref, bias_ref, q_ref, kn_ref, vn_ref, *refs, n_pages, n_new, n_samp):
    del pt_ref
    o_ref = refs[-1]
    samples = range(n_samp)
    k_refs = [refs[s * n_pages:(s + 1) * n_pages] for s in samples]
    v_refs = [refs[(n_samp + s) * n_pages:(n_samp + s + 1) * n_pages] for s in samples]
    page = refs[0].shape[-1]
    flat = lambda ref: ref[...].reshape(BRANCH_W, page).astype(BF16)
    rows = N_HEAD * 8
    row = lax.broadcasted_iota(jnp.int32, (rows, 1), 0)
    row_t = jnp.bitwise_and(row, 7)
    row_h = lax.shift_right_logical(row, 3)
    lane_head = _head_id((1, BRANCH_W), 1)
    bias = jnp.zeros((rows, 1), F32)
    for h in range(N_HEAD):
        bias = jnp.where(row_h == h, bias_ref[h] * LOG2E, bias)

    qs, carry, out = [], [], []
    for s in samples:
        q = jnp.zeros((rows, BRANCH_W), F32)
        for t in range(n_new):
            q = q + jnp.where(row_t == t, q_ref[t, s], 0.0)
        q = jnp.where(row_h == lane_head, q, 0.0)
        c = jnp.zeros((rows, 1), F32)
        o = jnp.zeros((rows, BRANCH_W), F32)
        for j in reversed(range(n_new)):
            seen = row_t > j
            log_beta, drop = _sb_logs(jnp.sum(q * kn_ref[j, s], axis=-1, keepdims=True) + bias, seen)
            o = o + jnp.where(seen, jnp.exp2(log_beta - c), 0.0) * vn_ref[j, s]
            c = c + drop
        qs.append(q)
        carry.append(c)
        out.append(o)

    cols = lambda a, p: a[:, p * page:(p + 1) * page]
    kt = [jnp.concatenate([flat(r) for r in k_refs[s]], axis=1) for s in samples]
    logs = [_sb_logs(_dot(qs[s].astype(BF16), kt[s]) + bias, None) for s in samples]
    stacked = [jnp.concatenate([cols(drop, p) for p in range(n_pages)], axis=0).astype(BF16) for _, drop in logs]
    upper = _upper_ones(page)
    local = [_dot(st, upper) for st in stacked]
    ws = []
    for s in samples:
        later, c = [None] * n_pages, carry[s]
        for p in reversed(range(n_pages)):
            loc = local[s][p * rows:(p + 1) * rows]
            later[p] = loc + c
            c = c + loc[:, 0:1] + cols(logs[s][1], p)[:, 0:1]
        ws.append(jnp.exp2(logs[s][0] - jnp.concatenate(later, axis=1)).astype(BF16))
    vt = [jnp.concatenate([flat(r) for r in v_refs[s]], axis=1) for s in samples]
    out = [out[s] + _dot_nt(ws[s], vt[s]) for s in samples]

    for s in samples:
        res = jnp.zeros((8, BRANCH_W), F32)
        for h in range(N_HEAD):
            res = res + jnp.where(lane_head == h, out[s][h * 8:(h + 1) * 8], 0.0)
        for t in range(n_new):
            o_ref[t, s] = res[t:t + 1]


def _attn_sample(layer, page_table, bias, q4, kn4, vn4, cache_kt, cache_vt):
    n_new, db, _, w = q4.shape
    n_pages = page_table.shape[1]
    page = cache_kt.shape[4]
    n_samp = SAMPLES_PER_STEP
    assert db % n_samp == 0
    new_spec = pl.BlockSpec((n_new, n_samp, 1, w), lambda g, pt, bs: (0, g, 0, 0))

    def page_spec(s, p):
        return pl.BlockSpec((None, None, N_HEAD, HEAD_W, page),
                            lambda g, pt, bs: (layer, pt[g * n_samp + s, p], 0, 0, 0))

    page_specs = [page_spec(s, p) for s in range(n_samp) for p in range(n_pages)]
    grid_spec = pltpu.PrefetchScalarGridSpec(
        num_scalar_prefetch=2,
        grid=(db // n_samp,),
        in_specs=[new_spec, new_spec, new_spec] + page_specs * 2,
        out_specs=new_spec,
    )
    n_ops = n_samp * n_pages
    return pl.pallas_call(
        functools.partial(_attn_sample_kernel, n_pages=n_pages, n_new=n_new, n_samp=n_samp),
        grid_spec=grid_spec,
        out_shape=jax.ShapeDtypeStruct(q4.shape, F32),
        compiler_params=_params("arbitrary"),
        name="attn_sample",
    )(page_table, bias, q4, kn4, vn4, *([cache_kt] * n_ops), *([cache_vt] * n_ops))


def _hgrn_gates(qb, fr, lb, loglb, log1m):
    c = log1m + _log_sigmoid(fr)
    log_f = jnp.maximum(loglb, c) + jnp.log1p(jnp.exp(-jnp.abs(loglb - c)))
    key = (1.0 - lb) * _sigmoid(-fr)
    return _silu(qb), key, log_f


def _hgrn_prompt_kernel(z_ref, lb_ref, loglb_ref, log1m_ref, gn_ref, o_ref, s_ref,
                        st_scr, q_scr, k_scr, v_scr, b_scr, p_scr, vx_scr, od_scr):
    ti = pl.program_id(0)
    w = BRANCH_W
    ck, sub = HGRN_CHUNK, HGRN_SUB
    n_sub = ck // sub
    nb, tl = z_ref.shape[0], z_ref.shape[1]
    rows = range(nb)

    @pl.when(ti == 0)
    def _():
        st_scr[...] = jnp.zeros_like(st_scr)

    bd = _head_block_ones()
    bd_mask = _head_id((w, w), 0) == _head_id((w, w), 1)
    r = lax.broadcasted_iota(jnp.int32, (ck, ck), 0)
    c = lax.broadcasted_iota(jnp.int32, (ck, ck), 1)
    lower_incl = jnp.where(c <= r, 1.0, 0.0).astype(BF16)
    row_ck = lax.broadcasted_iota(jnp.int32, (ck, 1), 0)
    row_sub = lax.broadcasted_iota(jnp.int32, (sub, 1), 0)
    stack_mask = (lax.shift_right_logical(lax.broadcasted_iota(jnp.int32, (N_HEAD * sub, w), 0), 4)
                  == _head_id((N_HEAD * sub, w), 1))
    lb, loglb, log1m, gn = lb_ref[...], loglb_ref[...], log1m_ref[...], gn_ref[...]
    stack = lambda a: jnp.where(stack_mask, jnp.concatenate([a] * N_HEAD, axis=0), 0.0).astype(BF16)

    def chunk(ci, _):
        r0 = pl.multiple_of(ci * ck, ck)
        zz = [z_ref[n, pl.ds(r0, ck), :] for n in rows]
        gates = [_hgrn_gates(z[:, 0:w], z[:, w:2 * w], lb, loglb, log1m) for z in zz]
        qh, key = [g[0] for g in gates], [g[1] for g in gates]
        val = [z[:, 2 * w:3 * w] for z in zz]
        parts = [_split3(g[2]) for g in gates]
        b = [_dot(lower_incl, hi) + _dot(lower_incl, mid) + _dot(lower_incl, lo)
             for hi, mid, lo in parts]
        for n in rows:
            q_scr[n] = qh[n]
            k_scr[n] = key[n]
            v_scr[n] = val[n]
            b_scr[n] = b[n] * LOG2E

        st = [st_scr[n] for n in rows]
        out = [_dot_nt((qh[n] * jnp.exp(b[n])).astype(BF16), st[n].astype(BF16)) for n in rows]
        for j in range(n_sub - 1):
            blk = slice(j * sub, (j + 1) * sub)
            e_j = [x[(j + 1) * sub - 1:(j + 1) * sub] for x in b]
            qj = [(qh[n] * jnp.exp(jnp.where(row_ck >= (j + 1) * sub, b[n] - e_j[n], NEG_BIG))).astype(BF16)
                  for n in rows]
            k_st = [stack(key[n][blk] * jnp.exp(e_j[n] - b[n][blk])) for n in rows]
            att = [_dot_nt(qj[n], k_st[n]).astype(BF16) for n in rows]
            out = [out[n] + _dot(att[n], stack(val[n][blk])) for n in rows]

        b_last = [x[ck - 1:ck] for x in b]
        upd = [_dot_tn(val[n].astype(BF16), (key[n] * jnp.exp(b_last[n] - b[n])).astype(BF16)) for n in rows]
        for n in rows:
            st_scr[n] = jnp.where(bd_mask, st[n] * jnp.exp(b_last[n]) + upd[n], 0.0)

        def diag(n, _):
            for si in range(n_sub):
                s0 = si * sub
                q_i = q_scr[n, s0:s0 + sub, :]
                b_i = b_scr[n, s0:s0 + sub, :]
                for s in range(sub):
                    k_s = k_scr[n, s0 + s:s0 + s + 1, :]
                    b_s = b_scr[n, s0 + s:s0 + s + 1, :]
                    v_s = v_scr[n, s0 + s:s0 + s + 1, :]
                    dst = slice((s0 + s) * sub, (s0 + s + 1) * sub)
                    p_scr[dst, :] = (q_i * k_s * jnp.exp2(jnp.where(row_sub >= s, b_i - b_s, NEG_BIG))).astype(BF16)
                    vx_scr[dst, :] = jnp.broadcast_to(v_s, (sub, w))
            att = _dot(p_scr[...], bd)
            od_scr[n] = jnp.sum((att * vx_scr[...]).reshape(n_sub, sub, sub, w), axis=1).reshape(ck, w)
            return 0

        lax.fori_loop(0, nb, diag, 0)
        for n in rows:
            o_ref[n, pl.ds(r0, ck), :] = (_head_rms(out[n] + od_scr[n], gn, bd) * _silu(zz[n][:, 3 * w:4 * w]))
        return 0

    lax.fori_loop(0, tl // ck, chunk, 0)

    @pl.when(ti == pl.num_programs(0) - 1)
    def _():
        for n in rows:
            s_ref[n] = st_scr[n].T


def _hgrn_prompt(zb3, lb, loglb, log1m, gn):
    b, l, _ = zb3.shape
    w = BRANCH_W
    ck = HGRN_CHUNK
    tl = min(TL_HGRN, l)
    assert l % tl == 0 and tl % ck == 0
    fix = lambda j: (0, 0)
    scr = lambda rows, dt=F32: pltpu.VMEM((b, rows, w), dt)
    pairs = ck * HGRN_SUB
    return pl.pallas_call(
        _hgrn_prompt_kernel,
        grid=(l // tl,),
        in_specs=[pl.BlockSpec((b, tl, 4 * w), lambda j: (0, j, 0))] + [pl.BlockSpec((1, w), fix)] * 4,
        out_specs=[pl.BlockSpec((b, tl, w), lambda j: (0, j, 0)),
                   pl.BlockSpec((b, w, w), lambda j: (0, 0, 0))],
        out_shape=[jax.ShapeDtypeStruct((b, l, w), F32), jax.ShapeDtypeStruct((b, w, w), F32)],
        scratch_shapes=[scr(w), scr(ck), scr(ck), scr(ck), scr(ck),
                        pltpu.VMEM((pairs, w), BF16), pltpu.VMEM((pairs, w), F32), scr(ck)],
        compiler_params=_params("arbitrary"),
        name="hgrn_prompt",
    )(zb3, lb, loglb, log1m, gn)


def _hgrn_sample_kernel(z_ref, s0_ref, lb_ref, gn_ref, o_ref, s_ref, f_scr, k_scr, q_scr, v_scr, o_scr,
                        *, n_new, db):
    j = pl.program_id(0)
    w = BRANCH_W
    k_per_step = s0_ref.shape[0] // HEAD_W
    steps_per_head = HEAD_W // k_per_step
    lb = lb_ref[...]

    @pl.when(j == 0)
    def _():
        for t in range(n_new):
            zz = z_ref[t * db:(t + 1) * db, :]
            fr = zz[:, w:2 * w]
            f_scr[t] = (lb + (1.0 - lb) * _sigmoid(fr)).T
            k_scr[t] = ((1.0 - lb) * _sigmoid(-fr)).T
            q_scr[t] = _silu(zz[:, 0:w]).T
            v_scr[t] = zz[:, 2 * w:3 * w].T
            o_scr[t] = jnp.zeros((w, db), F32)

    head = j // steps_per_head
    k_base = head * HEAD_W + (j % steps_per_head) * k_per_step
    v0 = pl.multiple_of(head * HEAD_W, HEAD_W)
    for kk in range(k_per_step):
        rows = slice(kk * HEAD_W, (kk + 1) * HEAD_W)
        s_k = s0_ref[rows, :]
        row = k_base + kk
        for t in range(n_new):
            s_k = (f_scr[t, pl.ds(row, 1), :] * s_k
                   + k_scr[t, pl.ds(row, 1), :] * v_scr[t, pl.ds(v0, HEAD_W), :])
            o_scr[t, pl.ds(v0, HEAD_W), :] += q_scr[t, pl.ds(row, 1), :] * s_k
        s_ref[rows, :] = s_k

    @pl.when(j == pl.num_programs(0) - 1)
    def _():
        bd = _head_block_ones()
        for t in range(n_new):
            gate = _silu(z_ref[t * db:(t + 1) * db, 3 * w:4 * w])
            o_ref[t * db:(t + 1) * db, :] = _head_rms(o_scr[t].T, gn_ref[...], bd) * gate


def _hgrn_sample(layer, zb, state_t, lb, gn, n_new):
    t, _ = zb.shape
    db = t // n_new
    w = BRANCH_W
    n_state = state_t.shape[1]
    rows = 16 * HEAD_W
    fix = lambda j: (0, 0)
    scr = pltpu.VMEM((n_new, w, db), F32)
    return pl.pallas_call(
        functools.partial(_hgrn_sample_kernel, n_new=n_new, db=db),
        grid=(n_state // rows,),
        in_specs=[pl.BlockSpec((t, 4 * w), fix),
                  pl.BlockSpec((None, rows, db), lambda j: (layer, j, 0)),
                  pl.BlockSpec((1, w), fix), pl.BlockSpec((1, w), fix)],
        out_specs=[pl.BlockSpec((t, w), fix), pl.BlockSpec((rows, db), lambda j: (j, 0))],
        out_shape=[jax.ShapeDtypeStruct((t, w), F32), jax.ShapeDtypeStruct((n_state, db), F32)],
        scratch_shapes=[scr] * 5,
        compiler_params=_params("arbitrary"),
        name="hgrn_sample",
    )(zb, state_t, lb, gn)


def _rglru_gates(xconv, wg, bg, sp_lam):
    w = BRANCH_W
    g = _sigmoid(_dot(xconv.astype(BF16), wg) + bg)
    log_a = -RG_C * g[:, 0:w] * sp_lam
    a = jnp.exp(log_a)
    one_minus_a2 = -jnp.tanh(log_a) * (a * a + 1.0)
    u = jnp.sqrt(one_minus_a2) * (g[:, w:2 * w] * xconv)
    return a, u


def _rglru_prompt_kernel(z_ref, cw_ref, cb_ref, wg_ref, bg_ref, lam_ref, o_ref, h_ref, ext_scr, h_scr):
    ti = pl.program_id(1)
    w = BRANCH_W
    tl = z_ref.shape[1]
    halo = 8

    @pl.when(ti == 0)
    def _():
        ext_scr[0:halo, :] = jnp.zeros((halo, w), F32)
        h_scr[...] = jnp.zeros_like(h_scr)

    x = z_ref[0, :, 0:w]
    ext_scr[halo:, :] = x
    xconv = cb_ref[...] + cw_ref[CONV_C - 1:CONV_C, :] * x
    for j in range(CONV_C - 1):
        xconv = xconv + cw_ref[j:j + 1, :] * ext_scr[pl.ds(halo - (CONV_C - 1) + j, tl), :]
    ext_scr[0:halo, :] = x[tl - halo:tl]
    a, u = _rglru_gates(xconv, wg_ref[...], bg_ref[...], _softplus(-lam_ref[...]))
    d = 1
    while d < tl:
        u = u + a * _shift_rows(u, d, 0.0)
        a = a * _shift_rows(a, d, 1.0)
        d *= 2
    h = u + a * h_scr[0:1, :]
    h_last = h[tl - 1:tl]
    h_scr[...] = jnp.broadcast_to(h_last, h_scr.shape)
    h_ref[0] = h_last
    o_ref[0] = h * _gelu(z_ref[0, :, w:2 * w])


def _rglru_prompt(zc3, cw, cb, wg, bg, lam):
    b, l, _ = zc3.shape
    w = BRANCH_W
    tl = min(TL_SCAN, l)
    fix = lambda i, j: (0, 0)
    return pl.pallas_call(
        _rglru_prompt_kernel,
        grid=(b, l // tl),
        in_specs=[pl.BlockSpec((1, tl, 2 * w), lambda i, j: (i, j, 0)),
                  pl.BlockSpec((CONV_C, w), fix), pl.BlockSpec((1, w), fix),
                  pl.BlockSpec((w, 2 * w), fix), pl.BlockSpec((1, 2 * w), fix), pl.BlockSpec((1, w), fix)],
        out_specs=[pl.BlockSpec((1, tl, w), lambda i, j: (i, j, 0)),
                   pl.BlockSpec((1, 1, w), lambda i, j: (i, 0, 0))],
        out_shape=[jax.ShapeDtypeStruct((b, l, w), F32), jax.ShapeDtypeStruct((b, 1, w), F32)],
        scratch_shapes=[pltpu.VMEM((tl + 8, w), F32), pltpu.VMEM((8, w), F32)],
        compiler_params=_params("parallel", "arbitrary"),
        name="rglru_prompt",
    )(zc3, cw, cb, wg, bg, lam)


def _pool_select(sums, x, pos, wp, scale):
    lane_group = _head_id((1, BRANCH_W), 1)
    pooled = jnp.zeros_like(x)
    for g, win in enumerate(POOL_WINDOWS):
        cnt = jnp.minimum(pos + 1, win).astype(F32)
        pooled = jnp.where(lane_group == g, sums[g] / cnt, pooled)
    return _dot((pooled - x).astype(BF16), wp) * scale


def _pool_prompt_kernel(x_ref, wp_ref, sc_ref, o_ref, ext_scr):
    ti = pl.program_id(1)
    w = BRANCH_W
    tl = x_ref.shape[1]
    halo = 16

    @pl.when(ti == 0)
    def _():
        ext_scr[0:halo, :] = jnp.zeros((halo, w), F32)

    x = x_ref[0]
    ext_scr[halo:, :] = x
    e = ext_scr[...]
    sums = []
    d = 1
    for _ in POOL_WINDOWS:
        e = e + pltpu.roll(e, d, 0)
        sums.append(e[halo:])
        d *= 2
    ext_scr[0:halo, :] = x[tl - halo:tl]
    pos = ti * tl + lax.broadcasted_iota(jnp.int32, (tl, 1), 0)
    o_ref[0] = _pool_select(sums, x, pos, wp_ref[...], sc_ref[...])


def _pool_prompt(zd3, wp, scale):
    b, l, w = zd3.shape
    tl = min(TL_SCAN, l)
    fix = lambda i, j: (0, 0)
    return pl.pallas_call(
        _pool_prompt_kernel,
        grid=(b, l // tl),
        in_specs=[pl.BlockSpec((1, tl, w), lambda i, j: (i, j, 0)),
                  pl.BlockSpec((w, w), fix), pl.BlockSpec((1, w), fix)],
        out_specs=pl.BlockSpec((1, tl, w), lambda i, j: (i, j, 0)),
        out_shape=jax.ShapeDtypeStruct((b, l, w), F32),
        scratch_shapes=[pltpu.VMEM((tl + 16, w), F32)],
        compiler_params=_params("parallel", "arbitrary"),
        name="pool_prompt",
    )(zd3, wp, scale)


def _cd_sample_kernel(zc_ref, zd_ref, h0_ref, ch_ref, ph_ref, cw_ref, cb_ref, wg_ref, bg_ref, lam_ref,
                      wp_ref, sc_ref, oc_ref, od_ref, h_ref, cn_ref, pn_ref, xc_scr, *, n_new, db, pos0):
    w = BRANCH_W
    slab = lambda ref, t, c0=0: ref[t * db:(t + 1) * db, c0:c0 + w]
    hist = lambda ref, j: ref[j]

    n_hist = CONV_C - 1
    ext = [hist(ch_ref, j) for j in range(n_hist)] + [slab(zc_ref, t) for t in range(n_new)]
    for t in range(n_new):
        acc = cb_ref[...] + cw_ref[0:1, :] * ext[t]
        for j in range(1, CONV_C):
            acc = acc + cw_ref[j:j + 1, :] * ext[t + j]
        xc_scr[t * db:(t + 1) * db, :] = acc
    a, u = _rglru_gates(xc_scr[...], wg_ref[...], bg_ref[...], _softplus(-lam_ref[...]))
    h = h0_ref[...]
    for t in range(n_new):
        h = a[t * db:(t + 1) * db] * h + u[t * db:(t + 1) * db]
        oc_ref[t * db:(t + 1) * db, :] = h * _gelu(slab(zc_ref, t, w))
    h_ref[...] = h
    for j in range(n_hist):
        cn_ref[j] = ext[n_new + j]

    pext = [hist(ph_ref, j) for j in range(POOL_HIST)] + [slab(zd_ref, t) for t in range(n_new)]
    for t in range(n_new):
        sums, run, k = [], None, 0
        for win in POOL_WINDOWS:
            while k < win:
                term = pext[POOL_HIST + t - k]
                run = term if run is None else run + term
                k += 1
            sums.append(run)
        pos = jnp.full((db, 1), pos0 + t, jnp.int32)
        od_ref[t * db:(t + 1) * db, :] = _pool_select(sums, pext[POOL_HIST + t], pos, wp_ref[...], sc_ref[...])
    for j in range(POOL_HIST):
        pn_ref[j] = pext[n_new + j]


def _cd_sample(zc, zd, h0, conv_hist, pool_hist, cw, cb, wg, bg, lam, wp, scale, n_new, pos0):
    t, _ = zc.shape
    db = t // n_new
    w = BRANCH_W
    shp = lambda c: jax.ShapeDtypeStruct((db, c), F32)
    return pl.pallas_call(
        functools.partial(_cd_sample_kernel, n_new=n_new, db=db, pos0=pos0),
        out_shape=[jax.ShapeDtypeStruct((t, w), F32), jax.ShapeDtypeStruct((t, w), F32),
                   shp(w), jax.ShapeDtypeStruct((CONV_C - 1, db, w), F32),
                   jax.ShapeDtypeStruct((POOL_HIST, db, w), F32)],
        scratch_shapes=[pltpu.VMEM((t, w), F32)],
        compiler_params=pltpu.CompilerParams(vmem_limit_bytes=V7X_VMEM_LIMIT_BYTES),
        name="rglru_pool_sample",
    )(zc, zd, h0, conv_hist, pool_hist, cw, cb, wg, bg, lam, wp, scale)


def _merge_kernel(x_ref, oa_ref, ob_ref, oc_ref, od_ref, g_ref, wg_ref, wb_ref, wo_ref, o_ref):
    x = x_ref[...]
    d = x.shape[1]
    h = _rms(x, g_ref[...]).astype(BF16)
    mix = jnp.zeros(x.shape, F32)
    for n, br in enumerate((oa_ref, ob_ref, oc_ref, od_ref)):
        gate = _sigmoid(_dot(h, wg_ref[:, n * d:(n + 1) * d]))
        mix = mix + gate * _dot(br[...].astype(BF16), wb_ref[n])
    o_ref[...] = x + _dot(mix.astype(BF16), wo_ref[...])


def _merge(layer, x2, oa, ob, oc, od, g, wg, wb, wo):
    t, d = x2.shape
    tm = min(TM_PROJ, t)
    w = BRANCH_W
    row = lambda i: (i, 0)
    fix = lambda i: (0, 0)
    return pl.pallas_call(
        _merge_kernel,
        grid=(t // tm,),
        in_specs=[pl.BlockSpec((tm, d), row)] + [pl.BlockSpec((tm, w), row)] * 4
                 + [pl.BlockSpec((1, d), fix), pl.BlockSpec((d, N_BRANCH * d), fix),
                    pl.BlockSpec((None, N_BRANCH, w, d), lambda i: (layer, 0, 0, 0)),
                    pl.BlockSpec((None, d, d), lambda i: (layer, 0, 0))],
        out_specs=pl.BlockSpec((tm, d), row),
        out_shape=jax.ShapeDtypeStruct((t, d), F32),
        compiler_params=_params("parallel"),
        name="merge",
    )(x2, oa, ob, oc, od, g, wg, wb, wo)


def _ple(x, pe, gp, wpg, wp):
    gate = _sigmoid(_dot(_rms(x, gp).astype(BF16), wpg))
    return x + gate * _dot(pe.astype(BF16), wp)


def _ffn_prompt_kernel(x_ref, xp_ref, g_ref, wa_ref, wb_ref, cwa_ref, cwb_ref, cba_ref, cbb_ref, wd_ref,
                       pe_ref, gp_ref, wpg_ref, wp_ref, o_ref, sa_ref, sb_ref,
                       xn_scr, xnp_scr, acc_scr, ea_scr, eb_scr, *, tiles_per_seq):
    i, f = pl.program_id(0), pl.program_id(1)
    tm = x_ref.shape[0]
    halo = FFN_HALO

    @pl.when(f == 0)
    def _():
        xn_scr[...] = _rms(x_ref[...], g_ref[...]).astype(BF16)
        xnp_scr[...] = _rms(xp_ref[...], g_ref[...]).astype(BF16)
        acc_scr[...] = jnp.zeros_like(acc_scr)

    seq_start = (i % tiles_per_seq) == 0

    ts = min(FFN_STRIP, tm)
    strips = [slice(r0, r0 + ts) for r0 in range(0, tm, ts)]

    halves = ((wa_ref, ea_scr, sa_ref), (wb_ref, eb_scr, sb_ref))

    def conv(cw_ref, cb_ref, e_scr, rows):
        y = cb_ref[...]
        for j in range(CONV_F):
            y = y + cw_ref[j:j + 1, :] * e_scr[pl.ds(halo - (CONV_F - 1) + j + rows.start, ts), :]
        return y

    for w_ref, e_scr, s_ref in halves:
        e_scr[0:halo, :] = jnp.where(seq_start, 0.0, _dot(xnp_scr[...], w_ref[...]))
        for rows in strips:
            e_scr[halo + rows.start:halo + rows.stop, :] = _dot(xn_scr[rows, :], w_ref[...])
        s_ref[0] = e_scr[pl.ds(halo + tm - (CONV_F - 1), CONV_F - 1), :]
    for rows in strips:
        gated = _gelu(conv(cwa_ref, cba_ref, ea_scr, rows)) * conv(cwb_ref, cbb_ref, eb_scr, rows)
        acc_scr[rows, :] += _dot(gated.astype(BF16), wd_ref[...])

    @pl.when(f == pl.num_programs(1) - 1)
    def _():
        o_ref[...] = _ple(x_ref[...] + acc_scr[...], pe_ref[...], gp_ref[...], wpg_ref[...], wp_ref[...])


def _ffn_prompt(layer, x2, seq_len, g, w_up, cw, cb, w_down, pe3, gp, wpg, wp):
    t, d = x2.shape
    dff = w_down.shape[1]
    tm = min(TM_FFN, seq_len)
    tf = TF_FFN
    halo = FFN_HALO
    assert seq_len % tm == 0 and dff % tf == 0 and tm % halo == 0
    nf = dff // tf
    tps = seq_len // tm
    dp = pe3.shape[2]
    fix = lambda i, f: (0, 0)
    st_shape = jax.ShapeDtypeStruct((t // tm, CONV_F - 1, dff), F32)
    st_spec = pl.BlockSpec((1, CONV_F - 1, tf), lambda i, f: (i, 0, f))
    return pl.pallas_call(
        functools.partial(_ffn_prompt_kernel, tiles_per_seq=tps),
        grid=(t // tm, nf),
        in_specs=[pl.BlockSpec((tm, d), lambda i, f: (i, 0)),
                  pl.BlockSpec((halo, d), lambda i, f: (jnp.maximum(i * (tm // halo) - 1, 0), 0)),
                  pl.BlockSpec((1, d), fix),
                  pl.BlockSpec((None, d, tf), lambda i, f: (layer, 0, f)),
                  pl.BlockSpec((None, d, tf), lambda i, f: (layer, 0, f + nf)),
                  pl.BlockSpec((CONV_F, tf), lambda i, f: (0, f)),
                  pl.BlockSpec((CONV_F, tf), lambda i, f: (0, f + nf)),
                  pl.BlockSpec((1, tf), lambda i, f: (0, f)),
                  pl.BlockSpec((1, tf), lambda i, f: (0, f + nf)),
                  pl.BlockSpec((None, tf, d), lambda i, f: (layer, f, 0)),
                  pl.BlockSpec((None, tm, dp), lambda i, f: (layer, i, 0)),
                  pl.BlockSpec((1, d), fix), pl.BlockSpec((None, d, d), lambda i, f: (layer, 0, 0)),
                  pl.BlockSpec((None, dp, d), lambda i, f: (layer, 0, 0))],
        out_specs=[pl.BlockSpec((tm, d), lambda i, f: (i, 0)), st_spec, st_spec],
        out_shape=[jax.ShapeDtypeStruct((t, d), F32), st_shape, st_shape],
        scratch_shapes=[pltpu.VMEM((tm, d), BF16), pltpu.VMEM((halo, d), BF16), pltpu.VMEM((tm, d), F32),
                        pltpu.VMEM((tm + halo, tf), F32), pltpu.VMEM((tm + halo, tf), F32)],
        compiler_params=_params("arbitrary", "arbitrary"),
        name="ffn_prompt",
    )(x2, x2, g, w_up, w_up, cw, cw, cb, cb, w_down, pe3, gp, wpg, wp)


def _ffn_sample_kernel(x_ref, g_ref, wa_ref, wb_ref, cwa_ref, cwb_ref, cba_ref, cbb_ref, wd_ref,
                       ha_ref, hb_ref, pe_ref, gp_ref, wpg_ref, wp_ref,
                       o_ref, sa_ref, sb_ref, xn_scr, acc_scr, g_scr, *, n_new, db):
    f = pl.program_id(0)

    @pl.when(f == 0)
    def _():
        xn_scr[...] = _rms(x_ref[...], g_ref[...]).astype(BF16)
        acc_scr[...] = jnp.zeros_like(acc_scr)

    def conv_half(w_ref, cw_ref, cb_ref, h_ref, s_ref):
        u = _dot(xn_scr[...], w_ref[...])
        ext = [h_ref[:, j, :] for j in range(CONV_F - 1)] + [u[t * db:(t + 1) * db] for t in range(n_new)]
        for j in range(CONV_F - 1):
            s_ref[:, j, :] = ext[n_new + j]
        ys = []
        for t in range(n_new):
            y = cb_ref[...] + cw_ref[0:1, :] * ext[t]
            for j in range(1, CONV_F):
                y = y + cw_ref[j:j + 1, :] * ext[t + j]
            ys.append(y)
        return ys

    ya = conv_half(wa_ref, cwa_ref, cba_ref, ha_ref, sa_ref)
    yb = conv_half(wb_ref, cwb_ref, cbb_ref, hb_ref, sb_ref)
    for t in range(n_new):
        g_scr[t * db:(t + 1) * db, :] = (_gelu(ya[t]) * yb[t]).astype(BF16)
    acc_scr[...] += _dot(g_scr[...], wd_ref[...])

    @pl.when(f == pl.num_programs(0) - 1)
    def _():
        o_ref[...] = _ple(x_ref[...] + acc_scr[...], pe_ref[...], gp_ref[...], wpg_ref[...], wp_ref[...])


def _ffn_sample(layer, x2, g, w_up, cw, cb, w_down, hist4, pe3, gp, wpg, wp, n_new):
    t, d = x2.shape
    db = t // n_new
    dff = w_down.shape[1]
    tf = TF_FFN_SAMPLE
    nf = dff // tf
    dp = pe3.shape[2]
    nh = CONV_F - 1
    fix = lambda f: (0, 0)
    hist = lambda half: pl.BlockSpec((None, db, nh, tf), lambda f: (layer, 0, 0, f + half * nf))
    st_shape = jax.ShapeDtypeStruct((db, nh, dff), F32)
    st_spec = pl.BlockSpec((db, nh, tf), lambda f: (0, 0, f))
    return pl.pallas_call(
        functools.partial(_ffn_sample_kernel, n_new=n_new, db=db),
        grid=(nf,),
        in_specs=[pl.BlockSpec((t, d), fix), pl.BlockSpec((1, d), fix),
                  pl.BlockSpec((None, d, tf), lambda f: (layer, 0, f)),
                  pl.BlockSpec((None, d, tf), lambda f: (layer, 0, f + nf)),
                  pl.BlockSpec((CONV_F, tf), lambda f: (0, f)), pl.BlockSpec((CONV_F, tf), lambda f: (0, f + nf)),
                  pl.BlockSpec((1, tf), lambda f: (0, f)), pl.BlockSpec((1, tf), lambda f: (0, f + nf)),
                  pl.BlockSpec((None, tf, d), lambda f: (layer, f, 0)),
                  hist(0), hist(1),
                  pl.BlockSpec((None, t, dp), lambda f: (layer, 0, 0)),
                  pl.BlockSpec((1, d), fix), pl.BlockSpec((None, d, d), lambda f: (layer, 0, 0)),
                  pl.BlockSpec((None, dp, d), lambda f: (layer, 0, 0))],
        out_specs=[pl.BlockSpec((t, d), fix), st_spec, st_spec],
        out_shape=[jax.ShapeDtypeStruct((t, d), F32), st_shape, st_shape],
        scratch_shapes=[pltpu.VMEM((t, d), BF16), pltpu.VMEM((t, d), F32), pltpu.VMEM((t, tf), BF16)],
        compiler_params=_params("arbitrary"),
        name="ffn_sample",
    )(x2, g, w_up, w_up, cw, cw, cb, cb, w_down, hist4, hist4, pe3, gp, wpg, wp)


def _block_diag(w4):
    h, n, _ = w4.shape
    eye = jnp.eye(h, dtype=w4.dtype)
    return (eye[:, None, :, None] * w4[:, :, None, :]).reshape(h * n, h * n)


def kernel(x_prompt, x_sample, cache_k, cache_v, state_hgrn, state_rglru_h, state_rglru_conv, state_pool,
           state_ffn_conv, page_table, p_prompt, p_sample, norm_mix, w_in, q_norm, k_norm, sb_bias, lb_logits,
           hgrn_norm, conv_c_w, conv_c_b, w_rg_a, b_rg_a, w_rg_x, b_rg_x, lam, w_pool, pool_scale, w_branch,
           w_out, norm_ffn, w_up, conv_f_w, conv_f_b, w_down, norm_ple, w_ple_gate, w_ple):
    depth = w_in.shape[0]
    b, l, d = x_prompt.shape
    db, n_new, _ = x_sample.shape
    w = BRANCH_W
    n_mix = 10 * w
    dff = w_down.shape[1]
    dp = p_prompt.shape[-1]
    n_pool, page = cache_k.shape[1], cache_k.shape[2]
    pos0 = page_table.shape[1] * page

    lb_all, loglb_all, log1m_all = _lower_bounds(lb_logits.astype(F32))
    row = lambda a, i: a[i].reshape(1, -1)
    tile_heads = lambda a, i: jnp.tile(a[i], N_HEAD).reshape(1, w)

    xp = x_prompt.reshape(b * l, d)
    xs = x_sample.transpose(1, 0, 2).reshape(n_new * db, d)
    pe_p = p_prompt.reshape(depth, b * l, dp)
    pe_s = p_sample.transpose(0, 2, 1, 3).reshape(depth, n_new * db, dp)
    cache_kt = cache_k.transpose(0, 1, 3, 4, 2)
    cache_vt = cache_v.transpose(0, 1, 3, 4, 2)
    hgrn_state = state_hgrn.transpose(0, 2, 3, 4, 1).reshape(depth, -1, db)
    conv_state = state_rglru_conv.transpose(0, 2, 1, 3)
    pool_state = state_pool.transpose(0, 2, 1, 3)

    w_br, w_o = w_branch.astype(BF16), w_out.astype(BF16)
    w_u, w_d = w_up.astype(BF16), w_down.astype(BF16)
    w_pg, w_pe = w_ple_gate.astype(BF16), w_ple.astype(BF16)

    outs = {k: [] for k in ("kp", "vp", "ks", "vs", "sp", "ss", "hp", "hs", "cp", "cs", "pp", "ps", "fp", "fs")}
    for i in range(depth):
        w_mix = w_in[i, :, :n_mix].astype(BF16)
        w_gate = w_in[i, :, n_mix:].astype(BF16)
        wg_c = jnp.concatenate([_block_diag(w_rg_a[i]), _block_diag(w_rg_x[i])], axis=1).astype(BF16)
        bg_c = jnp.concatenate([b_rg_a[i], b_rg_x[i]]).reshape(1, 2 * w)
        wp_d = _block_diag(w_pool[i]).astype(BF16)
        g_mix, g_ffn, g_ple = row(norm_mix, i), row(norm_ffn, i), row(norm_ple, i)
        qn, kn, gn = tile_heads(q_norm, i), tile_heads(k_norm, i), tile_heads(hgrn_norm, i)
        lb, loglb, log1m = row(lb_all, i), row(loglb_all, i), row(log1m_all, i)
        cw_c, cb_c, lam_i, sc_d = conv_c_w[i], row(conv_c_b, i), row(lam, i), row(pool_scale, i)
        cw_f, cb_f = conv_f_w[i], row(conv_f_b, i)
        bias = sb_bias[i].astype(F32)

        q, kt, vt, ktb, vtb, zb, zc, zd = _inproj_prompt(xp.reshape(b, l, d), g_mix, w_mix, qn, kn)
        o_a = _attn_prompt(bias, q, ktb, vtb)
        o_b, s_p = _hgrn_prompt(zb, lb, loglb, log1m, gn)
        o_c, h_p = _rglru_prompt(zc, cw_c, cb_c, wg_c, bg_c, lam_i)
        o_d = _pool_prompt(zd, wp_d, sc_d)
        f2 = lambda a: a.reshape(b * l, w)
        x1 = _merge(i, xp, f2(o_a), f2(o_b), f2(o_c), f2(o_d), g_mix, w_gate, w_br, w_o)
        xp, fa, fb = _ffn_prompt(i, x1, l, g_ffn, w_u, cw_f, cb_f, w_d, pe_p, g_ple, w_pg, w_pe)
        outs["kp"].append(kt)
        outs["vp"].append(vt)
        s_heads = s_p.reshape(b, N_HEAD, HEAD_W, N_HEAD, HEAD_W)
        outs["sp"].append(jnp.stack([s_heads[:, h, :, h, :] for h in range(N_HEAD)], axis=1))
        outs["hp"].append(h_p.reshape(b, w))
        outs["cp"].append(zc[:, l - (CONV_C - 1):, :w])
        outs["pp"].append(zd[:, l - POOL_HIST:, :])
        tps = fa.shape[0] // b
        outs["fp"].append(jnp.concatenate([fa[tps - 1::tps], fb[tps - 1::tps]], axis=-1))

        q, k, v, zb, zc, zd = _inproj_sample(xs, g_mix, w_mix, qn, kn)
        r4 = lambda a: a.reshape(n_new, db, 1, w)
        o_a = _attn_sample(i, page_table, bias, r4(q), r4(k), r4(v), cache_kt, cache_vt).reshape(n_new * db, w)
        o_b, s_s = _hgrn_sample(i, zb, hgrn_state, lb, gn, n_new)
        o_c, o_d, h_s, c_s, p_s = _cd_sample(zc, zd, state_rglru_h[i], conv_state[i], pool_state[i],
                                             cw_c, cb_c, wg_c, bg_c, lam_i, wp_d, sc_d, n_new, pos0)
        x1 = _merge(i, xs, o_a, o_b, o_c, o_d, g_mix, w_gate, w_br, w_o)
        xs, fa, fb = _ffn_sample(i, x1, g_ffn, w_u, cw_f, cb_f, w_d, state_ffn_conv, pe_s,
                                 g_ple, w_pg, w_pe, n_new)
        outs["ks"].append(k)
        outs["vs"].append(v)
        outs["ss"].append(s_s)
        outs["hs"].append(h_s)
        outs["cs"].append(c_s)
        outs["ps"].append(p_s)
        outs["fs"].append(jnp.concatenate([fa, fb], axis=-1))

    stk = lambda key: jnp.stack(outs[key], axis=0)
    to_batch_major = lambda a: a.reshape(depth, n_new, db, N_HEAD, HEAD_W).transpose(0, 2, 1, 3, 4)
    y_sample = xs.reshape(n_new, db, d).transpose(1, 0, 2)
    from_transposed = lambda a: a.reshape(depth, b, N_HEAD, HEAD_W, l).transpose(0, 1, 4, 2, 3)
    hgrn_s = stk("ss").reshape(depth, N_HEAD, HEAD_W, HEAD_W, db).transpose(0, 4, 1, 2, 3)
    rows_to_batch = lambda a: a.transpose(0, 2, 1, 3)
    return (xp.reshape(b, l, d), y_sample,
            from_transposed(stk("kp")), from_transposed(stk("vp")),
            to_batch_major(stk("ks")), to_batch_major(stk("vs")),
            stk("sp"), hgrn_s, stk("hp"), stk("hs"), stk("cp"), rows_to_batch(stk("cs")),
            stk("pp"), rows_to_batch(stk("ps")), stk("fp"), stk("fs"))
```

```python
import functools
import math

import jax
import jax.numpy as jnp
from jax import lax
from jax.experimental import pallas as pl
from jax.experimental.pallas import tpu as pltpu

F32 = jnp.float32
BF16 = jnp.bfloat16
EPS = 1e-6

N_HEAD = 4
HEAD_W = 64
BRANCH_W = N_HEAD * HEAD_W
N_BRANCH = 4
RG_C = 8.0
POOL_WINDOWS = (2, 4, 8, 16)
POOL_HIST = max(POOL_WINDOWS) - 1
CONV_C = 4
CONV_F = 3
HGRN_CHUNK = 64
HGRN_SUB = 16
NEG_BIG = -1e30
LOG2E = math.log2(math.e)

V7X_VMEM_LIMIT_BYTES = 56 * 1024 * 1024

TM_PROJ = 512
TQ_ATTN = 256
TK_ATTN = 256
TL_SCAN = 512
TL_HGRN = 256
TM_FFN = 512
TF_FFN = 256
FFN_HALO = 16
FFN_DOWN_GROUP = 4
SAMPLES_PER_STEP = 2


def _params(*sem):
    return pltpu.CompilerParams(dimension_semantics=sem, vmem_limit_bytes=V7X_VMEM_LIMIT_BYTES)


def _dot(a, b):
    return jnp.dot(a, b, preferred_element_type=F32)


def _dot_nt(a, b):
    return lax.dot_general(a, b, (((1,), (1,)), ((), ())), preferred_element_type=F32)


def _dot_tn(a, b):
    return lax.dot_general(a, b, (((0,), (0,)), ((), ())), preferred_element_type=F32)


def _split2(x):
    hi = x.astype(BF16)
    lo = (x - hi.astype(F32)).astype(BF16)
    return hi, lo


def _split3(x):
    hi = x.astype(BF16)
    r = x - hi.astype(F32)
    mid = r.astype(BF16)
    lo = (r - mid.astype(F32)).astype(BF16)
    return hi, mid, lo


def _dot_x2(x, m):
    hi, lo = _split2(x)
    return _dot(hi, m) + _dot(lo, m)


def _sigmoid(x):
    return 1.0 / (1.0 + jnp.exp(-x))


def _softplus_tail(x):
    return jnp.log1p(jnp.exp(-jnp.abs(x)))


def _log_sigmoid(x):
    return jnp.minimum(x, 0.0) - _softplus_tail(x)


def _softplus(x):
    return jnp.maximum(x, 0.0) + _softplus_tail(x)


def _silu(x):
    return x * _sigmoid(x)


def _gelu(x):
    c = math.sqrt(2.0 / math.pi)
    return 0.5 * x * (1.0 + jnp.tanh(c * (x + 0.044715 * (x * x * x))))


def _rms(x, g):
    ms = jnp.mean(x * x, axis=-1, keepdims=True)
    return x * lax.rsqrt(ms + EPS) * g


def _head_id(shape, dim):
    return lax.shift_right_logical(lax.broadcasted_iota(jnp.int32, shape, dim), 6)


def _head_block_ones():
    n = BRANCH_W
    return jnp.where(_head_id((n, n), 0) == _head_id((n, n), 1), 1.0, 0.0).astype(BF16)


def _head_rms(a, g, bd):
    ms = _dot_x2(a * a, bd) * (1.0 / HEAD_W)
    return a * lax.rsqrt(ms + EPS) * g


def _shift_rows(x, d, fill):
    rolled = pltpu.roll(x, d, 0)
    row = lax.broadcasted_iota(jnp.int32, x.shape, 0)
    return jnp.where(row < d, fill, rolled)


def _lb_kernel(lg_ref, lb_ref, loglb_ref, log1m_ref):
    x = lg_ref[...]
    depth = x.shape[0]
    rows = [x[i:i + 1] for i in range(depth)]
    m = functools.reduce(jnp.maximum, rows)
    e = [jnp.exp(r - m) for r in rows]
    tot = functools.reduce(lambda a, b: a + b, e)
    zero = jnp.zeros_like(m)
    lb_ref[0:1, :] = zero
    loglb_ref[0:1, :] = jnp.full_like(m, -jnp.inf)
    log1m_ref[0:1, :] = zero
    acc = zero
    for i in range(1, depth):
        acc = acc + e[i] / tot
        lb_ref[i:i + 1, :] = acc
        loglb_ref[i:i + 1, :] = jnp.log(acc)
        log1m_ref[i:i + 1, :] = jnp.log1p(-acc)


def _lower_bounds(lb_logits):
    shp = jax.ShapeDtypeStruct(lb_logits.shape, F32)
    return pl.pallas_call(_lb_kernel, out_shape=(shp, shp, shp), name="hgrn_lower_bounds")(lb_logits)


def _inproj_common(x, g_ref, w_ref, qn_ref, kn_ref):
    h = _rms(x, g_ref[...]).astype(BF16)
    z = _dot(h, w_ref[...])
    bd = _head_block_ones()
    w = BRANCH_W
    q = _head_rms(z[:, 0:w], qn_ref[...], bd) * (HEAD_W ** -0.5 * LOG2E)
    k = _head_rms(z[:, w:2 * w], kn_ref[...], bd)
    return q, k, z[:, 2 * w:3 * w], z[:, 3 * w:7 * w], z[:, 7 * w:9 * w], z[:, 9 * w:10 * w]


def _inproj_prompt_kernel(x_ref, g_ref, w_ref, qn_ref, kn_ref,
                          q_o, kt_o, vt_o, ktb_o, vtb_o, zb_o, zc_o, zd_o):
    q, k, v, zb, zc, zd = _inproj_common(x_ref[0], g_ref, w_ref, qn_ref, kn_ref)
    q_o[0] = q.astype(BF16)
    kt, vt = k.T, v.T
    kt_o[0] = kt
    vt_o[0] = vt
    tk = ktb_o.shape[2]
    for c in range(ktb_o.shape[0]):
        ktb_o[c] = kt[:, c * tk:(c + 1) * tk].astype(BF16)
        vtb_o[c] = vt[:, c * tk:(c + 1) * tk].astype(BF16)
    zb_o[0] = zb
    zc_o[0] = zc
    zd_o[0] = zd


def _inproj_prompt(x3, g, w, qn, kn):
    b, l, d = x3.shape
    tm = min(TM_PROJ, l)
    tk = min(TK_ATTN, l)
    n = w.shape[1]
    bw = BRANCH_W
    per = tm // tk
    row = lambda i, j: (i, j, 0)
    fix = lambda i, j: (0, 0)
    col = lambda i, j: (i, 0, j)
    tiles = lambda i, j: (i * (l // tm) + j, 0, 0)
    tok = lambda c, dt: (jax.ShapeDtypeStruct((b, l, c), dt), pl.BlockSpec((1, tm, c), row))
    tr = (jax.ShapeDtypeStruct((b, bw, l), F32), pl.BlockSpec((1, bw, tm), col))
    trb = (jax.ShapeDtypeStruct((b * l // tk, bw, tk), BF16), pl.BlockSpec((per, bw, tk), tiles))
    outs = [tok(bw, BF16), tr, tr, trb, trb, tok(4 * bw, F32), tok(2 * bw, F32), tok(bw, F32)]
    return pl.pallas_call(
        _inproj_prompt_kernel,
        grid=(b, l // tm),
        in_specs=[pl.BlockSpec((1, tm, d), row), pl.BlockSpec((1, d), fix), pl.BlockSpec((d, n), fix),
                  pl.BlockSpec((1, bw), fix), pl.BlockSpec((1, bw), fix)],
        out_specs=[s for _, s in outs],
        out_shape=[s for s, _ in outs],
        compiler_params=_params("parallel", "parallel"),
        name="inproj_prompt",
    )(x3, g, w, qn, kn)


def _inproj_sample_kernel(x_ref, g_ref, w_ref, qn_ref, kn_ref, q_o, k_o, v_o, zb_o, zc_o, zd_o):
    for ref, val in zip((q_o, k_o, v_o, zb_o, zc_o, zd_o),
                        _inproj_common(x_ref[...], g_ref, w_ref, qn_ref, kn_ref)):
        ref[...] = val


def _inproj_sample(x2, g, w, qn, kn):
    t, d = x2.shape
    bw = BRANCH_W
    widths = (bw, bw, bw, 4 * bw, 2 * bw, bw)
    return pl.pallas_call(
        _inproj_sample_kernel,
        out_shape=[jax.ShapeDtypeStruct((t, c), F32) for c in widths],
        compiler_params=pltpu.CompilerParams(vmem_limit_bytes=V7X_VMEM_LIMIT_BYTES),
        name="inproj_sample",
    )(x2, g, w, qn, kn)


def _upper_ones(n):
    r = lax.broadcasted_iota(jnp.int32, (n, n), 0)
    c = lax.broadcasted_iota(jnp.int32, (n, n), 1)
    return jnp.where(r > c, 1.0, 0.0).astype(BF16)


def _sb_logs(z, mask):
    m = jnp.maximum(z, 0.0)
    n = z - m
    t = jnp.log2(1.0 + jnp.exp2(n - m))
    drop = m + t
    return n - t, (drop if mask is None else jnp.where(mask, drop, 0.0))


def _attn_prompt_kernel(bias_ref, q_ref, kt_ref, vt_ref, o_ref):
    qi = pl.program_id(1)
    tq, tk = q_ref.shape[1], kt_ref.shape[2]
    per = tq // tk
    q = q_ref[0]
    heads = range(N_HEAD)
    lanes = lambda h: slice(h * HEAD_W, (h + 1) * HEAD_W)
    qs = [q[:, lanes(h)] for h in heads]
    bias = [bias_ref[h] * LOG2E for h in heads]
    upper = _upper_ones(tk)
    q_pos = qi * tq + lax.broadcasted_iota(jnp.int32, (tq, tk), 0)
    k_off = lax.broadcasted_iota(jnp.int32, (tq, tk), 1)

    def logits(kidx):
        return [_dot(qs[h], kt_ref[kidx, lanes(h), :]) + bias[h] for h in heads]

    def sums(zs, mask):
        logs = [_sb_logs(z, mask) for z in zs]
        local = [_dot(drop.astype(BF16), upper) for _, drop in logs]
        return ([lb for lb, _ in logs], local, [local[h][:, 0:1] + logs[h][1][:, 0:1] for h in heads])

    def values(kidx, log_beta, local, carries, mask):
        pvs = []
        for h in heads:
            w = jnp.exp2(log_beta[h] - (local[h] + carries[h]))
            if mask is not None:
                w = jnp.where(mask, w, 0.0)
            pvs.append(_dot_nt(w.astype(BF16), vt_ref[kidx, lanes(h), :]))
        return pvs

    add = lambda xs, ys: [x + y for x, y in zip(xs, ys)]
    first = qi * per
    accs = [jnp.zeros((tq, HEAD_W), F32)] * N_HEAD
    carries = [jnp.zeros((tq, 1), F32)] * N_HEAD
    for m in range(per - 1, -1, -1):
        kidx = first + m
        mask = kidx * tk + k_off < q_pos
        log_beta, local, totals = sums(logits(kidx), mask)
        accs = add(accs, values(kidx, log_beta, local, carries, mask))
        carries = add(carries, totals)

    def body(j, state):
        accs, carries = state
        kidx = first - 1 - j
        log_beta, local, totals = sums(logits(kidx), None)
        return add(accs, values(kidx, log_beta, local, carries, None)), add(carries, totals)

    accs = lax.fori_loop(0, first, body, (accs, carries))[0]
    for h in heads:
        o_ref[0, :, lanes(h)] = accs[h]


def _attn_prompt(bias, q3, ktb, vtb):
    b, l, w = q3.shape
    tk = ktb.shape[2]
    tq = min(TQ_ATTN, l)
    nk = l // tk
    return pl.pallas_call(
        _attn_prompt_kernel,
        grid=(b, l // tq),
        in_specs=[pl.BlockSpec(memory_space=pltpu.SMEM),
                  pl.BlockSpec((1, tq, w), lambda i, j: (i, j, 0)),
                  pl.BlockSpec((nk, w, tk), lambda i, j: (i, 0, 0)),
                  pl.BlockSpec((nk, w, tk), lambda i, j: (i, 0, 0))],
        out_specs=pl.BlockSpec((1, tq, w), lambda i, j: (i, j, 0)),
        out_shape=jax.ShapeDtypeStruct((b, l, w), F32),
        compiler_params=_params("parallel", "arbitrary"),
        name="attn_prompt",
    )(bias, q3, ktb, vtb)


def _attn_sample_kernel(pt_ref, bias_ref, q_ref, kn_ref, vn_ref, *refs, n_pages, n_new, n_samp):
    del pt_ref
    o_ref = refs[-1]
    samples = range(n_samp)
    k_refs = [refs[s * n_pages:(s + 1) * n_pages] for s in samples]
    v_refs = [refs[(n_samp + s) * n_pages:(n_samp + s + 1) * n_pages] for s in samples]
    page = refs[0].shape[-1]
    flat = lambda ref: ref[...].reshape(BRANCH_W, page).astype(BF16)
    rows = N_HEAD * 8
    row = lax.broadcasted_iota(jnp.int32, (rows, 1), 0)
    row_t = jnp.bitwise_and(row, 7)
    row_h = lax.shift_right_logical(row, 3)
    lane_head = _head_id((1, BRANCH_W), 1)
    bias = jnp.zeros((rows, 1), F32)
    for h in range(N_HEAD):
        bias = jnp.where(row_h == h, bias_ref[h] * LOG2E, bias)

    qs, carry, out = [], [], []
    for s in samples:
        q = jnp.zeros((rows, BRANCH_W), F32)
        for t in range(n_new):
            q = q + jnp.where(row_t == t, q_ref[t, s], 0.0)
        q = jnp.where(row_h == lane_head, q, 0.0)
        c = jnp.zeros((rows, 1), F32)
        o = jnp.zeros((rows, BRANCH_W), F32)
        for j in reversed(range(n_new)):
            seen = row_t > j
            log_beta, drop = _sb_logs(jnp.sum(q * kn_ref[j, s], axis=-1, keepdims=True) + bias, seen)
            o = o + jnp.where(seen, jnp.exp2(log_beta - c), 0.0) * vn_ref[j, s]
            c = c + drop
        qs.append(q)
        carry.append(c)
        out.append(o)

    cols = lambda a, p: a[:, p * page:(p + 1) * page]
    kt = [jnp.concatenate([flat(r) for r in k_refs[s]], axis=1) for s in samples]
    logs = [_sb_logs(_dot(qs[s].astype(BF16), kt[s]) + bias, None) for s in samples]
    stacked = [jnp.concatenate([cols(drop, p) for p in range(n_pages)], axis=0).astype(BF16) for _, drop in logs]
    upper = _upper_ones(page)
    local = [_dot(st, upper) for st in stacked]
    ws = []
    for s in samples:
        later, c = [None] * n_pages, carry[s]
        for p in reversed(range(n_pages)):
            loc = local[s][p * rows:(p + 1) * rows]
            later[p] = loc + c
            c = c + loc[:, 0:1] + cols(logs[s][1], p)[:, 0:1]
        ws.append(jnp.exp2(logs[s][0] - jnp.concatenate(later, axis=1)).astype(BF16))
    vt = [jnp.concatenate([flat(r) for r in v_refs[s]], axis=1) for s in samples]
    out = [out[s] + _dot_nt(ws[s], vt[s]) for s in samples]

    for s in samples:
        res = jnp.zeros((8, BRANCH_W), F32)
        for h in range(N_HEAD):
            res = res + jnp.where(lane_head == h, out[s][h * 8:(h + 1) * 8], 0.0)
        for t in range(n_new):
            o_ref[t, s] = res[t:t + 1]


def _attn_sample(layer, page_table, bias, q4, kn4, vn4, cache_kt, cache_vt):
    n_new, db, _, w = q4.shape
    n_pages = page_table.shape[1]
    page = cache_kt.shape[4]
    n_samp = SAMPLES_PER_STEP
    assert db % n_samp == 0
    new_spec = pl.BlockSpec((n_new, n_samp, 1, w), lambda g, pt, bs: (0, g, 0, 0))

    def page_spec(s, p):
        return pl.BlockSpec((None, None, N_HEAD, HEAD_W, page),
                            lambda g, pt, bs: (layer, pt[g * n_samp + s, p], 0, 0, 0))

    page_specs = [page_spec(s, p) for s in range(n_samp) for p in range(n_pages)]
    grid_spec = pltpu.PrefetchScalarGridSpec(
        num_scalar_prefetch=2,
        grid=(db // n_samp,),
        in_specs=[new_spec, new_spec, new_spec] + page_specs * 2,
        out_specs=new_spec,
    )
    n_ops = n_samp * n_pages
    return pl.pallas_call(
        functools.partial(_attn_sample_kernel, n_pages=n_pages, n_new=n_new, n_samp=n_samp),
        grid_spec=grid_spec,
        out_shape=jax.ShapeDtypeStruct(q4.shape, F32),
        compiler_params=_params("arbitrary"),
        name="attn_sample",
    )(page_table, bias, q4, kn4, vn4, *([cache_kt] * n_ops), *([cache_vt] * n_ops))


def _hgrn_gates(qb, fr, lb, loglb, log1m):
    c = log1m + _log_sigmoid(fr)
    log_f = jnp.maximum(loglb, c) + jnp.log1p(jnp.exp(-jnp.abs(loglb - c)))
    key = (1.0 - lb) * _sigmoid(-fr)
    return _silu(qb), key, log_f


def _hgrn_prompt_kernel(z_ref, lb_ref, loglb_ref, log1m_ref, gn_ref, o_ref, s_ref,
                        st_scr, q_scr, k_scr, v_scr, b_scr, p_scr, vx_scr, od_scr):
    ti = pl.program_id(0)
    w = BRANCH_W
    ck, sub = HGRN_CHUNK, HGRN_SUB
    n_sub = ck // sub
    nb, tl = z_ref.shape[0], z_ref.shape[1]
    rows = range(nb)

    @pl.when(ti == 0)
    def _():
        st_scr[...] = jnp.zeros_like(st_scr)

    bd = _head_block_ones()
    bd_mask = _head_id((w, w), 0) == _head_id((w, w), 1)
    r = lax.broadcasted_iota(jnp.int32, (ck, ck), 0)
    c = lax.broadcasted_iota(jnp.int32, (ck, ck), 1)
    lower_incl = jnp.where(c <= r, 1.0, 0.0).astype(BF16)
    row_ck = lax.broadcasted_iota(jnp.int32, (ck, 1), 0)
    row_sub = lax.broadcasted_iota(jnp.int32, (sub, 1), 0)
    stack_mask = (lax.shift_right_logical(lax.broadcasted_iota(jnp.int32, (N_HEAD * sub, w), 0), 4)
                  == _head_id((N_HEAD * sub, w), 1))
    lb, loglb, log1m, gn = lb_ref[...], loglb_ref[...], log1m_ref[...], gn_ref[...]
    stack = lambda a: jnp.where(stack_mask, jnp.concatenate([a] * N_HEAD, axis=0), 0.0).astype(BF16)

    def chunk(ci, _):
        r0 = pl.multiple_of(ci * ck, ck)
        zz = [z_ref[n, pl.ds(r0, ck), :] for n in rows]
        gates = [_hgrn_gates(z[:, 0:w], z[:, w:2 * w], lb, loglb, log1m) for z in zz]
        qh, key = [g[0] for g in gates], [g[1] for g in gates]
        val = [z[:, 2 * w:3 * w] for z in zz]
        parts = [_split3(g[2]) for g in gates]
        b = [_dot(lower_incl, hi) + _dot(lower_incl, mid) + _dot(lower_incl, lo)
             for hi, mid, lo in parts]
        for n in rows:
            q_scr[n] = qh[n]
            k_scr[n] = key[n]
            v_scr[n] = val[n]
            b_scr[n] = b[n] * LOG2E

        st = [st_scr[n] for n in rows]
        out = [_dot_nt((qh[n] * jnp.exp(b[n])).astype(BF16), st[n].astype(BF16)) for n in rows]
        for j in range(n_sub - 1):
            blk = slice(j * sub, (j + 1) * sub)
            e_j = [x[(j + 1) * sub - 1:(j + 1) * sub] for x in b]
            qj = [(qh[n] * jnp.exp(jnp.where(row_ck >= (j + 1) * sub, b[n] - e_j[n], NEG_BIG))).astype(BF16)
                  for n in rows]
            k_st = [stack(key[n][blk] * jnp.exp(e_j[n] - b[n][blk])) for n in rows]
            att = [_dot_nt(qj[n], k_st[n]).astype(BF16) for n in rows]
            out = [out[n] + _dot(att[n], stack(val[n][blk])) for n in rows]

        b_last = [x[ck - 1:ck] for x in b]
        upd = [_dot_tn(val[n].astype(BF16), (key[n] * jnp.exp(b_last[n] - b[n])).astype(BF16)) for n in rows]
        for n in rows:
            st_scr[n] = jnp.where(bd_mask, st[n] * jnp.exp(b_last[n]) + upd[n], 0.0)

        def diag(n, _):
            for si in range(n_sub):
                s0 = si * sub
                q_i = q_scr[n, s0:s0 + sub, :]
                b_i = b_scr[n, s0:s0 + sub, :]
                for s in range(sub):
                    k_s = k_scr[n, s0 + s:s0 + s + 1, :]
                    b_s = b_scr[n, s0 + s:s0 + s + 1, :]
                    v_s = v_scr[n, s0 + s:s0 + s + 1, :]
                    dst = slice((s0 + s) * sub, (s0 + s + 1) * sub)
                    p_scr[dst, :] = (q_i * k_s * jnp.exp2(jnp.where(row_sub >= s, b_i - b_s, NEG_BIG))).astype(BF16)
                    vx_scr[dst, :] = jnp.broadcast_to(v_s, (sub, w))
            att = _dot(p_scr[...], bd)
            od_scr[n] = jnp.sum((att * vx_scr[...]).reshape(n_sub, sub, sub, w), axis=1).reshape(ck, w)
            return 0

        lax.fori_loop(0, nb, diag, 0)
        for n in rows:
            o_ref[n, pl.ds(r0, ck), :] = (_head_rms(out[n] + od_scr[n], gn, bd) * _silu(zz[n][:, 3 * w:4 * w]))
        return 0

    lax.fori_loop(0, tl // ck, chunk, 0)

    @pl.when(ti == pl.num_programs(0) - 1)
    def _():
        for n in rows:
            s_ref[n] = st_scr[n].T


def _hgrn_prompt(zb3, lb, loglb, log1m, gn):
    b, l, _ = zb3.shape
    w = BRANCH_W
    ck = HGRN_CHUNK
    tl = min(TL_HGRN, l)
    assert l % tl == 0 and tl % ck == 0
    fix = lambda j: (0, 0)
    scr = lambda rows, dt=F32: pltpu.VMEM((b, rows, w), dt)
    pairs = ck * HGRN_SUB
    return pl.pallas_call(
        _hgrn_prompt_kernel,
        grid=(l // tl,),
        in_specs=[pl.BlockSpec((b, tl, 4 * w), lambda j: (0, j, 0))] + [pl.BlockSpec((1, w), fix)] * 4,
        out_specs=[pl.BlockSpec((b, tl, w), lambda j: (0, j, 0)),
                   pl.BlockSpec((b, w, w), lambda j: (0, 0, 0))],
        out_shape=[jax.ShapeDtypeStruct((b, l, w), F32), jax.ShapeDtypeStruct((b, w, w), F32)],
        scratch_shapes=[scr(w), scr(ck), scr(ck), scr(ck), scr(ck),
                        pltpu.VMEM((pairs, w), BF16), pltpu.VMEM((pairs, w), F32), scr(ck)],
        compiler_params=_params("arbitrary"),
        name="hgrn_prompt",
    )(zb3, lb, loglb, log1m, gn)


def _hgrn_sample_kernel(z_ref, s0_ref, lb_ref, gn_ref, o_ref, s_ref, f_scr, k_scr, q_scr, v_scr, o_scr,
                        *, n_new, db):
    j = pl.program_id(0)
    w = BRANCH_W
    k_per_step = s0_ref.shape[0] // HEAD_W
    steps_per_head = HEAD_W // k_per_step
    lb = lb_ref[...]

    @pl.when(j == 0)
    def _():
        for t in range(n_new):
            zz = z_ref[t * db:(t + 1) * db, :]
            fr = zz[:, w:2 * w]
            f_scr[t] = (lb + (1.0 - lb) * _sigmoid(fr)).T
            k_scr[t] = ((1.0 - lb) * _sigmoid(-fr)).T
            q_scr[t] = _silu(zz[:, 0:w]).T
            v_scr[t] = zz[:, 2 * w:3 * w].T
            o_scr[t] = jnp.zeros((w, db), F32)

    head = j // steps_per_head
    k_base = head * HEAD_W + (j % steps_per_head) * k_per_step
    v0 = pl.multiple_of(head * HEAD_W, HEAD_W)
    for kk in range(k_per_step):
        rows = slice(kk * HEAD_W, (kk + 1) * HEAD_W)
        s_k = s0_ref[rows, :]
        row = k_base + kk
        for t in range(n_new):
            s_k = (f_scr[t, pl.ds(row, 1), :] * s_k
                   + k_scr[t, pl.ds(row, 1), :] * v_scr[t, pl.ds(v0, HEAD_W), :])
            o_scr[t, pl.ds(v0, HEAD_W), :] += q_scr[t, pl.ds(row, 1), :] * s_k
        s_ref[rows, :] = s_k

    @pl.when(j == pl.num_programs(0) - 1)
    def _():
        bd = _head_block_ones()
        for t in range(n_new):
            gate = _silu(z_ref[t * db:(t + 1) * db, 3 * w:4 * w])
            o_ref[t * db:(t + 1) * db, :] = _head_rms(o_scr[t].T, gn_ref[...], bd) * gate


def _hgrn_sample(layer, zb, state_t, lb, gn, n_new):
    t, _ = zb.shape
    db = t // n_new
    w = BRANCH_W
    n_state = state_t.shape[1]
    rows = 16 * HEAD_W
    fix = lambda j: (0, 0)
    scr = pltpu.VMEM((n_new, w, db), F32)
    return pl.pallas_call(
        functools.partial(_hgrn_sample_kernel, n_new=n_new, db=db),
        grid=(n_state // rows,),
        in_specs=[pl.BlockSpec((t, 4 * w), fix),
                  pl.BlockSpec((None, rows, db), lambda j: (layer, j, 0)),
                  pl.BlockSpec((1, w), fix), pl.BlockSpec((1, w), fix)],
        out_specs=[pl.BlockSpec((t, w), fix), pl.BlockSpec((rows, db), lambda j: (j, 0))],
        out_shape=[jax.ShapeDtypeStruct((t, w), F32), jax.ShapeDtypeStruct((n_state, db), F32)],
        scratch_shapes=[scr] * 5,
        compiler_params=_params("arbitrary"),
        name="hgrn_sample",
    )(zb, state_t, lb, gn)


def _rglru_gates(xconv, wg, bg, sp_lam):
    w = BRANCH_W
    g = _sigmoid(_dot(xconv.astype(BF16), wg) + bg)
    log_a = -RG_C * g[:, 0:w] * sp_lam
    a = jnp.exp(log_a)
    one_minus_a2 = -jnp.tanh(log_a) * (a * a + 1.0)
    u = jnp.sqrt(one_minus_a2) * (g[:, w:2 * w] * xconv)
    return a, u


def _rglru_prompt_kernel(z_ref, cw_ref, cb_ref, wg_ref, bg_ref, lam_ref, o_ref, h_ref, ext_scr, h_scr):
    ti = pl.program_id(1)
    w = BRANCH_W
    tl = z_ref.shape[1]
    halo = 8

    @pl.when(ti == 0)
    def _():
        ext_scr[0:halo, :] = jnp.zeros((halo, w), F32)
        h_scr[...] = jnp.zeros_like(h_scr)

    x = z_ref[0, :, 0:w]
    ext_scr[halo:, :] = x
    xconv = cb_ref[...] + cw_ref[CONV_C - 1:CONV_C, :] * x
    for j in range(CONV_C - 1):
        xconv = xconv + cw_ref[j:j + 1, :] * ext_scr[pl.ds(halo - (CONV_C - 1) + j, tl), :]
    ext_scr[0:halo, :] = x[tl - halo:tl]
    a, u = _rglru_gates(xconv, wg_ref[...], bg_ref[...], _softplus(-lam_ref[...]))
    d = 1
    while d < tl:
        u = u + a * _shift_rows(u, d, 0.0)
        a = a * _shift_rows(a, d, 1.0)
        d *= 2
    h = u + a * h_scr[0:1, :]
    h_last = h[tl - 1:tl]
    h_scr[...] = jnp.broadcast_to(h_last, h_scr.shape)
    h_ref[0] = h_last
    o_ref[0] = h * _gelu(z_ref[0, :, w:2 * w])


def _rglru_prompt(zc3, cw, cb, wg, bg, lam):
    b, l, _ = zc3.shape
    w = BRANCH_W
    tl = min(TL_SCAN, l)
    fix = lambda i, j: (0, 0)
    return pl.pallas_call(
        _rglru_prompt_kernel,
        grid=(b, l // tl),
        in_specs=[pl.BlockSpec((1, tl, 2 * w), lambda i, j: (i, j, 0)),
                  pl.BlockSpec((CONV_C, w), fix), pl.BlockSpec((1, w), fix),
                  pl.BlockSpec((w, 2 * w), fix), pl.BlockSpec((1, 2 * w), fix), pl.BlockSpec((1, w), fix)],
        out_specs=[pl.BlockSpec((1, tl, w), lambda i, j: (i, j, 0)),
                   pl.BlockSpec((1, 1, w), lambda i, j: (i, 0, 0))],
        out_shape=[jax.ShapeDtypeStruct((b, l, w), F32), jax.ShapeDtypeStruct((b, 1, w), F32)],
        scratch_shapes=[pltpu.VMEM((tl + 8, w), F32), pltpu.VMEM((8, w), F32)],
        compiler_params=_params("parallel", "arbitrary"),
        name="rglru_prompt",
    )(zc3, cw, cb, wg, bg, lam)


def _pool_select(sums, x, pos, wp, scale):
    lane_group = _head_id((1, BRANCH_W), 1)
    pooled = jnp.zeros_like(x)
    for g, win in enumerate(POOL_WINDOWS):
        cnt = jnp.minimum(pos + 1, win).astype(F32)
        pooled = jnp.where(lane_group == g, sums[g] / cnt, pooled)
    return _dot((pooled - x).astype(BF16), wp) * scale


def _pool_prompt_kernel(x_ref, wp_ref, sc_ref, o_ref, ext_scr):
    ti = pl.program_id(1)
    w = BRANCH_W
    tl = x_ref.shape[1]
    halo = 16

    @pl.when(ti == 0)
    def _():
        ext_scr[0:halo, :] = jnp.zeros((halo, w), F32)

    x = x_ref[0]
    ext_scr[halo:, :] = x
    e = ext_scr[...]
    sums = []
    d = 1
    for _ in POOL_WINDOWS:
        e = e + pltpu.roll(e, d, 0)
        sums.append(e[halo:])
        d *= 2
    ext_scr[0:halo, :] = x[tl - halo:tl]
    pos = ti * tl + lax.broadcasted_iota(jnp.int32, (tl, 1), 0)
    o_ref[0] = _pool_select(sums, x, pos, wp_ref[...], sc_ref[...])


def _pool_prompt(zd3, wp, scale):
    b, l, w = zd3.shape
    tl = min(TL_SCAN, l)
    fix = lambda i, j: (0, 0)
    return pl.pallas_call(
        _pool_prompt_kernel,
        grid=(b, l // tl),
        in_specs=[pl.BlockSpec((1, tl, w), lambda i, j: (i, j, 0)),
                  pl.BlockSpec((w, w), fix), pl.BlockSpec((1, w), fix)],
        out_specs=pl.BlockSpec((1, tl, w), lambda i, j: (i, j, 0)),
        out_shape=jax.ShapeDtypeStruct((b, l, w), F32),
        scratch_shapes=[pltpu.VMEM((tl + 16, w), F32)],
        compiler_params=_params("parallel", "arbitrary"),
        name="pool_prompt",
    )(zd3, wp, scale)


def _cd_sample_kernel(zc_ref, zd_ref, h0_ref, ch_ref, ph_ref, cw_ref, cb_ref, wg_ref, bg_ref, lam_ref,
                      wp_ref, sc_ref, oc_ref, od_ref, h_ref, cn_ref, pn_ref, xc_scr, *, n_new, db, pos0):
    w = BRANCH_W
    slab = lambda ref, t, c0=0: ref[t * db:(t + 1) * db, c0:c0 + w]
    hist = lambda ref, j: ref[j]

    n_hist = CONV_C - 1
    ext = [hist(ch_ref, j) for j in range(n_hist)] + [slab(zc_ref, t) for t in range(n_new)]
    for t in range(n_new):
        acc = cb_ref[...] + cw_ref[0:1, :] * ext[t]
        for j in range(1, CONV_C):
            acc = acc + cw_ref[j:j + 1, :] * ext[t + j]
        xc_scr[t * db:(t + 1) * db, :] = acc
    a, u = _rglru_gates(xc_scr[...], wg_ref[...], bg_ref[...], _softplus(-lam_ref[...]))
    h = h0_ref[...]
    for t in range(n_new):
        h = a[t * db:(t + 1) * db] * h + u[t * db:(t + 1) * db]
        oc_ref[t * db:(t + 1) * db, :] = h * _gelu(slab(zc_ref, t, w))
    h_ref[...] = h
    for j in range(n_hist):
        cn_ref[j] = ext[n_new + j]

    pext = [hist(ph_ref, j) for j in range(POOL_HIST)] + [slab(zd_ref, t) for t in range(n_new)]
    for t in range(n_new):
        sums, run, k = [], None, 0
        for win in POOL_WINDOWS:
            while k < win:
                term = pext[POOL_HIST + t - k]
                run = term if run is None else run + term
                k += 1
            sums.append(run)
        pos = jnp.full((db, 1), pos0 + t, jnp.int32)
        od_ref[t * db:(t + 1) * db, :] = _pool_select(sums, pext[POOL_HIST + t], pos, wp_ref[...], sc_ref[...])
    for j in range(POOL_HIST):
        pn_ref[j] = pext[n_new + j]


def _cd_sample(zc, zd, h0, conv_hist, pool_hist, cw, cb, wg, bg, lam, wp, scale, n_new, pos0):
    t, _ = zc.shape
    db = t // n_new
    w = BRANCH_W
    shp = lambda c: jax.ShapeDtypeStruct((db, c), F32)
    return pl.pallas_call(
        functools.partial(_cd_sample_kernel, n_new=n_new, db=db, pos0=pos0),
        out_shape=[jax.ShapeDtypeStruct((t, w), F32), jax.ShapeDtypeStruct((t, w), F32),
                   shp(w), jax.ShapeDtypeStruct((CONV_C - 1, db, w), F32),
                   jax.ShapeDtypeStruct((POOL_HIST, db, w), F32)],
        scratch_shapes=[pltpu.VMEM((t, w), F32)],
        compiler_params=pltpu.CompilerParams(vmem_limit_bytes=V7X_VMEM_LIMIT_BYTES),
        name="rglru_pool_sample",
    )(zc, zd, h0, conv_hist, pool_hist, cw, cb, wg, bg, lam, wp, scale)


def _merge_kernel(x_ref, oa_ref, ob_ref, oc_ref, od_ref, g_ref, wg_ref, wb_ref, wo_ref, o_ref):
    x = x_ref[...]
    d = x.shape[1]
    h = _rms(x, g_ref[...]).astype(BF16)
    mix = jnp.zeros(x.shape, F32)
    for n, br in enumerate((oa_ref, ob_ref, oc_ref, od_ref)):
        gate = _sigmoid(_dot(h, wg_ref[:, n * d:(n + 1) * d]))
        mix = mix + gate * _dot(br[...].astype(BF16), wb_ref[n])
    o_ref[...] = x + _dot(mix.astype(BF16), wo_ref[...])


def _merge(layer, x2, oa, ob, oc, od, g, wg, wb, wo):
    t, d = x2.shape
    tm = min(TM_PROJ, t)
    w = BRANCH_W
    row = lambda i: (i, 0)
    fix = lambda i: (0, 0)
    return pl.pallas_call(
        _merge_kernel,
        grid=(t // tm,),
        in_specs=[pl.BlockSpec((tm, d), row)] + [pl.BlockSpec((tm, w), row)] * 4
                 + [pl.BlockSpec((1, d), fix), pl.BlockSpec((d, N_BRANCH * d), fix),
                    pl.BlockSpec((None, N_BRANCH, w, d), lambda i: (layer, 0, 0, 0)),
                    pl.BlockSpec((None, d, d), lambda i: (layer, 0, 0))],
        out_specs=pl.BlockSpec((tm, d), row),
        out_shape=jax.ShapeDtypeStruct((t, d), F32),
        compiler_params=_params("parallel"),
        name="merge",
    )(x2, oa, ob, oc, od, g, wg, wb, wo)


def _ple(x, pe, gp, wpg, wp):
    gate = _sigmoid(_dot(_rms(x, gp).astype(BF16), wpg))
    return x + gate * _dot(pe.astype(BF16), wp)


def _ffn_prompt_kernel(x_ref, xp_ref, g_ref, wu_ref, cw_ref, cb_ref, wd_ref, pe_ref, gp_ref, wpg_ref, wp_ref,
                       o_ref, s_ref, e_scr, gated_scr, *, tiles_per_seq):
    i = pl.program_id(0)
    tm = x_ref.shape[0]
    halo = FFN_HALO
    dff = wd_ref.shape[0]
    tf = e_scr.shape[3]
    x = x_ref[...]
    xn = _rms(x, g_ref[...]).astype(BF16)
    xnp = _rms(xp_ref[...], g_ref[...]).astype(BF16)
    seq_start = (i % tiles_per_seq) == 0

    def conv_half(e_ref, c0):
        cols = slice(c0, c0 + tf)
        w = wu_ref[:, cols]
        e_ref[0:halo, :] = jnp.where(seq_start, 0.0, _dot(xnp, w))
        e_ref[halo:, :] = _dot(xn, w)
        s_ref[0, :, cols] = e_ref[pl.ds(halo + tm - (CONV_F - 1), CONV_F - 1), :]
        y = cb_ref[:, cols]
        for j in range(CONV_F):
            y = y + cw_ref[j:j + 1, cols] * e_ref[pl.ds(halo - (CONV_F - 1) + j, tm), :]
        return y

    n_chunks = dff // tf
    x2 = x
    done = 0
    for c in range(n_chunks + 1):
        if c < n_chunks:
            slot = c % 2
            ya = conv_half(e_scr.at[slot, 0], c * tf)
            yb = conv_half(e_scr.at[slot, 1], dff + c * tf)
        if c == n_chunks or (c - done > FFN_DOWN_GROUP):
            stop = c if c == n_chunks else c - 1
            rows = slice(done * tf, stop * tf)
            x2 = x2 + _dot(gated_scr[:, rows], wd_ref[rows, :])
            done = stop
        if c < n_chunks:
            gated_scr[:, c * tf:(c + 1) * tf] = (_gelu(ya) * yb).astype(BF16)
    o_ref[...] = _ple(x2, pe_ref[...], gp_ref[...], wpg_ref[...], wp_ref[...])


def _ffn_prompt(layer, x2, seq_len, g, w_up, cw, cb, w_down, pe3, gp, wpg, wp):
    t, d = x2.shape
    dff = w_down.shape[1]
    tm = min(TM_FFN, seq_len)
    tf = TF_FFN
    halo = FFN_HALO
    assert seq_len % tm == 0 and dff % tf == 0 and tm % halo == 0
    tps = seq_len // tm
    dp = pe3.shape[2]
    fix = lambda i: (0, 0)
    once = pl.Buffered(1)
    weight = lambda shape: pl.BlockSpec((None,) + shape, lambda i: (layer, 0, 0), pipeline_mode=once)
    return pl.pallas_call(
        functools.partial(_ffn_prompt_kernel, tiles_per_seq=tps),
        grid=(t // tm,),
        in_specs=[pl.BlockSpec((tm, d), lambda i: (i, 0)),
                  pl.BlockSpec((halo, d), lambda i: (jnp.maximum(i * (tm // halo) - 1, 0), 0)),
                  pl.BlockSpec((1, d), fix),
                  weight((d, 2 * dff)),
                  pl.BlockSpec((CONV_F, 2 * dff), fix), pl.BlockSpec((1, 2 * dff), fix),
                  weight((dff, d)),
                  pl.BlockSpec((None, tm, dp), lambda i: (layer, i, 0)),
                  pl.BlockSpec((1, d), fix), weight((d, d)), weight((dp, d))],
        out_specs=[pl.BlockSpec((tm, d), lambda i: (i, 0)),
                   pl.BlockSpec((1, CONV_F - 1, 2 * dff), lambda i: (i, 0, 0))],
        out_shape=[jax.ShapeDtypeStruct((t, d), F32), jax.ShapeDtypeStruct((t // tm, CONV_F - 1, 2 * dff), F32)],
        scratch_shapes=[pltpu.VMEM((2, 2, tm + halo, tf), F32), pltpu.VMEM((tm, dff), BF16)],
        compiler_params=_params("arbitrary"),
        name="ffn_prompt",
    )(x2, x2, g, w_up, cw, cb, w_down, pe3, gp, wpg, wp)


def _ffn_sample_kernel(x_ref, g_ref, wa_ref, wb_ref, cwa_ref, cwb_ref, cba_ref, cbb_ref, wd_ref,
                       ha_ref, hb_ref, pe_ref, gp_ref, wpg_ref, wp_ref,
                       o_ref, sa_ref, sb_ref, xn_scr, acc_scr, g_scr, *, n_new, db):
    f = pl.program_id(0)

    @pl.when(f == 0)
    def _():
        xn_scr[...] = _rms(x_ref[...], g_ref[...]).astype(BF16)
        acc_scr[...] = jnp.zeros_like(acc_scr)

    def conv_half(w_ref, cw_ref, cb_ref, h_ref, s_ref):
        u = _dot(xn_scr[...], w_ref[...])
        ext = [h_ref[:, j, :] for j in range(CONV_F - 1)] + [u[t * db:(t + 1) * db] for t in range(n_new)]
        for j in range(CONV_F - 1):
            s_ref[:, j, :] = ext[n_new + j]
        ys = []
        for t in range(n_new):
            y = cb_ref[...] + cw_ref[0:1, :] * ext[t]
            for j in range(1, CONV_F):
                y = y + cw_ref[j:j + 1, :] * ext[t + j]
            ys.append(y)
        return ys

    ya = conv_half(wa_ref, cwa_ref, cba_ref, ha_ref, sa_ref)
    yb = conv_half(wb_ref, cwb_ref, cbb_ref, hb_ref, sb_ref)
    for t in range(n_new):
        g_scr[t * db:(t + 1) * db, :] = (_gelu(ya[t]) * yb[t]).astype(BF16)
    acc_scr[...] += _dot(g_scr[...], wd_ref[...])

    @pl.when(f == pl.num_programs(0) - 1)
    def _():
        o_ref[...] = _ple(x_ref[...] + acc_scr[...], pe_ref[...], gp_ref[...], wpg_ref[...], wp_ref[...])


def _ffn_sample(layer, x2, g, w_up, cw, cb, w_down, hist4, pe3, gp, wpg, wp, n_new):
    t, d = x2.shape
    db = t // n_new
    dff = w_down.shape[1]
    tf = TF_FFN
    nf = dff // tf
    dp = pe3.shape[2]
    nh = CONV_F - 1
    fix = lambda f: (0, 0)
    hist = lambda half: pl.BlockSpec((None, db, nh, tf), lambda f: (layer, 0, 0, f + half * nf))
    st_shape = jax.ShapeDtypeStruct((db, nh, dff), F32)
    st_spec = pl.BlockSpec((db, nh, tf), lambda f: (0, 0, f))
    return pl.pallas_call(
        functools.partial(_ffn_sample_kernel, n_new=n_new, db=db),
        grid=(nf,),
        in_specs=[pl.BlockSpec((t, d), fix), pl.BlockSpec((1, d), fix),
                  pl.BlockSpec((None, d, tf), lambda f: (layer, 0, f)),
                  pl.BlockSpec((None, d, tf), lambda f: (layer, 0, f + nf)),
                  pl.BlockSpec((CONV_F, tf), lambda f: (0, f)), pl.BlockSpec((CONV_F, tf), lambda f: (0, f + nf)),
                  pl.BlockSpec((1, tf), lambda f: (0, f)), pl.BlockSpec((1, tf), lambda f: (0, f + nf)),
                  pl.BlockSpec((None, tf, d), lambda f: (layer, f, 0)),
                  hist(0), hist(1),
                  pl.BlockSpec((None, t, dp), lambda f: (layer, 0, 0)),
                  pl.BlockSpec((1, d), fix), pl.BlockSpec((None, d, d), lambda f: (layer, 0, 0)),
                  pl.BlockSpec((None, dp, d), lambda f: (layer, 0, 0))],
        out_specs=[pl.BlockSpec((t, d), fix), st_spec, st_spec],
        out_shape=[jax.ShapeDtypeStruct((t, d), F32), st_shape, st_shape],
        scratch_shapes=[pltpu.VMEM((t, d), BF16), pltpu.VMEM((t, d), F32), pltpu.VMEM((t, tf), BF16)],
        compiler_params=_params("arbitrary"),
        name="ffn_sample",
    )(x2, g, w_up, w_up, cw, cw, cb, cb, w_down, hist4, hist4, pe3, gp, wpg, wp)


def _block_diag(w4):
    h, n, _ = w4.shape
    eye = jnp.eye(h, dtype=w4.dtype)
    return (eye[:, None, :, None] * w4[:, :, None, :]).reshape(h * n, h * n)


def kernel(x_prompt, x_sample, cache_k, cache_v, state_hgrn, state_rglru_h, state_rglru_conv, state_pool,
           state_ffn_conv, page_table, p_prompt, p_sample, norm_mix, w_in, q_norm, k_norm, sb_bias, lb_logits,
           hgrn_norm, conv_c_w, conv_c_b, w_rg_a, b_rg_a, w_rg_x, b_rg_x, lam, w_pool, pool_scale, w_branch,
           w_out, norm_ffn, w_up, conv_f_w, conv_f_b, w_down, norm_ple, w_ple_gate, w_ple):
    depth = w_in.shape[0]
    b, l, d = x_prompt.shape
    db, n_new, _ = x_sample.shape
    w = BRANCH_W
    n_mix = 10 * w
    dff = w_down.shape[1]
    dp = p_prompt.shape[-1]
    n_pool, page = cache_k.shape[1], cache_k.shape[2]
    pos0 = page_table.shape[1] * page

    lb_all, loglb_all, log1m_all = _lower_bounds(lb_logits.astype(F32))
    row = lambda a, i: a[i].reshape(1, -1)
    tile_heads = lambda a, i: jnp.tile(a[i], N_HEAD).reshape(1, w)

    xp = x_prompt.reshape(b * l, d)
    xs = x_sample.transpose(1, 0, 2).reshape(n_new * db, d)
    pe_p = p_prompt.reshape(depth, b * l, dp)
    pe_s = p_sample.transpose(0, 2, 1, 3).reshape(depth, n_new * db, dp)
    cache_kt = cache_k.transpose(0, 1, 3, 4, 2)
    cache_vt = cache_v.transpose(0, 1, 3, 4, 2)
    hgrn_state = state_hgrn.transpose(0, 2, 3, 4, 1).reshape(depth, -1, db)
    conv_state = state_rglru_conv.transpose(0, 2, 1, 3)
    pool_state = state_pool.transpose(0, 2, 1, 3)

    w_br, w_o = w_branch.astype(BF16), w_out.astype(BF16)
    w_u, w_d = w_up.astype(BF16), w_down.astype(BF16)
    w_pg, w_pe = w_ple_gate.astype(BF16), w_ple.astype(BF16)

    outs = {k: [] for k in ("kp", "vp", "ks", "vs", "sp", "ss", "hp", "hs", "cp", "cs", "pp", "ps", "fp", "fs")}
    for i in range(depth):
        w_mix = w_in[i, :, :n_mix].astype(BF16)
        w_gate = w_in[i, :, n_mix:].astype(BF16)
        wg_c = jnp.concatenate([_block_diag(w_rg_a[i]), _block_diag(w_rg_x[i])], axis=1).astype(BF16)
        bg_c = jnp.concatenate([b_rg_a[i], b_rg_x[i]]).reshape(1, 2 * w)
        wp_d = _block_diag(w_pool[i]).astype(BF16)
        g_mix, g_ffn, g_ple = row(norm_mix, i), row(norm_ffn, i), row(norm_ple, i)
        qn, kn, gn = tile_heads(q_norm, i), tile_heads(k_norm, i), tile_heads(hgrn_norm, i)
        lb, loglb, log1m = row(lb_all, i), row(loglb_all, i), row(log1m_all, i)
        cw_c, cb_c, lam_i, sc_d = conv_c_w[i], row(conv_c_b, i), row(lam, i), row(pool_scale, i)
        cw_f, cb_f = conv_f_w[i], row(conv_f_b, i)
        bias = sb_bias[i].astype(F32)

        q, kt, vt, ktb, vtb, zb, zc, zd = _inproj_prompt(xp.reshape(b, l, d), g_mix, w_mix, qn, kn)
        o_a = _attn_prompt(bias, q, ktb, vtb)
        o_b, s_p = _hgrn_prompt(zb, lb, loglb, log1m, gn)
        o_c, h_p = _rglru_prompt(zc, cw_c, cb_c, wg_c, bg_c, lam_i)
        o_d = _pool_prompt(zd, wp_d, sc_d)
        f2 = lambda a: a.reshape(b * l, w)
        x1 = _merge(i, xp, f2(o_a), f2(o_b), f2(o_c), f2(o_d), g_mix, w_gate, w_br, w_o)
        xp, f_p = _ffn_prompt(i, x1, l, g_ffn, w_u, cw_f, cb_f, w_d, pe_p, g_ple, w_pg, w_pe)
        outs["kp"].append(kt)
        outs["vp"].append(vt)
        s_heads = s_p.reshape(b, N_HEAD, HEAD_W, N_HEAD, HEAD_W)
        outs["sp"].append(jnp.stack([s_heads[:, h, :, h, :] for h in range(N_HEAD)], axis=1))
        outs["hp"].append(h_p.reshape(b, w))
        outs["cp"].append(zc[:, l - (CONV_C - 1):, :w])
        outs["pp"].append(zd[:, l - POOL_HIST:, :])
        tps = f_p.shape[0] // b
        outs["fp"].append(f_p[tps - 1::tps])

        q, k, v, zb, zc, zd = _inproj_sample(xs, g_mix, w_mix, qn, kn)
        r4 = lambda a: a.reshape(n_new, db, 1, w)
        o_a = _attn_sample(i, page_table, bias, r4(q), r4(k), r4(v), cache_kt, cache_vt).reshape(n_new * db, w)
        o_b, s_s = _hgrn_sample(i, zb, hgrn_state, lb, gn, n_new)
        o_c, o_d, h_s, c_s, p_s = _cd_sample(zc, zd, state_rglru_h[i], conv_state[i], pool_state[i],
                                             cw_c, cb_c, wg_c, bg_c, lam_i, wp_d, sc_d, n_new, pos0)
        x1 = _merge(i, xs, o_a, o_b, o_c, o_d, g_mix, w_gate, w_br, w_o)
        xs, fa, fb = _ffn_sample(i, x1, g_ffn, w_u, cw_f, cb_f, w_d, state_ffn_conv, pe_s,
                                 g_ple, w_pg, w_pe, n_new)
        outs["ks"].append(k)
        outs["vs"].append(v)
        outs["ss"].append(s_s)
        outs["hs"].append(h_s)
        outs["cs"].append(c_s)
        outs["ps"].append(p_s)
        outs["fs"].append(jnp.concatenate([fa, fb], axis=-1))

    stk = lambda key: jnp.stack(outs[key], axis=0)
    to_batch_major = lambda a: a.reshape(depth, n_new, db, N_HEAD, HEAD_W).transpose(0, 2, 1, 3, 4)
    y_sample = xs.reshape(n_new, db, d).transpose(1, 0, 2)
    from_transposed = lambda a: a.reshape(depth, b, N_HEAD, HEAD_W, l).transpose(0, 1, 4, 2, 3)
    hgrn_s = stk("ss").reshape(depth, N_HEAD, HEAD_W, HEAD_W, db).transpose(0, 4, 1, 2, 3)
    rows_to_batch = lambda a: a.transpose(0, 2, 1, 3)
    return (xp.reshape(b, l, d), y_sample,
            from_transposed(stk("kp")), from_transposed(stk("vp")),
            to_batch_major(stk("ks")), to_batch_major(stk("vs")),
            stk("sp"), hgrn_s, stk("hp"), stk("hs"), stk("cp"), rows_to_batch(stk("cs")),
            stk("pp"), rows_to_batch(stk("ps")), stk("fp"), stk("fs"))
```

```python
import functools
import math

import jax
import jax.numpy as jnp
from jax import lax
from jax.experimental import pallas as pl
from jax.experimental.pallas import tpu as pltpu

F32 = jnp.float32
BF16 = jnp.bfloat16
EPS = 1e-6

N_HEAD = 4
HEAD_W = 64
BRANCH_W = N_HEAD * HEAD_W
N_BRANCH = 4
RG_C = 8.0
POOL_WINDOWS = (2, 4, 8, 16)
POOL_HIST = max(POOL_WINDOWS) - 1
CONV_C = 4
CONV_F = 3
HGRN_CHUNK = 64
HGRN_SUB = 16
NEG_BIG = -1e30
LOG2E = math.log2(math.e)

V7X_VMEM_LIMIT_BYTES = 56 * 1024 * 1024

TM_PROJ = 512
TQ_ATTN = 256
TK_ATTN = 256
TL_SCAN = 512
TL_HGRN = 256
TM_FFN = 512
TF_FFN = 256
FFN_HALO = 16
FFN_DOWN_GROUP = 4
SAMPLES_PER_STEP = 2


def _params(*sem):
    return pltpu.CompilerParams(dimension_semantics=sem, vmem_limit_bytes=V7X_VMEM_LIMIT_BYTES)


def _dot(a, b):
    return jnp.dot(a, b, preferred_element_type=F32)


def _dot_nt(a, b):
    return lax.dot_general(a, b, (((1,), (1,)), ((), ())), preferred_element_type=F32)


def _dot_tn(a, b):
    return lax.dot_general(a, b, (((0,), (0,)), ((), ())), preferred_element_type=F32)


def _split2(x):
    hi = x.astype(BF16)
    lo = (x - hi.astype(F32)).astype(BF16)
    return hi, lo


def _split3(x):
    hi = x.astype(BF16)
    r = x - hi.astype(F32)
    mid = r.astype(BF16)
    lo = (r - mid.astype(F32)).astype(BF16)
    return hi, mid, lo


def _dot_x2(x, m):
    hi, lo = _split2(x)
    return _dot(hi, m) + _dot(lo, m)


def _sigmoid(x):
    return 1.0 / (1.0 + jnp.exp(-x))


def _softplus_tail(x):
    return jnp.log1p(jnp.exp(-jnp.abs(x)))


def _log_sigmoid(x):
    return jnp.minimum(x, 0.0) - _softplus_tail(x)


def _softplus(x):
    return jnp.maximum(x, 0.0) + _softplus_tail(x)


def _silu(x):
    return x * _sigmoid(x)


def _gelu(x):
    c = math.sqrt(2.0 / math.pi)
    return 0.5 * x * (1.0 + jnp.tanh(c * (x + 0.044715 * (x * x * x))))


def _rms(x, g):
    ms = jnp.mean(x * x, axis=-1, keepdims=True)
    return x * lax.rsqrt(ms + EPS) * g


def _head_id(shape, dim):
    return lax.shift_right_logical(lax.broadcasted_iota(jnp.int32, shape, dim), 6)


def _head_block_ones():
    n = BRANCH_W
    return jnp.where(_head_id((n, n), 0) == _head_id((n, n), 1), 1.0, 0.0).astype(BF16)


def _head_rms(a, g, bd):
    ms = _dot_x2(a * a, bd) * (1.0 / HEAD_W)
    return a * lax.rsqrt(ms + EPS) * g


def _shift_rows(x, d, fill):
    rolled = pltpu.roll(x, d, 0)
    row = lax.broadcasted_iota(jnp.int32, x.shape, 0)
    return jnp.where(row < d, fill, rolled)


def _lb_kernel(lg_ref, lb_ref, loglb_ref, log1m_ref):
    x = lg_ref[...]
    depth = x.shape[0]
    rows = [x[i:i + 1] for i in range(depth)]
    m = functools.reduce(jnp.maximum, rows)
    e = [jnp.exp(r - m) for r in rows]
    tot = functools.reduce(lambda a, b: a + b, e)
    zero = jnp.zeros_like(m)
    lb_ref[0:1, :] = zero
    loglb_ref[0:1, :] = jnp.full_like(m, -jnp.inf)
    log1m_ref[0:1, :] = zero
    acc = zero
    for i in range(1, depth):
        acc = acc + e[i] / tot
        lb_ref[i:i + 1, :] = acc
        loglb_ref[i:i + 1, :] = jnp.log(acc)
        log1m_ref[i:i + 1, :] = jnp.log1p(-acc)


def _lower_bounds(lb_logits):
    shp = jax.ShapeDtypeStruct(lb_logits.shape, F32)
    return pl.pallas_call(_lb_kernel, out_shape=(shp, shp, shp), name="hgrn_lower_bounds")(lb_logits)


def _inproj_common(x, g_ref, w_ref, qn_ref, kn_ref):
    h = _rms(x, g_ref[...]).astype(BF16)
    z = _dot(h, w_ref[...])
    bd = _head_block_ones()
    w = BRANCH_W
    q = _head_rms(z[:, 0:w], qn_ref[...], bd) * (HEAD_W ** -0.5 * LOG2E)
    k = _head_rms(z[:, w:2 * w], kn_ref[...], bd)
    return q, k, z[:, 2 * w:3 * w], z[:, 3 * w:7 * w], z[:, 7 * w:9 * w], z[:, 9 * w:10 * w]


def _inproj_prompt_kernel(x_ref, g_ref, w_ref, qn_ref, kn_ref,
                          q_o, kt_o, vt_o, ktb_o, vtb_o, zb_o, zc_o, zd_o):
    q, k, v, zb, zc, zd = _inproj_common(x_ref[0], g_ref, w_ref, qn_ref, kn_ref)
    q_o[0] = q.astype(BF16)
    kt, vt = k.T, v.T
    kt_o[0] = kt
    vt_o[0] = vt
    tk = ktb_o.shape[2]
    for c in range(ktb_o.shape[0]):
        ktb_o[c] = kt[:, c * tk:(c + 1) * tk].astype(BF16)
        vtb_o[c] = vt[:, c * tk:(c + 1) * tk].astype(BF16)
    zb_o[0] = zb
    zc_o[0] = zc
    zd_o[0] = zd


def _inproj_prompt(x3, g, w, qn, kn):
    b, l, d = x3.shape
    tm = min(TM_PROJ, l)
    tk = min(TK_ATTN, l)
    n = w.shape[1]
    bw = BRANCH_W
    per = tm // tk
    row = lambda i, j: (i, j, 0)
    fix = lambda i, j: (0, 0)
    col = lambda i, j: (i, 0, j)
    tiles = lambda i, j: (i * (l // tm) + j, 0, 0)
    tok = lambda c, dt: (jax.ShapeDtypeStruct((b, l, c), dt), pl.BlockSpec((1, tm, c), row))
    tr = (jax.ShapeDtypeStruct((b, bw, l), F32), pl.BlockSpec((1, bw, tm), col))
    trb = (jax.ShapeDtypeStruct((b * l // tk, bw, tk), BF16), pl.BlockSpec((per, bw, tk), tiles))
    outs = [tok(bw, BF16), tr, tr, trb, trb, tok(4 * bw, F32), tok(2 * bw, F32), tok(bw, F32)]
    return pl.pallas_call(
        _inproj_prompt_kernel,
        grid=(b, l // tm),
        in_specs=[pl.BlockSpec((1, tm, d), row), pl.BlockSpec((1, d), fix), pl.BlockSpec((d, n), fix),
                  pl.BlockSpec((1, bw), fix), pl.BlockSpec((1, bw), fix)],
        out_specs=[s for _, s in outs],
        out_shape=[s for s, _ in outs],
        compiler_params=_params("parallel", "parallel"),
        name="inproj_prompt",
    )(x3, g, w, qn, kn)


def _inproj_sample_kernel(x_ref, g_ref, w_ref, qn_ref, kn_ref, q_o, k_o, v_o, zb_o, zc_o, zd_o):
    for ref, val in zip((q_o, k_o, v_o, zb_o, zc_o, zd_o),
                        _inproj_common(x_ref[...], g_ref, w_ref, qn_ref, kn_ref)):
        ref[...] = val


def _inproj_sample(x2, g, w, qn, kn):
    t, d = x2.shape
    bw = BRANCH_W
    widths = (bw, bw, bw, 4 * bw, 2 * bw, bw)
    return pl.pallas_call(
        _inproj_sample_kernel,
        out_shape=[jax.ShapeDtypeStruct((t, c), F32) for c in widths],
        compiler_params=pltpu.CompilerParams(vmem_limit_bytes=V7X_VMEM_LIMIT_BYTES),
        name="inproj_sample",
    )(x2, g, w, qn, kn)


def _upper_ones(n):
    r = lax.broadcasted_iota(jnp.int32, (n, n), 0)
    c = lax.broadcasted_iota(jnp.int32, (n, n), 1)
    return jnp.where(r > c, 1.0, 0.0).astype(BF16)


def _sb_logs(z, mask):
    m = jnp.maximum(z, 0.0)
    n = z - m
    t = jnp.log2(1.0 + jnp.exp2(n - m))
    drop = m + t
    return n - t, (drop if mask is None else jnp.where(mask, drop, 0.0))


def _attn_prompt_kernel(bias_ref, q_ref, kt_ref, vt_ref, o_ref):
    qi = pl.program_id(1)
    tq, tk = q_ref.shape[1], kt_ref.shape[2]
    per = tq // tk
    q = q_ref[0]
    heads = range(N_HEAD)
    lanes = lambda h: slice(h * HEAD_W, (h + 1) * HEAD_W)
    qs = [q[:, lanes(h)] for h in heads]
    bias = [bias_ref[h] * LOG2E for h in heads]
    upper = _upper_ones(tk)
    q_pos = qi * tq + lax.broadcasted_iota(jnp.int32, (tq, tk), 0)
    k_off = lax.broadcasted_iota(jnp.int32, (tq, tk), 1)

    def logits(kidx):
        return [_dot(qs[h], kt_ref[kidx, lanes(h), :]) + bias[h] for h in heads]

    def sums(zs, mask):
        logs = [_sb_logs(z, mask) for z in zs]
        local = [_dot(drop.astype(BF16), upper) for _, drop in logs]
        return ([lb for lb, _ in logs], local, [local[h][:, 0:1] + logs[h][1][:, 0:1] for h in heads])

    def values(kidx, log_beta, local, carries, mask):
        pvs = []
        for h in heads:
            w = jnp.exp2(log_beta[h] - (local[h] + carries[h]))
            if mask is not None:
                w = jnp.where(mask, w, 0.0)
            pvs.append(_dot_nt(w.astype(BF16), vt_ref[kidx, lanes(h), :]))
        return pvs

    add = lambda xs, ys: [x + y for x, y in zip(xs, ys)]
    first = qi * per
    accs = [jnp.zeros((tq, HEAD_W), F32)] * N_HEAD
    carries = [jnp.zeros((tq, 1), F32)] * N_HEAD
    for m in range(per - 1, -1, -1):
        kidx = first + m
        mask = kidx * tk + k_off < q_pos
        log_beta, local, totals = sums(logits(kidx), mask)
        accs = add(accs, values(kidx, log_beta, local, carries, mask))
        carries = add(carries, totals)

    def body(j, state):
        accs, carries = state
        kidx = first - 1 - j
        log_beta, local, totals = sums(logits(kidx), None)
        return add(accs, values(kidx, log_beta, local, carries, None)), add(carries, totals)

    accs = lax.fori_loop(0, first, body, (accs, carries))[0]
    for h in heads:
        o_ref[0, :, lanes(h)] = accs[h]


def _attn_prompt(bias, q3, ktb, vtb):
    b, l, w = q3.shape
    tk = ktb.shape[2]
    tq = min(TQ_ATTN, l)
    nk = l // tk
    return pl.pallas_call(
        _attn_prompt_kernel,
        grid=(b, l // tq),
        in_specs=[pl.BlockSpec(memory_space=pltpu.SMEM),
                  pl.BlockSpec((1, tq, w), lambda i, j: (i, j, 0)),
                  pl.BlockSpec((nk, w, tk), lambda i, j: (i, 0, 0)),
                  pl.BlockSpec((nk, w, tk), lambda i, j: (i, 0, 0))],
        out_specs=pl.BlockSpec((1, tq, w), lambda i, j: (i, j, 0)),
        out_shape=jax.ShapeDtypeStruct((b, l, w), F32),
        compiler_params=_params("parallel", "arbitrary"),
        name="attn_prompt",
    )(bias, q3, ktb, vtb)


def _attn_sample_kernel(pt_ref, bias_ref, q_ref, kn_ref, vn_ref, *refs, n_pages, n_new, n_samp):
    del pt_ref
    o_ref = refs[-1]
    samples = range(n_samp)
    k_refs = [refs[s * n_pages:(s + 1) * n_pages] for s in samples]
    v_refs = [refs[(n_samp + s) * n_pages:(n_samp + s + 1) * n_pages] for s in samples]
    page = refs[0].shape[-1]
    flat = lambda ref: ref[...].reshape(BRANCH_W, page).astype(BF16)
    rows = N_HEAD * 8
    row = lax.broadcasted_iota(jnp.int32, (rows, 1), 0)
    row_t = jnp.bitwise_and(row, 7)
    row_h = lax.shift_right_logical(row, 3)
    lane_head = _head_id((1, BRANCH_W), 1)
    bias = jnp.zeros((rows, 1), F32)
    for h in range(N_HEAD):
        bias = jnp.where(row_h == h, bias_ref[h] * LOG2E, bias)

    qs, carry, out = [], [], []
    for s in samples:
        q = jnp.zeros((rows, BRANCH_W), F32)
        for t in range(n_new):
            q = q + jnp.where(row_t == t, q_ref[t, s], 0.0)
        q = jnp.where(row_h == lane_head, q, 0.0)
        c = jnp.zeros((rows, 1), F32)
        o = jnp.zeros((rows, BRANCH_W), F32)
        for j in reversed(range(n_new)):
            seen = row_t > j
            log_beta, drop = _sb_logs(jnp.sum(q * kn_ref[j, s], axis=-1, keepdims=True) + bias, seen)
            o = o + jnp.where(seen, jnp.exp2(log_beta - c), 0.0) * vn_ref[j, s]
            c = c + drop
        qs.append(q)
        carry.append(c)
        out.append(o)

    cols = lambda a, p: a[:, p * page:(p + 1) * page]
    kt = [jnp.concatenate([flat(r) for r in k_refs[s]], axis=1) for s in samples]
    logs = [_sb_logs(_dot(qs[s].astype(BF16), kt[s]) + bias, None) for s in samples]
    stacked = [jnp.concatenate([cols(drop, p) for p in range(n_pages)], axis=0).astype(BF16) for _, drop in logs]
    upper = _upper_ones(page)
    local = [_dot(st, upper) for st in stacked]
    ws = []
    for s in samples:
        later, c = [None] * n_pages, carry[s]
        for p in reversed(range(n_pages)):
            loc = local[s][p * rows:(p + 1) * rows]
            later[p] = loc + c
            c = c + loc[:, 0:1] + cols(logs[s][1], p)[:, 0:1]
        ws.append(jnp.exp2(logs[s][0] - jnp.concatenate(later, axis=1)).astype(BF16))
    vt = [jnp.concatenate([flat(r) for r in v_refs[s]], axis=1) for s in samples]
    out = [out[s] + _dot_nt(ws[s], vt[s]) for s in samples]

    for s in samples:
        res = jnp.zeros((8, BRANCH_W), F32)
        for h in range(N_HEAD):
            res = res + jnp.where(lane_head == h, out[s][h * 8:(h + 1) * 8], 0.0)
        for t in range(n_new):
            o_ref[t, s] = res[t:t + 1]


def _attn_sample(layer, page_table, bias, q4, kn4, vn4, cache_kt, cache_vt):
    n_new, db, _, w = q4.shape
    n_pages = page_table.shape[1]
    page = cache_kt.shape[4]
    n_samp = SAMPLES_PER_STEP
    assert db % n_samp == 0
    new_spec = pl.BlockSpec((n_new, n_samp, 1, w), lambda g, pt, bs: (0, g, 0, 0))

    def page_spec(s, p):
        return pl.BlockSpec((None, None, N_HEAD, HEAD_W, page),
                            lambda g, pt, bs: (layer, pt[g * n_samp + s, p], 0, 0, 0))

    page_specs = [page_spec(s, p) for s in range(n_samp) for p in range(n_pages)]
    grid_spec = pltpu.PrefetchScalarGridSpec(
        num_scalar_prefetch=2,
        grid=(db // n_samp,),
        in_specs=[new_spec, new_spec, new_spec] + page_specs * 2,
        out_specs=new_spec,
    )
    n_ops = n_samp * n_pages
    return pl.pallas_call(
        functools.partial(_attn_sample_kernel, n_pages=n_pages, n_new=n_new, n_samp=n_samp),
        grid_spec=grid_spec,
        out_shape=jax.ShapeDtypeStruct(q4.shape, F32),
        compiler_params=_params("arbitrary"),
        name="attn_sample",
    )(page_table, bias, q4, kn4, vn4, *([cache_kt] * n_ops), *([cache_vt] * n_ops))


def _hgrn_gates(qb, fr, lb, loglb, log1m):
    c = log1m + _log_sigmoid(fr)
    log_f = jnp.maximum(loglb, c) + jnp.log1p(jnp.exp(-jnp.abs(loglb - c)))
    key = (1.0 - lb) * _sigmoid(-fr)
    return _silu(qb), key, log_f


def _hgrn_prompt_kernel(z_ref, lb_ref, loglb_ref, log1m_ref, gn_ref, o_ref, s_ref,
                        st_scr, q_scr, k_scr, v_scr, b_scr, p_scr, vx_scr, od_scr):
    ti = pl.program_id(0)
    w = BRANCH_W
    ck, sub = HGRN_CHUNK, HGRN_SUB
    n_sub = ck // sub
    nb, tl = z_ref.shape[0], z_ref.shape[1]
    rows = range(nb)

    @pl.when(ti == 0)
    def _():
        st_scr[...] = jnp.zeros_like(st_scr)

    bd = _head_block_ones()
    bd_mask = _head_id((w, w), 0) == _head_id((w, w), 1)
    r = lax.broadcasted_iota(jnp.int32, (ck, ck), 0)
    c = lax.broadcasted_iota(jnp.int32, (ck, ck), 1)
    lower_incl = jnp.where(c <= r, 1.0, 0.0).astype(BF16)
    row_ck = lax.broadcasted_iota(jnp.int32, (ck, 1), 0)
    row_sub = lax.broadcasted_iota(jnp.int32, (sub, 1), 0)
    stack_mask = (lax.shift_right_logical(lax.broadcasted_iota(jnp.int32, (N_HEAD * sub, w), 0), 4)
                  == _head_id((N_HEAD * sub, w), 1))
    lb, loglb, log1m, gn = lb_ref[...], loglb_ref[...], log1m_ref[...], gn_ref[...]
    stack = lambda a: jnp.where(stack_mask, jnp.concatenate([a] * N_HEAD, axis=0), 0.0).astype(BF16)

    def chunk(ci, _):
        r0 = pl.multiple_of(ci * ck, ck)
        zz = [z_ref[n, pl.ds(r0, ck), :] for n in rows]
        gates = [_hgrn_gates(z[:, 0:w], z[:, w:2 * w], lb, loglb, log1m) for z in zz]
        qh, key = [g[0] for g in gates], [g[1] for g in gates]
        val = [z[:, 2 * w:3 * w] for z in zz]
        parts = [_split3(g[2]) for g in gates]
        b = [_dot(lower_incl, hi) + _dot(lower_incl, mid) + _dot(lower_incl, lo)
             for hi, mid, lo in parts]
        for n in rows:
            q_scr[n] = qh[n]
            v_scr[n] = val[n]
            b_scr[n] = b[n] * LOG2E
            k_scr[n] = b[n] * LOG2E - jnp.log2(key[n])

        st = [st_scr[n] for n in rows]
        out = [_dot_nt((qh[n] * jnp.exp(b[n])).astype(BF16), st[n].astype(BF16)) for n in rows]
        for j in range(n_sub - 1):
            blk = slice(j * sub, (j + 1) * sub)
            e_j = [x[(j + 1) * sub - 1:(j + 1) * sub] for x in b]
            qj = [(qh[n] * jnp.exp(jnp.where(row_ck >= (j + 1) * sub, b[n] - e_j[n], NEG_BIG))).astype(BF16)
                  for n in rows]
            k_st = [stack(key[n][blk] * jnp.exp(e_j[n] - b[n][blk])) for n in rows]
            att = [_dot_nt(qj[n], k_st[n]).astype(BF16) for n in rows]
            out = [out[n] + _dot(att[n], stack(val[n][blk])) for n in rows]

        b_last = [x[ck - 1:ck] for x in b]
        upd = [_dot_tn(val[n].astype(BF16), (key[n] * jnp.exp(b_last[n] - b[n])).astype(BF16)) for n in rows]
        for n in rows:
            st_scr[n] = jnp.where(bd_mask, st[n] * jnp.exp(b_last[n]) + upd[n], 0.0)

        def diag(n, _):
            for si in range(n_sub):
                s0 = si * sub
                q_i = q_scr[n, s0:s0 + sub, :]
                b_i = b_scr[n, s0:s0 + sub, :]
                for s in range(sub):
                    c_s = k_scr[n, s0 + s:s0 + s + 1, :]
                    v_s = v_scr[n, s0 + s:s0 + s + 1, :]
                    dst = slice((s0 + s) * sub, (s0 + s + 1) * sub)
                    p_scr[dst, :] = (q_i * jnp.exp2(jnp.where(row_sub >= s, b_i - c_s, NEG_BIG))).astype(BF16)
                    vx_scr[dst, :] = jnp.broadcast_to(v_s, (sub, w))
            att = _dot(p_scr[...], bd)
            od_scr[n] = jnp.sum((att * vx_scr[...]).reshape(n_sub, sub, sub, w), axis=1).reshape(ck, w)
            return 0

        lax.fori_loop(0, nb, diag, 0)
        for n in rows:
            o_ref[n, pl.ds(r0, ck), :] = (_head_rms(out[n] + od_scr[n], gn, bd) * _silu(zz[n][:, 3 * w:4 * w]))
        return 0

    lax.fori_loop(0, tl // ck, chunk, 0)

    @pl.when(ti == pl.num_programs(0) - 1)
    def _():
        for n in rows:
            s_ref[n] = st_scr[n].T


def _hgrn_prompt(zb3, lb, loglb, log1m, gn):
    b, l, _ = zb3.shape
    w = BRANCH_W
    ck = HGRN_CHUNK
    tl = min(TL_HGRN, l)
    assert l % tl == 0 and tl % ck == 0
    fix = lambda j: (0, 0)
    scr = lambda rows, dt=F32: pltpu.VMEM((b, rows, w), dt)
    pairs = ck * HGRN_SUB
    return pl.pallas_call(
        _hgrn_prompt_kernel,
        grid=(l // tl,),
        in_specs=[pl.BlockSpec((b, tl, 4 * w), lambda j: (0, j, 0))] + [pl.BlockSpec((1, w), fix)] * 4,
        out_specs=[pl.BlockSpec((b, tl, w), lambda j: (0, j, 0)),
                   pl.BlockSpec((b, w, w), lambda j: (0, 0, 0))],
        out_shape=[jax.ShapeDtypeStruct((b, l, w), F32), jax.ShapeDtypeStruct((b, w, w), F32)],
        scratch_shapes=[scr(w), scr(ck), scr(ck), scr(ck), scr(ck),
                        pltpu.VMEM((pairs, w), BF16), pltpu.VMEM((pairs, w), F32), scr(ck)],
        compiler_params=_params("arbitrary"),
        name="hgrn_prompt",
    )(zb3, lb, loglb, log1m, gn)


def _hgrn_sample_kernel(z_ref, s0_ref, lb_ref, gn_ref, o_ref, s_ref, f_scr, k_scr, q_scr, v_scr, o_scr,
                        *, n_new, db):
    j = pl.program_id(0)
    w = BRANCH_W
    k_per_step = s0_ref.shape[0] // HEAD_W
    steps_per_head = HEAD_W // k_per_step
    lb = lb_ref[...]

    @pl.when(j == 0)
    def _():
        for t in range(n_new):
            zz = z_ref[t * db:(t + 1) * db, :]
            fr = zz[:, w:2 * w]
            f_scr[t] = (lb + (1.0 - lb) * _sigmoid(fr)).T
            k_scr[t] = ((1.0 - lb) * _sigmoid(-fr)).T
            q_scr[t] = _silu(zz[:, 0:w]).T
            v_scr[t] = zz[:, 2 * w:3 * w].T
            o_scr[t] = jnp.zeros((w, db), F32)

    head = j // steps_per_head
    k_base = head * HEAD_W + (j % steps_per_head) * k_per_step
    v0 = pl.multiple_of(head * HEAD_W, HEAD_W)
    for kk in range(k_per_step):
        rows = slice(kk * HEAD_W, (kk + 1) * HEAD_W)
        s_k = s0_ref[rows, :]
        row = k_base + kk
        for t in range(n_new):
            s_k = (f_scr[t, pl.ds(row, 1), :] * s_k
                   + k_scr[t, pl.ds(row, 1), :] * v_scr[t, pl.ds(v0, HEAD_W), :])
            o_scr[t, pl.ds(v0, HEAD_W), :] += q_scr[t, pl.ds(row, 1), :] * s_k
        s_ref[rows, :] = s_k

    @pl.when(j == pl.num_programs(0) - 1)
    def _():
        bd = _head_block_ones()
        for t in range(n_new):
            gate = _silu(z_ref[t * db:(t + 1) * db, 3 * w:4 * w])
            o_ref[t * db:(t + 1) * db, :] = _head_rms(o_scr[t].T, gn_ref[...], bd) * gate


def _hgrn_sample(layer, zb, state_t, lb, gn, n_new):
    t, _ = zb.shape
    db = t // n_new
    w = BRANCH_W
    n_state = state_t.shape[1]
    rows = 16 * HEAD_W
    fix = lambda j: (0, 0)
    scr = pltpu.VMEM((n_new, w, db), F32)
    return pl.pallas_call(
        functools.partial(_hgrn_sample_kernel, n_new=n_new, db=db),
        grid=(n_state // rows,),
        in_specs=[pl.BlockSpec((t, 4 * w), fix),
                  pl.BlockSpec((None, rows, db), lambda j: (layer, j, 0)),
                  pl.BlockSpec((1, w), fix), pl.BlockSpec((1, w), fix)],
        out_specs=[pl.BlockSpec((t, w), fix), pl.BlockSpec((rows, db), lambda j: (j, 0))],
        out_shape=[jax.ShapeDtypeStruct((t, w), F32), jax.ShapeDtypeStruct((n_state, db), F32)],
        scratch_shapes=[scr] * 5,
        compiler_params=_params("arbitrary"),
        name="hgrn_sample",
    )(zb, state_t, lb, gn)


def _rglru_gates(xconv, wg, bg, sp_lam):
    w = BRANCH_W
    g = _sigmoid(_dot(xconv.astype(BF16), wg) + bg)
    log_a = -RG_C * g[:, 0:w] * sp_lam
    a = jnp.exp(log_a)
    one_minus_a2 = -jnp.tanh(log_a) * (a * a + 1.0)
    u = jnp.sqrt(one_minus_a2) * (g[:, w:2 * w] * xconv)
    return a, u


def _rglru_prompt_kernel(z_ref, cw_ref, cb_ref, wg_ref, bg_ref, lam_ref, o_ref, h_ref, ext_scr, h_scr):
    ti = pl.program_id(1)
    w = BRANCH_W
    tl = z_ref.shape[1]
    halo = 8

    @pl.when(ti == 0)
    def _():
        ext_scr[0:halo, :] = jnp.zeros((halo, w), F32)
        h_scr[...] = jnp.zeros_like(h_scr)

    x = z_ref[0, :, 0:w]
    ext_scr[halo:, :] = x
    xconv = cb_ref[...] + cw_ref[CONV_C - 1:CONV_C, :] * x
    for j in range(CONV_C - 1):
        xconv = xconv + cw_ref[j:j + 1, :] * ext_scr[pl.ds(halo - (CONV_C - 1) + j, tl), :]
    ext_scr[0:halo, :] = x[tl - halo:tl]
    a, u = _rglru_gates(xconv, wg_ref[...], bg_ref[...], _softplus(-lam_ref[...]))
    d = 1
    while d < tl:
        u = u + a * _shift_rows(u, d, 0.0)
        a = a * _shift_rows(a, d, 1.0)
        d *= 2
    h = u + a * h_scr[0:1, :]
    h_last = h[tl - 1:tl]
    h_scr[...] = jnp.broadcast_to(h_last, h_scr.shape)
    h_ref[0] = h_last
    o_ref[0] = h * _gelu(z_ref[0, :, w:2 * w])


def _rglru_prompt(zc3, cw, cb, wg, bg, lam):
    b, l, _ = zc3.shape
    w = BRANCH_W
    tl = min(TL_SCAN, l)
    fix = lambda i, j: (0, 0)
    return pl.pallas_call(
        _rglru_prompt_kernel,
        grid=(b, l // tl),
        in_specs=[pl.BlockSpec((1, tl, 2 * w), lambda i, j: (i, j, 0)),
                  pl.BlockSpec((CONV_C, w), fix), pl.BlockSpec((1, w), fix),
                  pl.BlockSpec((w, 2 * w), fix), pl.BlockSpec((1, 2 * w), fix), pl.BlockSpec((1, w), fix)],
        out_specs=[pl.BlockSpec((1, tl, w), lambda i, j: (i, j, 0)),
                   pl.BlockSpec((1, 1, w), lambda i, j: (i, 0, 0))],
        out_shape=[jax.ShapeDtypeStruct((b, l, w), F32), jax.ShapeDtypeStruct((b, 1, w), F32)],
        scratch_shapes=[pltpu.VMEM((tl + 8, w), F32), pltpu.VMEM((8, w), F32)],
        compiler_params=_params("parallel", "arbitrary"),
        name="rglru_prompt",
    )(zc3, cw, cb, wg, bg, lam)


def _pool_select(sums, x, pos, wp, scale):
    lane_group = _head_id((1, BRANCH_W), 1)
    pooled = jnp.zeros_like(x)
    for g, win in enumerate(POOL_WINDOWS):
        cnt = jnp.minimum(pos + 1, win).astype(F32)
        pooled = jnp.where(lane_group == g, sums[g] / cnt, pooled)
    return _dot((pooled - x).astype(BF16), wp) * scale


def _pool_prompt_kernel(x_ref, wp_ref, sc_ref, o_ref, ext_scr):
    ti = pl.program_id(1)
    w = BRANCH_W
    tl = x_ref.shape[1]
    halo = 16

    @pl.when(ti == 0)
    def _():
        ext_scr[0:halo, :] = jnp.zeros((halo, w), F32)

    x = x_ref[0]
    ext_scr[halo:, :] = x
    e = ext_scr[...]
    sums = []
    d = 1
    for _ in POOL_WINDOWS:
        e = e + pltpu.roll(e, d, 0)
        sums.append(e[halo:])
        d *= 2
    ext_scr[0:halo, :] = x[tl - halo:tl]
    pos = ti * tl + lax.broadcasted_iota(jnp.int32, (tl, 1), 0)
    o_ref[0] = _pool_select(sums, x, pos, wp_ref[...], sc_ref[...])


def _pool_prompt(zd3, wp, scale):
    b, l, w = zd3.shape
    tl = min(TL_SCAN, l)
    fix = lambda i, j: (0, 0)
    return pl.pallas_call(
        _pool_prompt_kernel,
        grid=(b, l // tl),
        in_specs=[pl.BlockSpec((1, tl, w), lambda i, j: (i, j, 0)),
                  pl.BlockSpec((w, w), fix), pl.BlockSpec((1, w), fix)],
        out_specs=pl.BlockSpec((1, tl, w), lambda i, j: (i, j, 0)),
        out_shape=jax.ShapeDtypeStruct((b, l, w), F32),
        scratch_shapes=[pltpu.VMEM((tl + 16, w), F32)],
        compiler_params=_params("parallel", "arbitrary"),
        name="pool_prompt",
    )(zd3, wp, scale)


def _cd_sample_kernel(zc_ref, zd_ref, h0_ref, ch_ref, ph_ref, cw_ref, cb_ref, wg_ref, bg_ref, lam_ref,
                      wp_ref, sc_ref, oc_ref, od_ref, h_ref, cn_ref, pn_ref, xc_scr, *, n_new, db, pos0):
    w = BRANCH_W
    slab = lambda ref, t, c0=0: ref[t * db:(t + 1) * db, c0:c0 + w]
    hist = lambda ref, j: ref[j]

    n_hist = CONV_C - 1
    ext = [hist(ch_ref, j) for j in range(n_hist)] + [slab(zc_ref, t) for t in range(n_new)]
    for t in range(n_new):
        acc = cb_ref[...] + cw_ref[0:1, :] * ext[t]
        for j in range(1, CONV_C):
            acc = acc + cw_ref[j:j + 1, :] * ext[t + j]
        xc_scr[t * db:(t + 1) * db, :] = acc
    a, u = _rglru_gates(xc_scr[...], wg_ref[...], bg_ref[...], _softplus(-lam_ref[...]))
    h = h0_ref[...]
    for t in range(n_new):
        h = a[t * db:(t + 1) * db] * h + u[t * db:(t + 1) * db]
        oc_ref[t * db:(t + 1) * db, :] = h * _gelu(slab(zc_ref, t, w))
    h_ref[...] = h
    for j in range(n_hist):
        cn_ref[j] = ext[n_new + j]

    pext = [hist(ph_ref, j) for j in range(POOL_HIST)] + [slab(zd_ref, t) for t in range(n_new)]
    for t in range(n_new):
        sums, run, k = [], None, 0
        for win in POOL_WINDOWS:
            while k < win:
                term = pext[POOL_HIST + t - k]
                run = term if run is None else run + term
                k += 1
            sums.append(run)
        pos = jnp.full((db, 1), pos0 + t, jnp.int32)
        od_ref[t * db:(t + 1) * db, :] = _pool_select(sums, pext[POOL_HIST + t], pos, wp_ref[...], sc_ref[...])
    for j in range(POOL_HIST):
        pn_ref[j] = pext[n_new + j]


def _cd_sample(zc, zd, h0, conv_hist, pool_hist, cw, cb, wg, bg, lam, wp, scale, n_new, pos0):
    t, _ = zc.shape
    db = t // n_new
    w = BRANCH_W
    shp = lambda c: jax.ShapeDtypeStruct((db, c), F32)
    return pl.pallas_call(
        functools.partial(_cd_sample_kernel, n_new=n_new, db=db, pos0=pos0),
        out_shape=[jax.ShapeDtypeStruct((t, w), F32), jax.ShapeDtypeStruct((t, w), F32),
                   shp(w), jax.ShapeDtypeStruct((CONV_C - 1, db, w), F32),
                   jax.ShapeDtypeStruct((POOL_HIST, db, w), F32)],
        scratch_shapes=[pltpu.VMEM((t, w), F32)],
        compiler_params=pltpu.CompilerParams(vmem_limit_bytes=V7X_VMEM_LIMIT_BYTES),
        name="rglru_pool_sample",
    )(zc, zd, h0, conv_hist, pool_hist, cw, cb, wg, bg, lam, wp, scale)


def _merge_kernel(x_ref, oa_ref, ob_ref, oc_ref, od_ref, g_ref, wg_ref, wb_ref, wo_ref, o_ref):
    x = x_ref[...]
    d = x.shape[1]
    h = _rms(x, g_ref[...]).astype(BF16)
    mix = jnp.zeros(x.shape, F32)
    for n, br in enumerate((oa_ref, ob_ref, oc_ref, od_ref)):
        gate = _sigmoid(_dot(h, wg_ref[:, n * d:(n + 1) * d]))
        mix = mix + gate * _dot(br[...].astype(BF16), wb_ref[n])
    o_ref[...] = x + _dot(mix.astype(BF16), wo_ref[...])


def _merge(layer, x2, oa, ob, oc, od, g, wg, wb, wo):
    t, d = x2.shape
    tm = min(TM_PROJ, t)
    w = BRANCH_W
    row = lambda i: (i, 0)
    fix = lambda i: (0, 0)
    return pl.pallas_call(
        _merge_kernel,
        grid=(t // tm,),
        in_specs=[pl.BlockSpec((tm, d), row)] + [pl.BlockSpec((tm, w), row)] * 4
                 + [pl.BlockSpec((1, d), fix), pl.BlockSpec((d, N_BRANCH * d), fix),
                    pl.BlockSpec((None, N_BRANCH, w, d), lambda i: (layer, 0, 0, 0)),
                    pl.BlockSpec((None, d, d), lambda i: (layer, 0, 0))],
        out_specs=pl.BlockSpec((tm, d), row),
        out_shape=jax.ShapeDtypeStruct((t, d), F32),
        compiler_params=_params("parallel"),
        name="merge",
    )(x2, oa, ob, oc, od, g, wg, wb, wo)


def _ple(x, pe, gp, wpg, wp):
    gate = _sigmoid(_dot(_rms(x, gp).astype(BF16), wpg))
    return x + gate * _dot(pe.astype(BF16), wp)


def _ffn_prompt_kernel(x_ref, xp_ref, g_ref, wu_ref, cw_ref, cb_ref, wd_ref, pe_ref, gp_ref, wpg_ref, wp_ref,
                       o_ref, s_ref, e_scr, gated_scr, *, tiles_per_seq):
    i = pl.program_id(0)
    tm = x_ref.shape[0]
    halo = FFN_HALO
    dff = wd_ref.shape[0]
    tf = e_scr.shape[3]
    x = x_ref[...]
    xn = _rms(x, g_ref[...]).astype(BF16)
    xnp = _rms(xp_ref[...], g_ref[...]).astype(BF16)
    seq_start = (i % tiles_per_seq) == 0

    def conv_half(e_ref, c0):
        cols = slice(c0, c0 + tf)
        w = wu_ref[:, cols]
        e_ref[0:halo, :] = jnp.where(seq_start, 0.0, _dot(xnp, w))
        e_ref[halo:, :] = _dot(xn, w)
        s_ref[0, :, cols] = e_ref[pl.ds(halo + tm - (CONV_F - 1), CONV_F - 1), :]
        y = cb_ref[:, cols]
        for j in range(CONV_F):
            y = y + cw_ref[j:j + 1, cols] * e_ref[pl.ds(halo - (CONV_F - 1) + j, tm), :]
        return y

    n_chunks = dff // tf
    x2 = x
    done = 0
    for c in range(n_chunks + 1):
        if c < n_chunks:
            slot = c % 2
            ya = conv_half(e_scr.at[slot, 0], c * tf)
            yb = conv_half(e_scr.at[slot, 1], dff + c * tf)
        if c == n_chunks or (c - done > FFN_DOWN_GROUP):
            stop = c if c == n_chunks else c - 1
            rows = slice(done * tf, stop * tf)
            x2 = x2 + _dot(gated_scr[:, rows], wd_ref[rows, :])
            done = stop
        if c < n_chunks:
            gated_scr[:, c * tf:(c + 1) * tf] = (_gelu(ya) * yb).astype(BF16)
    o_ref[...] = _ple(x2, pe_ref[...], gp_ref[...], wpg_ref[...], wp_ref[...])


def _ffn_prompt(layer, x2, seq_len, g, w_up, cw, cb, w_down, pe3, gp, wpg, wp):
    t, d = x2.shape
    dff = w_down.shape[1]
    tm = min(TM_FFN, seq_len)
    tf = TF_FFN
    halo = FFN_HALO
    assert seq_len % tm == 0 and dff % tf == 0 and tm % halo == 0
    tps = seq_len // tm
    dp = pe3.shape[2]
    fix = lambda i: (0, 0)
    once = pl.Buffered(1)
    weight = lambda shape: pl.BlockSpec((None,) + shape, lambda i: (layer, 0, 0), pipeline_mode=once)
    return pl.pallas_call(
        functools.partial(_ffn_prompt_kernel, tiles_per_seq=tps),
        grid=(t // tm,),
        in_specs=[pl.BlockSpec((tm, d), lambda i: (i, 0)),
                  pl.BlockSpec((halo, d), lambda i: (jnp.maximum(i * (tm // halo) - 1, 0), 0)),
                  pl.BlockSpec((1, d), fix),
                  weight((d, 2 * dff)),
                  pl.BlockSpec((CONV_F, 2 * dff), fix), pl.BlockSpec((1, 2 * dff), fix),
                  weight((dff, d)),
                  pl.BlockSpec((None, tm, dp), lambda i: (layer, i, 0)),
                  pl.BlockSpec((1, d), fix), weight((d, d)), weight((dp, d))],
        out_specs=[pl.BlockSpec((tm, d), lambda i: (i, 0)),
                   pl.BlockSpec((1, CONV_F - 1, 2 * dff), lambda i: (i, 0, 0))],
        out_shape=[jax.ShapeDtypeStruct((t, d), F32), jax.ShapeDtypeStruct((t // tm, CONV_F - 1, 2 * dff), F32)],
        scratch_shapes=[pltpu.VMEM((2, 2, tm + halo, tf), F32), pltpu.VMEM((tm, dff), BF16)],
        compiler_params=_params("arbitrary"),
        name="ffn_prompt",
    )(x2, x2, g, w_up, cw, cb, w_down, pe3, gp, wpg, wp)


def _ffn_sample_kernel(x_ref, g_ref, wa_ref, wb_ref, cwa_ref, cwb_ref, cba_ref, cbb_ref, wd_ref,
                       ha_ref, hb_ref, pe_ref, gp_ref, wpg_ref, wp_ref,
                       o_ref, sa_ref, sb_ref, xn_scr, acc_scr, g_scr, *, n_new, db):
    f = pl.program_id(0)

    @pl.when(f == 0)
    def _():
        xn_scr[...] = _rms(x_ref[...], g_ref[...]).astype(BF16)
        acc_scr[...] = jnp.zeros_like(acc_scr)

    def conv_half(w_ref, cw_ref, cb_ref, h_ref, s_ref):
        u = _dot(xn_scr[...], w_ref[...])
        ext = [h_ref[:, j, :] for j in range(CONV_F - 1)] + [u[t * db:(t + 1) * db] for t in range(n_new)]
        for j in range(CONV_F - 1):
            s_ref[:, j, :] = ext[n_new + j]
        ys = []
        for t in range(n_new):
            y = cb_ref[...] + cw_ref[0:1, :] * ext[t]
            for j in range(1, CONV_F):
                y = y + cw_ref[j:j + 1, :] * ext[t + j]
            ys.append(y)
        return ys

    ya = conv_half(wa_ref, cwa_ref, cba_ref, ha_ref, sa_ref)
    yb = conv_half(wb_ref, cwb_ref, cbb_ref, hb_ref, sb_ref)
    for t in range(n_new):
        g_scr[t * db:(t + 1) * db, :] = (_gelu(ya[t]) * yb[t]).astype(BF16)
    acc_scr[...] += _dot(g_scr[...], wd_ref[...])

    @pl.when(f == pl.num_programs(0) - 1)
    def _():
        o_ref[...] = _ple(x_ref[...] + acc_scr[...], pe_ref[...], gp_ref[...], wpg_ref[...], wp_ref[...])


def _ffn_sample(layer, x2, g, w_up, cw, cb, w_down, hist4, pe3, gp, wpg, wp, n_new):
    t, d = x2.shape
    db = t // n_new
    dff = w_down.shape[1]
    tf = TF_FFN
    nf = dff // tf
    dp = pe3.shape[2]
    nh = CONV_F - 1
    fix = lambda f: (0, 0)
    hist = lambda half: pl.BlockSpec((None, db, nh, tf), lambda f: (layer, 0, 0, f + half * nf))
    st_shape = jax.ShapeDtypeStruct((db, nh, dff), F32)
    st_spec = pl.BlockSpec((db, nh, tf), lambda f: (0, 0, f))
    return pl.pallas_call(
        functools.partial(_ffn_sample_kernel, n_new=n_new, db=db),
        grid=(nf,),
        in_specs=[pl.BlockSpec((t, d), fix), pl.BlockSpec((1, d), fix),
                  pl.BlockSpec((None, d, tf), lambda f: (layer, 0, f)),
                  pl.BlockSpec((None, d, tf), lambda f: (layer, 0, f + nf)),
                  pl.BlockSpec((CONV_F, tf), lambda f: (0, f)), pl.BlockSpec((CONV_F, tf), lambda f: (0, f + nf)),
                  pl.BlockSpec((1, tf), lambda f: (0, f)), pl.BlockSpec((1, tf), lambda f: (0, f + nf)),
                  pl.BlockSpec((None, tf, d), lambda f: (layer, f, 0)),
                  hist(0), hist(1),
                  pl.BlockSpec((None, t, dp), lambda f: (layer, 0, 0)),
                  pl.BlockSpec((1, d), fix), pl.BlockSpec((None, d, d), lambda f: (layer, 0, 0)),
                  pl.BlockSpec((None, dp, d), lambda f: (layer, 0, 0))],
        out_specs=[pl.BlockSpec((t, d), fix), st_spec, st_spec],
        out_shape=[jax.ShapeDtypeStruct((t, d), F32), st_shape, st_shape],
        scratch_shapes=[pltpu.VMEM((t, d), BF16), pltpu.VMEM((t, d), F32), pltpu.VMEM((t, tf), BF16)],
        compiler_params=_params("arbitrary"),
        name="ffn_sample",
    )(x2, g, w_up, w_up, cw, cw, cb, cb, w_down, hist4, hist4, pe3, gp, wpg, wp)


def _block_diag(w4):
    h, n, _ = w4.shape
    eye = jnp.eye(h, dtype=w4.dtype)
    return (eye[:, None, :, None] * w4[:, :, None, :]).reshape(h * n, h * n)


def kernel(x_prompt, x_sample, cache_k, cache_v, state_hgrn, state_rglru_h, state_rglru_conv, state_pool,
           state_ffn_conv, page_table, p_prompt, p_sample, norm_mix, w_in, q_norm, k_norm, sb_bias, lb_logits,
           hgrn_norm, conv_c_w, conv_c_b, w_rg_a, b_rg_a, w_rg_x, b_rg_x, lam, w_pool, pool_scale, w_branch,
           w_out, norm_ffn, w_up, conv_f_w, conv_f_b, w_down, norm_ple, w_ple_gate, w_ple):
    depth = w_in.shape[0]
    b, l, d = x_prompt.shape
    db, n_new, _ = x_sample.shape
    w = BRANCH_W
    n_mix = 10 * w
    dff = w_down.shape[1]
    dp = p_prompt.shape[-1]
    n_pool, page = cache_k.shape[1], cache_k.shape[2]
    pos0 = page_table.shape[1] * page

    lb_all, loglb_all, log1m_all = _lower_bounds(lb_logits.astype(F32))
    row = lambda a, i: a[i].reshape(1, -1)
    tile_heads = lambda a, i: jnp.tile(a[i], N_HEAD).reshape(1, w)

    xp = x_prompt.reshape(b * l, d)
    xs = x_sample.transpose(1, 0, 2).reshape(n_new * db, d)
    pe_p = p_prompt.reshape(depth, b * l, dp)
    pe_s = p_sample.transpose(0, 2, 1, 3).reshape(depth, n_new * db, dp)
    cache_kt = cache_k.transpose(0, 1, 3, 4, 2)
    cache_vt = cache_v.transpose(0, 1, 3, 4, 2)
    hgrn_state = state_hgrn.transpose(0, 2, 3, 4, 1).reshape(depth, -1, db)
    conv_state = state_rglru_conv.transpose(0, 2, 1, 3)
    pool_state = state_pool.transpose(0, 2, 1, 3)

    w_br, w_o = w_branch.astype(BF16), w_out.astype(BF16)
    w_u, w_d = w_up.astype(BF16), w_down.astype(BF16)
    w_pg, w_pe = w_ple_gate.astype(BF16), w_ple.astype(BF16)

    outs = {k: [] for k in ("kp", "vp", "ks", "vs", "sp", "ss", "hp", "hs", "cp", "cs", "pp", "ps", "fp", "fs")}
    for i in range(depth):
        w_mix = w_in[i, :, :n_mix].astype(BF16)
        w_gate = w_in[i, :, n_mix:].astype(BF16)
        wg_c = jnp.concatenate([_block_diag(w_rg_a[i]), _block_diag(w_rg_x[i])], axis=1).astype(BF16)
        bg_c = jnp.concatenate([b_rg_a[i], b_rg_x[i]]).reshape(1, 2 * w)
        wp_d = _block_diag(w_pool[i]).astype(BF16)
        g_mix, g_ffn, g_ple = row(norm_mix, i), row(norm_ffn, i), row(norm_ple, i)
        qn, kn, gn = tile_heads(q_norm, i), tile_heads(k_norm, i), tile_heads(hgrn_norm, i)
        lb, loglb, log1m = row(lb_all, i), row(loglb_all, i), row(log1m_all, i)
        cw_c, cb_c, lam_i, sc_d = conv_c_w[i], row(conv_c_b, i), row(lam, i), row(pool_scale, i)
        cw_f, cb_f = conv_f_w[i], row(conv_f_b, i)
        bias = sb_bias[i].astype(F32)

        q, kt, vt, ktb, vtb, zb, zc, zd = _inproj_prompt(xp.reshape(b, l, d), g_mix, w_mix, qn, kn)
        o_a = _attn_prompt(bias, q, ktb, vtb)
        o_b, s_p = _hgrn_prompt(zb, lb, loglb, log1m, gn)
        o_c, h_p = _rglru_prompt(zc, cw_c, cb_c, wg_c, bg_c, lam_i)
        o_d = _pool_prompt(zd, wp_d, sc_d)
        f2 = lambda a: a.reshape(b * l, w)
        x1 = _merge(i, xp, f2(o_a), f2(o_b), f2(o_c), f2(o_d), g_mix, w_gate, w_br, w_o)
        xp, f_p = _ffn_prompt(i, x1, l, g_ffn, w_u, cw_f, cb_f, w_d, pe_p, g_ple, w_pg, w_pe)
        outs["kp"].append(kt)
        outs["vp"].append(vt)
        s_heads = s_p.reshape(b, N_HEAD, HEAD_W, N_HEAD, HEAD_W)
        outs["sp"].append(jnp.stack([s_heads[:, h, :, h, :] for h in range(N_HEAD)], axis=1))
        outs["hp"].append(h_p.reshape(b, w))
        outs["cp"].append(zc[:, l - (CONV_C - 1):, :w])
        outs["pp"].append(zd[:, l - POOL_HIST:, :])
        tps = f_p.shape[0] // b
        outs["fp"].append(f_p[tps - 1::tps])

        q, k, v, zb, zc, zd = _inproj_sample(xs, g_mix, w_mix, qn, kn)
        r4 = lambda a: a.reshape(n_new, db, 1, w)
        o_a = _attn_sample(i, page_table, bias, r4(q), r4(k), r4(v), cache_kt, cache_vt).reshape(n_new * db, w)
        o_b, s_s = _hgrn_sample(i, zb, hgrn_state, lb, gn, n_new)
        o_c, o_d, h_s, c_s, p_s = _cd_sample(zc, zd, state_rglru_h[i], conv_state[i], pool_state[i],
                                             cw_c, cb_c, wg_c, bg_c, lam_i, wp_d, sc_d, n_new, pos0)
        x1 = _merge(i, xs, o_a, o_b, o_c, o_d, g_mix, w_gate, w_br, w_o)
        xs, fa, fb = _ffn_sample(i, x1, g_ffn, w_u, cw_f, cb_f, w_d, state_ffn_conv, pe_s,
                                 g_ple, w_pg, w_pe, n_new)
        outs["ks"].append(k)
        outs["vs"].append(v)
        outs["ss"].append(s_s)
        outs["hs"].append(h_s)
        outs["cs"].append(c_s)
        outs["ps"].append(p_s)
        outs["fs"].append(jnp.concatenate([fa, fb], axis=-1))

    stk = lambda key: jnp.stack(outs[key], axis=0)
    to_batch_major = lambda a: a.reshape(depth, n_new, db, N_HEAD, HEAD_W).transpose(0, 2, 1, 3, 4)
    y_sample = xs.reshape(n_new, db, d).transpose(1, 0, 2)
    from_transposed = lambda a: a.reshape(depth, b, N_HEAD, HEAD_W, l).transpose(0, 1, 4, 2, 3)
    hgrn_s = stk("ss").reshape(depth, N_HEAD, HEAD_W, HEAD_W, db).transpose(0, 4, 1, 2, 3)
    rows_to_batch = lambda a: a.transpose(0, 2, 1, 3)
    return (xp.reshape(b, l, d), y_sample,
            from_transposed(stk("kp")), from_transposed(stk("vp")),
            to_batch_major(stk("ks")), to_batch_major(stk("vs")),
            stk("sp"), hgrn_s, stk("hp"), stk("hs"), stk("cp"), rows_to_batch(stk("cs")),
            stk("pp"), rows_to_batch(stk("ps")), stk("fp"), stk("fs"))
```
